```python
import math
import jax, jax.numpy as jnp
from jax import lax
import numpy as np

D_MODEL = 1024
BATCH = 16
SEQ = 256
DEPTH = 2
DEC_BATCH = 2
DEC_SEQ = 4096
PAST_LEN = 256

GRID_W = 64
Q_BLOCK = 128
D_MIX = D_MODEL
DIFF_HEADS = 4
DIFF_HEAD_DIM = 32
DIFF_V_DIM = 2 * DIFF_HEAD_DIM
DIFF_W = DIFF_HEADS * DIFF_V_DIM
S5_CH = 16
S5_W = D_MIX // 4
S5_GROUPS = S5_W // S5_CH
S5_STATE = 64
MLA_HEADS = 8
MLA_NOPE = 64
MLA_ROPE = 32
MLA_V = 64
MLA_Q_RANK = D_MODEL // 4
MLA_KV_RANK = D_MODEL // 8
MLA_W = MLA_HEADS * MLA_V
FFN_HIDDEN = ((8 * D_MODEL + 3 * 256 - 1) // (3 * 256)) * 256
IN_SIZES = (DIFF_HEADS * 2 * DIFF_HEAD_DIM, DIFF_HEADS * 2 * DIFF_HEAD_DIM, DIFF_W,
            S5_W, MLA_Q_RANK, MLA_KV_RANK, MLA_ROPE)
IN_W = sum(IN_SIZES)
IN_OFFSETS = tuple(sum(IN_SIZES[:i + 1]) for i in range(len(IN_SIZES) - 1))
ALPHA = (2 * DEPTH) ** 0.25
BETA = (8 * DEPTH) ** -0.25
LN_EPS = 1e-5
RMS_EPS = 1e-6
ROPE_BASE = 10000.0

kernel_name = "hybrid_diff_s5_mla_prefix_dit_step"

F32 = jnp.float32


def layer_norm(x, g, b):
    xf = x.astype(F32)
    mu = jnp.mean(xf, -1, keepdims=True)
    var = jnp.mean(jnp.square(xf - mu), -1, keepdims=True)
    return ((xf - mu) * lax.rsqrt(var + LN_EPS) * g.astype(F32) + b.astype(F32)).astype(x.dtype)


def rms_norm(x, g):
    xf = x.astype(F32)
    y = xf * lax.rsqrt(jnp.mean(xf * xf, -1, keepdims=True) + RMS_EPS) * g.astype(F32)
    return y.astype(x.dtype)


def grid_rope_tables(length, dim):
    rows = length // GRID_W
    row = jnp.repeat(jnp.arange(rows, dtype=F32), GRID_W)
    col = jnp.tile(jnp.arange(GRID_W, dtype=F32), rows)
    n_freq = dim // 4
    inv = ROPE_BASE ** (-jnp.arange(n_freq, dtype=F32) / n_freq)
    ang = jnp.concatenate([row[:, None] * inv, col[:, None] * inv], -1)
    return jnp.cos(ang), jnp.sin(ang)


def apply_rope(x, cos, sin):
    half = x.shape[-1] // 2
    xf = x.astype(F32)
    x1, x2 = xf[..., :half], xf[..., half:]
    shp = (1, cos.shape[0]) + (1,) * (x.ndim - 3) + (half,)
    cs, sn = cos.reshape(shp), sin.reshape(shp)
    return jnp.concatenate([x1 * cs - x2 * sn, x2 * cs + x1 * sn], -1).astype(x.dtype)


def over_query_blocks(fn, *qs):
    b, length = qs[0].shape[:2]
    nb = length // Q_BLOCK
    blocks = tuple(jnp.swapaxes(q.reshape((b, nb, Q_BLOCK) + q.shape[2:]), 0, 1) for q in qs)
    out = lax.map(lambda blk: fn(*blk), blocks)
    out = jnp.swapaxes(out, 0, 1)
    return out.reshape((b, length) + out.shape[3:])


def diff_attention(q, k, v, lam, norm_g, lam_init):
    scale = DIFF_HEAD_DIM ** -0.5

    def block(qb):
        s = jnp.einsum('bqhcd,bkhcd->bhcqk', qb, k).astype(F32) * scale
        p = jax.nn.softmax(s, axis=-1)
        a = p[:, :, 0] - lam * p[:, :, 1]
        return jnp.einsum('bhqk,bkhd->bqhd', a.astype(v.dtype), v)

    o = over_query_blocks(block, q)
    o = rms_norm(o, norm_g) * (1.0 - lam_init)
    return o.reshape(o.shape[0], o.shape[1], DIFF_W)


def mla_attention(q_nope, q_rope, k_nope, k_rope, v):
    scale = (MLA_NOPE + MLA_ROPE) ** -0.5

    def block(qn, qr):
        s = (jnp.einsum('bqhd,bkhd->bhqk', qn, k_nope)
             + jnp.einsum('bqhd,bkd->bhqk', qr, k_rope)).astype(F32) * scale
        p = jax.nn.softmax(s, axis=-1)
        return jnp.einsum('bhqk,bkhd->bqhd', p.astype(v.dtype), v)

    o = over_query_blocks(block, q_nope, q_rope)
    return o.reshape(o.shape[0], o.shape[1], MLA_W)


def s5_discretize(lam_re, lam_im, log_dt, b_re, b_im):
    lam_re, lam_im = lam_re.astype(F32), lam_im.astype(F32)
    b_re, b_im = b_re.astype(F32), b_im.astype(F32)
    dt = jnp.exp(log_dt.astype(F32))[:, None]
    mag = jnp.exp(lam_re * dt)
    ang = lam_im * dt
    a_re, a_im = mag * jnp.cos(ang), mag * jnp.sin(ang)
    den = lam_re * lam_re + lam_im * lam_im
    n_re, n_im = a_re - 1.0, a_im
    f_re = ((n_re * lam_re + n_im * lam_im) / den)[..., None]
    f_im = ((n_im * lam_re - n_re * lam_im) / den)[..., None]
    return a_re, a_im, f_re * b_re - f_im * b_im, f_re * b_im + f_im * b_re


def _complex_affine_combine(e1, e2):
    a1r, a1i, b1r, b1i = e1
    a2r, a2i, b2r, b2i = e2
    return (a2r * a1r - a2i * a1i, a2r * a1i + a2i * a1r,
            a2r * b1r - a2i * b1i + b2r, a2r * b1i + a2i * b1r + b2i)


def diag_scan(a_re, a_im, bu_re, bu_im, h0):
    if h0 is not None:
        h0_re, h0_im = h0
        bu_re = bu_re.at[:, 0].add(a_re * h0_re - a_im * h0_im)
        bu_im = bu_im.at[:, 0].add(a_re * h0_im + a_im * h0_re)
    ar = jnp.broadcast_to(a_re, bu_re.shape)
    ai = jnp.broadcast_to(a_im, bu_im.shape)
    _, _, h_re, h_im = lax.associative_scan(_complex_affine_combine, (ar, ai, bu_re, bu_im), axis=1)
    return h_re, h_im


def s5_mixer(u, lam_re, lam_im, log_dt, b_re, b_im, c_re, c_im, d_skip, w_glu, h0):
    bsz, length = u.shape[:2]
    uf = u.astype(F32).reshape(bsz, length, S5_GROUPS, S5_CH)
    y = uf * d_skip.astype(F32)
    finals = []
    for d in range(2):
        ar, ai, bbr, bbi = s5_discretize(lam_re[d], lam_im[d], log_dt[d], b_re[d], b_im[d])
        bu_re = jnp.einsum('blgh,gph->blgp', uf, bbr)
        bu_im = jnp.einsum('blgh,gph->blgp', uf, bbi)
        if d == 1:
            bu_re, bu_im = bu_re[:, ::-1], bu_im[:, ::-1]
        init = None if h0 is None else (h0[:, d, :, :, 0].astype(F32), h0[:, d, :, :, 1].astype(F32))
        h_re, h_im = diag_scan(ar, ai, bu_re, bu_im, init)
        if h0 is None:
            finals.append(jnp.stack([h_re[:, -1], h_im[:, -1]], -1))
        if d == 1:
            h_re, h_im = h_re[:, ::-1], h_im[:, ::-1]
        y = y + (jnp.einsum('gnp,blgp->blgn', c_re[d].astype(F32), h_re)
                 - jnp.einsum('gnp,blgp->blgn', c_im[d].astype(F32), h_im))
    y = jax.nn.gelu(y.reshape(bsz, length, S5_W))
    y = y * jax.nn.sigmoid(y @ w_glu.astype(F32))
    final = jnp.stack(finals, 1).astype(u.dtype) if h0 is None else None
    return y.astype(u.dtype), final


def token_mixer(h, lw, lam, lam_init, ctx):
    bsz, length, _ = h.shape
    z = h @ lw['w_in']
    dq, dk, dv, u, q_lat, kv_lat, k_rope = jnp.split(z, list(IN_OFFSETS), axis=-1)
    dq = dq.reshape(bsz, length, DIFF_HEADS, 2, DIFF_HEAD_DIM)
    dk = dk.reshape(bsz, length, DIFF_HEADS, 2, DIFF_HEAD_DIM)
    dv = dv.reshape(bsz, length, DIFF_HEADS, DIFF_V_DIM)
    ckv = rms_norm(kv_lat, lw['mla_kv_norm_g'])
    q = (rms_norm(q_lat, lw['mla_q_norm_g']) @ lw['mla_w_uq']).reshape(
        bsz, length, MLA_HEADS, MLA_NOPE + MLA_ROPE)
    q_nope, q_rope = q[..., :MLA_NOPE], q[..., MLA_NOPE:]
    if ctx is None:
        dk_all, dv_all, ckv_all, kr_all, h0 = dk, dv, ckv, k_rope, None
    else:
        c_k, c_v, c_ckv, c_kr, h0 = ctx
        cos_d, sin_d = grid_rope_tables(length, DIFF_HEAD_DIM)
        dq = apply_rope(dq, cos_d, sin_d)
        dk = apply_rope(dk, cos_d, sin_d)
        cos_m, sin_m = grid_rope_tables(length, MLA_ROPE)
        q_rope = apply_rope(q_rope, cos_m, sin_m)
        k_rope = apply_rope(k_rope, cos_m, sin_m)
        ctx_len = c_k.shape[1]
        dk_all = jnp.concatenate([dk, c_k.reshape(bsz, ctx_len, DIFF_HEADS, 2, DIFF_HEAD_DIM)], 1)
        dv_all = jnp.concatenate([dv, c_v], 1)
        ckv_all = jnp.concatenate([ckv, c_ckv], 1)
        kr_all = jnp.concatenate([k_rope, c_kr], 1)
    n_keys = ckv_all.shape[1]
    k_nope = (ckv_all @ lw['mla_w_uk']).reshape(bsz, n_keys, MLA_HEADS, MLA_NOPE)
    v_mla = (ckv_all @ lw['mla_w_uv']).reshape(bsz, n_keys, MLA_HEADS, MLA_V)

    diff_out = diff_attention(dq, dk_all, dv_all, lam, lw['diff_norm_g'], lam_init)
    s5_out, s5_final = s5_mixer(u, lw['s5_lam_re'], lw['s5_lam_im'], lw['s5_log_dt'],
                                lw['s5_b_re'], lw['s5_b_im'], lw['s5_c_re'], lw['s5_c_im'],
                                lw['s5_d'], lw['s5_w_glu'], h0)
    mla_out = mla_attention(q_nope, q_rope, k_nope, kr_all, v_mla)
    out = jnp.concatenate([diff_out, s5_out, mla_out], -1) @ lw['w_out']
    if ctx is None:
        new_ctx = (dk.reshape(bsz, length, DIFF_HEADS, 2 * DIFF_HEAD_DIM), dv, ckv, k_rope, s5_final)
        return out, new_ctx
    return out, None


def trunk_layer(x, cond, lw, layer_idx, ctx):
    mods = jax.nn.silu(cond) @ lw['w_ada'] + lw['b_ada']
    sh1, sc1, g1, sh2, sc2, g2 = jnp.split(mods[:, None, :], 6, axis=-1)
    lam_init = 0.8 - 0.6 * math.exp(-0.3 * layer_idx)
    lam = (jnp.exp(jnp.sum(lw['diff_lq1'].astype(F32) * lw['diff_lk1'].astype(F32)))
           - jnp.exp(jnp.sum(lw['diff_lq2'].astype(F32) * lw['diff_lk2'].astype(F32))) + lam_init)
    h = x * (1 + sc1) + sh1
    mix, new_ctx = token_mixer(h, lw, lam, lam_init, ctx)
    x = layer_norm(ALPHA * x + g1 * mix, lw['ln1_g'], lw['ln1_b'])
    h = x * (1 + sc2) + sh2
    f = (jax.nn.silu(h @ lw['ffn_w_gate']) * (h @ lw['ffn_w_up'])) @ lw['ffn_w_down']
    x = layer_norm(ALPHA * x + g2 * f, lw['ln2_g'], lw['ln2_b'])
    return x, new_ctx


def setup_inputs(seed: int = 0) -> dict:
    key = jax.random.key(seed)
    ks = iter(jax.random.split(key, 48))

    def nrm(shape, s=1.0):
        return jax.random.normal(next(ks), shape, F32) * s

    G, P = S5_GROUPS, S5_STATE
    lam_im_base = jnp.pi * jnp.arange(P, dtype=F32)
    return {
        'x_prompt': nrm((BATCH, SEQ, D_MODEL)),
        'x_sample': nrm((DEC_BATCH, DEC_SEQ, D_MODEL)),
        'c': nrm((DEC_BATCH, D_MODEL)),
        'cache_diff_k': nrm((DEC_BATCH, DEPTH, PAST_LEN, DIFF_HEADS, 2 * DIFF_HEAD_DIM)),
        'cache_diff_v': nrm((DEC_BATCH, DEPTH, PAST_LEN, DIFF_HEADS, DIFF_V_DIM)),
        'cache_mla_ckv': nrm((DEC_BATCH, DEPTH, PAST_LEN, MLA_KV_RANK)),
        'cache_mla_krope': nrm((DEC_BATCH, DEPTH, PAST_LEN, MLA_ROPE)),
        'state_s5': nrm((DEC_BATCH, DEPTH, 2, G, P, 2), 0.1),
        'c_ctx': nrm((D_MODEL,)),
        'w_ada': nrm((DEPTH, D_MODEL, 6 * D_MODEL), 0.5 * D_MODEL ** -0.5),
        'b_ada': nrm((DEPTH, 6 * D_MODEL), 0.02),
        'w_in': nrm((DEPTH, D_MODEL, IN_W), D_MODEL ** -0.5),
        'w_out': nrm((DEPTH, D_MIX, D_MODEL), BETA * D_MIX ** -0.5),
        'diff_lq1': nrm((DEPTH, DIFF_HEAD_DIM), 0.1),
        'diff_lk1': nrm((DEPTH, DIFF_HEAD_DIM), 0.1),
        'diff_lq2': nrm((DEPTH, DIFF_HEAD_DIM), 0.1),
        'diff_lk2': nrm((DEPTH, DIFF_HEAD_DIM), 0.1),
        'diff_norm_g': 1.0 + nrm((DEPTH, DIFF_V_DIM), 0.02),
        's5_lam_re': -0.5 + nrm((DEPTH, 2, G, P), 0.01),
        's5_lam_im': lam_im_base + nrm((DEPTH, 2, G, P), 0.01),
        's5_log_dt': jax.random.uniform(next(ks), (DEPTH, 2, G), F32, math.log(1e-3), math.log(1e-1)),
        's5_b_re': nrm((DEPTH, 2, G, P, S5_CH), (2 * S5_CH) ** -0.5),
        's5_b_im': nrm((DEPTH, 2, G, P, S5_CH), (2 * S5_CH) ** -0.5),
        's5_c_re': nrm((DEPTH, 2, G, S5_CH, P), S5_STATE ** -0.5),
        's5_c_im': nrm((DEPTH, 2, G, S5_CH, P), S5_STATE ** -0.5),
        's5_d': nrm((DEPTH, G, S5_CH)),
        's5_w_glu': nrm((DEPTH, S5_W, S5_W), S5_W ** -0.5),
        'mla_q_norm_g': 1.0 + nrm((DEPTH, MLA_Q_RANK), 0.02),
        'mla_w_uq': nrm((DEPTH, MLA_Q_RANK, MLA_HEADS * (MLA_NOPE + MLA_ROPE)), MLA_Q_RANK ** -0.5),
        'mla_kv_norm_g': 1.0 + nrm((DEPTH, MLA_KV_RANK), 0.02),
        'mla_w_uk': nrm((DEPTH, MLA_KV_RANK, MLA_HEADS * MLA_NOPE), MLA_KV_RANK ** -0.5),
        'mla_w_uv': nrm((DEPTH, MLA_KV_RANK, MLA_HEADS * MLA_V), MLA_KV_RANK ** -0.5),
        'ln1_g': 1.0 + nrm((DEPTH, D_MODEL), 0.02),
        'ln1_b': nrm((DEPTH, D_MODEL), 0.02),
        'ln2_g': 1.0 + nrm((DEPTH, D_MODEL), 0.02),
        'ln2_b': nrm((DEPTH, D_MODEL), 0.02),
        'ffn_w_gate': nrm((DEPTH, D_MODEL, FFN_HIDDEN), D_MODEL ** -0.5),
        'ffn_w_up': nrm((DEPTH, D_MODEL, FFN_HIDDEN), D_MODEL ** -0.5),
        'ffn_w_down': nrm((DEPTH, FFN_HIDDEN, D_MODEL), BETA * FFN_HIDDEN ** -0.5),
    }


def reference(x_prompt, x_sample, c, cache_diff_k, cache_diff_v, cache_mla_ckv, cache_mla_krope,
              state_s5, c_ctx, w_ada, b_ada, w_in, w_out, diff_lq1, diff_lk1, diff_lq2, diff_lk2,
              diff_norm_g, s5_lam_re, s5_lam_im, s5_log_dt, s5_b_re, s5_b_im, s5_c_re, s5_c_im,
              s5_d, s5_w_glu, mla_q_norm_g, mla_w_uq, mla_kv_norm_g, mla_w_uk, mla_w_uv,
              ln1_g, ln1_b, ln2_g, ln2_b, ffn_w_gate, ffn_w_up, ffn_w_down):
    stacked = {
        'w_ada': w_ada, 'b_ada': b_ada, 'w_in': w_in, 'w_out': w_out,
        'diff_lq1': diff_lq1, 'diff_lk1': diff_lk1, 'diff_lq2': diff_lq2, 'diff_lk2': diff_lk2,
        'diff_norm_g': diff_norm_g,
        's5_lam_re': s5_lam_re, 's5_lam_im': s5_lam_im, 's5_log_dt': s5_log_dt,
        's5_b_re': s5_b_re, 's5_b_im': s5_b_im, 's5_c_re': s5_c_re, 's5_c_im': s5_c_im,
        's5_d': s5_d, 's5_w_glu': s5_w_glu,
        'mla_q_norm_g': mla_q_norm_g, 'mla_w_uq': mla_w_uq, 'mla_kv_norm_g': mla_kv_norm_g,
        'mla_w_uk': mla_w_uk, 'mla_w_uv': mla_w_uv,
        'ln1_g': ln1_g, 'ln1_b': ln1_b, 'ln2_g': ln2_g, 'ln2_b': ln2_b,
        'ffn_w_gate': ffn_w_gate, 'ffn_w_up': ffn_w_up, 'ffn_w_down': ffn_w_down,
    }
    cond_ctx = jnp.broadcast_to(c_ctx, (x_prompt.shape[0], D_MODEL))
    y_prompt = x_prompt
    ks, vs, ckvs, krs, sts = [], [], [], [], []
    for l in range(DEPTH):
        lw = {name: arr[l] for name, arr in stacked.items()}
        y_prompt, (k_l, v_l, ckv_l, kr_l, st_l) = trunk_layer(y_prompt, cond_ctx, lw, l, None)
        ks.append(k_l); vs.append(v_l); ckvs.append(ckv_l); krs.append(kr_l); sts.append(st_l)
    y_sample = x_sample
    for l in range(DEPTH):
        lw = {name: arr[l] for name, arr in stacked.items()}
        ctx = (cache_diff_k[:, l], cache_diff_v[:, l], cache_mla_ckv[:, l], cache_mla_krope[:, l],
               state_s5[:, l])
        y_sample, _ = trunk_layer(y_sample, c, lw, l, ctx)
    new_diff_k = jnp.stack(ks, 1)
    new_diff_v = jnp.stack(vs, 1)
    new_mla_ckv = jnp.stack(ckvs, 1)
    new_mla_krope = jnp.stack(krs, 1)
    new_s5_state = jnp.stack(sts, 1)
    return (y_prompt, y_sample, new_diff_k, new_diff_v, new_mla_ckv, new_mla_krope, new_s5_state)
```

```python
import functools
import math

import jax
import jax.numpy as jnp
from jax import lax
from jax.experimental import pallas as pl
from jax.experimental.pallas import tpu as pltpu

F32 = jnp.float32
BF16 = jnp.bfloat16

D_MODEL = 1024
DEPTH = 2
GRID_W = 64
DIFF_HEADS = 4
DIFF_HEAD_DIM = 32
DIFF_V_DIM = 64
DIFF_W = 256
S5_CH = 16
S5_W = 256
S5_GROUPS = 16
S5_STATE = 64
S5_N = S5_GROUPS * S5_STATE
MLA_HEADS = 8
MLA_NOPE = 64
MLA_ROPE = 32
MLA_V = 64
MLA_Q_RANK = 256
MLA_KV_RANK = 128
MLA_W = 512
MLA_HEAD_PAD = 128
FFN_HIDDEN = 2816
IN_W_EXT = 1536
ALPHA = (2 * DEPTH) ** 0.25
LN_EPS = 1e-5
RMS_EPS = 1e-6
ROPE_BASE = 10000.0
LOG2E = 1.4426950408889634

LANES = 128
SUBLANES = 8
TOKEN_BLOCK = 256
Q_BLOCK = 256
VMEM_LIMIT = 56 * 1024 * 1024
ADA_ROWS = 8


def _cparams(n_axes):
    return pltpu.CompilerParams(dimension_semantics=("arbitrary",) * n_axes,
                                vmem_limit_bytes=VMEM_LIMIT)


def _full(shape):
    nd = len(shape)
    return pl.BlockSpec(shape, lambda *_: (0,) * nd)


def _dot(a, b):
    return jnp.dot(a, b, preferred_element_type=F32)


def _dot_nt(a, b):
    return lax.dot_general(a, b, (((1,), (1,)), ((), ())), preferred_element_type=F32)


def _layer_norm(x, g, b):
    mu = jnp.mean(x, axis=-1, keepdims=True)
    xc = x - mu
    var = jnp.mean(xc * xc, axis=-1, keepdims=True)
    return xc * lax.rsqrt(var + LN_EPS) * g + b


def _rms_norm(x, g):
    return x * lax.rsqrt(jnp.mean(x * x, axis=-1, keepdims=True) + RMS_EPS) * g


def _rope(x, cos, sin_hi, sin_lo):
    outs = []
    for j in range(x.shape[1] // LANES):
        xb = x[:, j * LANES:(j + 1) * LANES]
        outs.append(xb * cos + pltpu.roll(xb, 16, 1) * sin_hi + pltpu.roll(xb, LANES - 16, 1) * sin_lo)
    return outs[0] if len(outs) == 1 else jnp.concatenate(outs, axis=1)


def _ada_kernel(cond_ref, w_ref, b_ref, o_ref):
    c = cond_ref[...]
    s = c * jax.nn.sigmoid(c)
    o_ref[0] = _dot(s.astype(BF16), w_ref[0].astype(BF16)) + b_ref[0]


def _ada_call(cond, w_ada, b_ada):
    n_blk = 6
    return pl.pallas_call(
        _ada_kernel,
        grid=(DEPTH, n_blk),
        in_specs=[
            pl.BlockSpec((ADA_ROWS, D_MODEL), lambda l, j: (0, 0)),
            pl.BlockSpec((1, D_MODEL, D_MODEL), lambda l, j: (l, 0, j)),
            pl.BlockSpec((1, 1, D_MODEL), lambda l, j: (l, 0, j)),
        ],
        out_specs=pl.BlockSpec((1, ADA_ROWS, D_MODEL), lambda l, j: (l, 0, j)),
        out_shape=jax.ShapeDtypeStruct((DEPTH, ADA_ROWS, 6 * D_MODEL), F32),
        compiler_params=_cparams(2),
        name="ada",
    )(cond, w_ada, b_ada.reshape(DEPTH, 1, 6 * D_MODEL))


def _prep_kernel(*refs, latent, n_new, q_scale_diff, q_scale_mla):
    if latent:
        (x_ref, sh_ref, sc_ref, win_ref, gq_ref, gkv_ref, wuq_ref, wuk_ref, wuv_ref,
         cd_ref, shd_ref, sld_ref, cm_ref, shm_ref, slm_ref,
         ck_ref, cv_ref, cckv_ref, ckr_ref,
         qd_ref, kd_ref, vd_ref, u_ref, qc_ref, kc_ref, vm_ref) = refs
    else:
        (x_ref, sh_ref, sc_ref, win_ref, gq_ref, gkv_ref, wuq_ref, wuk_ref, wuv_ref,
         qd_ref, kd_ref, vd_ref, u_ref, qc_ref, kc_ref, vm_ref,
         k32_ref, v32_ref, ckv32_ref, kr32_ref) = refs

    def mla_keys(ckv, kr_wide):
        cb = ckv.astype(BF16)
        kn = _dot(cb, wuk_ref[...])
        for h in range(MLA_HEADS):
            kc_ref[0, h] = (kn[:, h * MLA_HEAD_PAD:(h + 1) * MLA_HEAD_PAD] + kr_wide).astype(BF16)
        vv = _dot(cb, wuv_ref[...]).astype(BF16)
        vm_ref[0, 0] = vv[:, :256]
        vm_ref[0, 1] = vv[:, 256:]

    def new_tokens():
        x = x_ref[0]
        h = x * (1.0 + sc_ref[0]) + sh_ref[0]
        z = _dot(h.astype(BF16), win_ref[...])
        dq, dk, dv = z[:, 0:256], z[:, 256:512], z[:, 512:768]
        u = z[:, 768:1024]
        q_lat, kv_lat, kr = z[:, 1024:1280], z[:, 1280:1408], z[:, 1408:1536]
        ckv = _rms_norm(kv_lat, gkv_ref[...])
        qn = _rms_norm(q_lat, gq_ref[...])
        qc = _dot(qn.astype(BF16), wuq_ref[...])
        if latent:
            cd, shd, sld = cd_ref[...], shd_ref[...], sld_ref[...]
            cm, shm, slm = cm_ref[...], shm_ref[...], slm_ref[...]
            dq = _rope(dq, cd, shd, sld)
            dk = _rope(dk, cd, shd, sld)
            qc = _rope(qc, cm, shm, slm)
            kr = _rope(kr, cm, shm, slm)
        else:
            k32_ref[0] = dk
            v32_ref[0] = dv
            ckv32_ref[0] = ckv
            kr32_ref[0] = kr
        qd_ref[0] = (dq * q_scale_diff).astype(BF16)
        kd_ref[0] = dk.astype(BF16)
        vd_ref[0] = dv.astype(BF16)
        u_ref[0] = u
        qcs = (qc * q_scale_mla).astype(BF16)
        for hh in range(MLA_HEADS):
            qc_ref[0, hh] = qcs[:, hh * MLA_HEAD_PAD:(hh + 1) * MLA_HEAD_PAD]
        mla_keys(ckv, kr)

    if latent:
        i = pl.program_id(1)
        pl.when(i < n_new)(new_tokens)

        @pl.when(i == n_new)
        def _():
            kd_ref[0] = ck_ref[0, 0].astype(BF16)
            vd_ref[0] = cv_ref[0, 0].astype(BF16)
            mla_keys(cckv_ref[0, 0], ckr_ref[0, 0])
    else:
        new_tokens()


def _prep_call(latent, layer, x, mods3, wts, tables=None, caches=None):
    bsz, length, _ = x.shape
    tm = TOKEN_BLOCK
    n_new = length // tm
    weights = [wts["w_in"], wts["gq"], wts["gkv"], wts["w_uq"], wts["w_uk"], wts["w_uv"]]
    w_specs = [_full(w.shape) for w in weights]
    scale_d = DIFF_HEAD_DIM ** -0.5 * LOG2E
    scale_m = (MLA_NOPE + MLA_ROPE) ** -0.5 * LOG2E
    body = functools.partial(_prep_kernel, latent=latent, n_new=n_new,
                             q_scale_diff=scale_d, q_scale_mla=scale_m)

    if latent:
        past = caches[0].shape[2]
        assert past == tm
        n_keys = length + past
        segs = SUBLANES
        seg_blocks = n_new // segs
        nb = lambda i: jnp.minimum(i, n_new - 1)
        row = lambda b: ((layer * ADA_ROWS + 1 + b) * 6)
        in_specs = ([pl.BlockSpec((1, tm, D_MODEL), lambda b, i: (b, nb(i), 0)),
                     pl.BlockSpec((1, 1, D_MODEL), lambda b, i: (row(b), 0, 0)),
                     pl.BlockSpec((1, 1, D_MODEL), lambda b, i: (row(b) + 1, 0, 0))]
                    + w_specs
                    + [pl.BlockSpec((tm, LANES), lambda b, i: (nb(i), 0))] * 6
                    + [pl.BlockSpec((1, 1, past, 256), lambda b, i: (b, layer, 0, 0)),
                       pl.BlockSpec((1, 1, past, 256), lambda b, i: (b, layer, 0, 0)),
                       pl.BlockSpec((1, 1, past, LANES), lambda b, i: (b, layer, 0, 0)),
                       pl.BlockSpec((1, 1, past, LANES), lambda b, i: (b, layer, 0, 0))])
        out_shape = [jax.ShapeDtypeStruct((bsz, length, 256), BF16),
                     jax.ShapeDtypeStruct((bsz, n_keys, 256), BF16),
                     jax.ShapeDtypeStruct((bsz, n_keys, 256), BF16),
                     jax.ShapeDtypeStruct((bsz, length // segs, segs * S5_W), F32),
                     jax.ShapeDtypeStruct((bsz, MLA_HEADS, length, MLA_HEAD_PAD), BF16),
                     jax.ShapeDtypeStruct((bsz, MLA_HEADS, n_keys, MLA_HEAD_PAD), BF16),
                     jax.ShapeDtypeStruct((bsz, 2, n_keys, 256), BF16)]
        out_specs = [pl.BlockSpec((1, tm, 256), lambda b, i: (b, nb(i), 0)),
                     pl.BlockSpec((1, tm, 256), lambda b, i: (b, i, 0)),
                     pl.BlockSpec((1, tm, 256), lambda b, i: (b, i, 0)),
                     pl.BlockSpec((1, tm, S5_W), lambda b, i: (b, nb(i) % seg_blocks, nb(i) // seg_blocks)),
                     pl.BlockSpec((1, MLA_HEADS, tm, MLA_HEAD_PAD), lambda b, i: (b, 0, nb(i), 0)),
                     pl.BlockSpec((1, MLA_HEADS, tm, MLA_HEAD_PAD), lambda b, i: (b, 0, i, 0)),
                     pl.BlockSpec((1, 2, tm, 256), lambda b, i: (b, 0, i, 0))]
        args = [x, mods3, mods3] + weights + list(tables) + list(caches)
        grid = (bsz, n_new + 1)
    else:
        assert length == tm and bsz % SUBLANES == 0
        row0 = layer * ADA_ROWS * 6
        in_specs = ([pl.BlockSpec((1, tm, D_MODEL), lambda b: (b, 0, 0)),
                     pl.BlockSpec((1, 1, D_MODEL), lambda b: (row0, 0, 0)),
                     pl.BlockSpec((1, 1, D_MODEL), lambda b: (row0 + 1, 0, 0))]
                    + w_specs)
        out_shape = [jax.ShapeDtypeStruct((bsz, length, 256), BF16),
                     jax.ShapeDtypeStruct((bsz, length, 256), BF16),
                     jax.ShapeDtypeStruct((bsz, length, 256), BF16),
                     jax.ShapeDtypeStruct((bsz // SUBLANES, length, SUBLANES * S5_W), F32),
                     jax.ShapeDtypeStruct((bsz, MLA_HEADS, length, MLA_HEAD_PAD), BF16),
                     jax.ShapeDtypeStruct((bsz, MLA_HEADS, length, MLA_HEAD_PAD), BF16),
                     jax.ShapeDtypeStruct((bsz, 2, length, 256), BF16),
                     jax.ShapeDtypeStruct((bsz, length, 256), F32),
                     jax.ShapeDtypeStruct((bsz, length, 256), F32),
                     jax.ShapeDtypeStruct((bsz, length, MLA_KV_RANK), F32),
                     jax.ShapeDtypeStruct((bsz, length, LANES), F32)]
        out_specs = [pl.BlockSpec((1, tm, 256), lambda b: (b, 0, 0)),
                     pl.BlockSpec((1, tm, 256), lambda b: (b, 0, 0)),
                     pl.BlockSpec((1, tm, 256), lambda b: (b, 0, 0)),
                     pl.BlockSpec((1, tm, S5_W), lambda b: (b // SUBLANES, 0, b % SUBLANES)),
                     pl.BlockSpec((1, MLA_HEADS, tm, MLA_HEAD_PAD), lambda b: (b, 0, 0, 0)),
                     pl.BlockSpec((1, MLA_HEADS, tm, MLA_HEAD_PAD), lambda b: (b, 0, 0, 0)),
                     pl.BlockSpec((1, 2, tm, 256), lambda b: (b, 0, 0, 0)),
                     pl.BlockSpec((1, tm, 256), lambda b: (b, 0, 0)),
                     pl.BlockSpec((1, tm, 256), lambda b: (b, 0, 0)),
                     pl.BlockSpec((1, tm, MLA_KV_RANK), lambda b: (b, 0, 0)),
                     pl.BlockSpec((1, tm, LANES), lambda b: (b, 0, 0))]
        args = [x, mods3, mods3] + weights
        grid = (bsz,)

    return pl.pallas_call(
        body, grid=grid, in_specs=in_specs, out_specs=out_specs, out_shape=out_shape,
        compiler_params=_cparams(len(grid)),
        name="prep_lat" if latent else "prep_ctx",
    )(*args)


def _diff_lambda(lq1_ref, lk1_ref, lq2_ref, lk2_ref, lam_init):
    s1 = jnp.sum(lq1_ref[...] * lk1_ref[...], axis=-1, keepdims=True)
    s2 = jnp.sum(lq2_ref[...] * lk2_ref[...], axis=-1, keepdims=True)
    return jnp.exp(s1) - jnp.exp(s2) + lam_init


def _diff_attn_kernel(q_ref, k_ref, v_ref, lq1_ref, lk1_ref, lq2_ref, lk2_ref, g_ref, o_ref,
                      acc_ref, *, lam_init):
    q = q_ref[0]
    lam = _diff_lambda(lq1_ref, lk1_ref, lq2_ref, lk2_ref, lam_init)
    lane = lax.broadcasted_iota(jnp.int32, (1, DIFF_W), 1)
    acc_ref[...] = jnp.zeros_like(acc_ref)

    def one_map(hc, carry):
        lo = hc * DIFF_HEAD_DIM
        qmask = jnp.where((lane >= lo) & (lane < lo + DIFF_HEAD_DIM), 1.0, 0.0).astype(BF16)
        s = _dot_nt(q * qmask, k_ref[0])
        m = jnp.max(s, axis=-1, keepdims=True)
        e = jnp.exp2(s - m)
        denom = jnp.sum(e, axis=-1, keepdims=True)
        pv = _dot(e.astype(BF16), v_ref[0])
        hlo = (hc // 2) * DIFF_V_DIM
        hmask = (lane >= hlo) & (lane < hlo + DIFF_V_DIM)
        coef = jnp.where(hc % 2 == 0, 1.0, -lam)
        acc_ref[...] += jnp.where(hmask, pv * (coef / denom), 0.0)
        return carry

    lax.fori_loop(0, 2 * DIFF_HEADS, one_map, 0)

    o = acc_ref[...]
    sq = o * o
    ms = jnp.zeros_like(o)
    for h in range(DIFF_HEADS):
        hmask = (lane >= h * DIFF_V_DIM) & (lane < (h + 1) * DIFF_V_DIM)
        tot = jnp.sum(jnp.where(hmask, sq, 0.0), axis=-1, keepdims=True)
        ms = jnp.where(hmask, tot * (1.0 / DIFF_V_DIM), ms)
    o_ref[0] = (o * lax.rsqrt(ms + RMS_EPS) * g_ref[...] * (1.0 - lam_init)).astype(BF16)


def _diff_attn_call(layer, qd, kd, vd, lam_params, g_tiled):
    bsz, length, _ = qd.shape
    n_keys = kd.shape[1]
    tq = min(Q_BLOCK, length)
    lam_init = 0.8 - 0.6 * math.exp(-0.3 * layer)
    return pl.pallas_call(
        functools.partial(_diff_attn_kernel, lam_init=lam_init),
        grid=(bsz, length // tq),
        in_specs=[pl.BlockSpec((1, tq, DIFF_W), lambda b, i: (b, i, 0)),
                  pl.BlockSpec((1, n_keys, DIFF_W), lambda b, i: (b, 0, 0)),
                  pl.BlockSpec((1, n_keys, DIFF_W), lambda b, i: (b, 0, 0))]
                 + [_full((1, DIFF_HEAD_DIM))] * 4 + [_full((1, DIFF_W))],
        out_specs=pl.BlockSpec((1, tq, DIFF_W), lambda b, i: (b, i, 0)),
        out_shape=jax.ShapeDtypeStruct((bsz, length, DIFF_W), BF16),
        scratch_shapes=[pltpu.VMEM((tq, DIFF_W), F32)],
        compiler_params=_cparams(2),
        name="diff_attn",
    )(qd, kd, vd, *lam_params, g_tiled)


def _mla_attn_kernel(q_ref, k_ref, v_ref, o_ref, acc_ref):
    lane = lax.broadcasted_iota(jnp.int32, (1, 256), 1)
    acc_ref[...] = jnp.zeros_like(acc_ref)

    def one_head(h, carry):
        s = _dot_nt(q_ref[0, h], k_ref[0, h])
        m = jnp.max(s, axis=-1, keepdims=True)
        e = jnp.exp2(s - m)
        denom = jnp.sum(e, axis=-1, keepdims=True)
        half = h // 4
        pv = _dot(e.astype(BF16), v_ref[0, half])
        hlo = (h % 4) * MLA_V
        hmask = (lane >= hlo) & (lane < hlo + MLA_V)
        acc_ref[half] += jnp.where(hmask, pv * (1.0 / denom), 0.0)
        return carry

    lax.fori_loop(0, MLA_HEADS, one_head, 0)
    o_ref[0] = jnp.concatenate([acc_ref[0], acc_ref[1]], axis=1).astype(BF16)


def _mla_attn_call(qc, kc, vm):
    bsz, _, length, _ = qc.shape
    n_keys = kc.shape[2]
    tq = min(Q_BLOCK, length)
    return pl.pallas_call(
        _mla_attn_kernel,
        grid=(bsz, length // tq),
        in_specs=[pl.BlockSpec((1, MLA_HEADS, tq, MLA_HEAD_PAD), lambda b, i: (b, 0, i, 0)),
                  pl.BlockSpec((1, MLA_HEADS, n_keys, MLA_HEAD_PAD), lambda b, i: (b, 0, 0, 0)),
                  pl.BlockSpec((1, 2, n_keys, 256), lambda b, i: (b, 0, 0, 0))],
        out_specs=pl.BlockSpec((1, tq, MLA_W), lambda b, i: (b, i, 0)),
        out_shape=jax.ShapeDtypeStruct((bsz, length, MLA_W), BF16),
        scratch_shapes=[pltpu.VMEM((2, tq, 256), F32)],
        compiler_params=_cparams(2),
        name="mla_attn",
    )(qc, kc, vm)


S5_CHUNK_STEPS = 64


def _s5_kernel(*refs, steps, segmented):
    if segmented:
        (u_ref, bre_ref, bim_ref, cre_ref, cim_ref, lre_ref, lim_ref, ldt_ref, d_ref, wglu_ref,
         h0_ref, y_ref, bur_ref, bui_ref, yacc_ref, inr_ref, ini_ref) = refs
    else:
        (u_ref, bre_ref, bim_ref, cre_ref, cim_ref, lre_ref, lim_ref, ldt_ref, d_ref, wglu_ref,
         y_ref, fin_ref, bur_ref, bui_ref, yacc_ref) = refs
    tc = S5_CHUNK_STEPS
    rows_c = tc * SUBLANES
    n_chunks = steps // tc
    yacc_ref[...] = jnp.zeros_like(yacc_ref)

    for d in range(2):
        lam_re, lam_im = lre_ref[d:d + 1, :], lim_ref[d:d + 1, :]
        dt = jnp.exp(ldt_ref[d:d + 1, :])
        mag = jnp.exp(lam_re * dt)
        ang = lam_im * dt
        a_re, a_im = mag * jnp.cos(ang), mag * jnp.sin(ang)
        den = lam_re * lam_re + lam_im * lam_im
        n_re, n_im = a_re - 1.0, a_im
        f_re = (n_re * lam_re + n_im * lam_im) / den
        f_im = (n_im * lam_re - n_re * lam_im) / den
        bbar_re = (f_re * bre_ref[d] - f_im * bim_ref[d]).astype(BF16)
        bbar_im = (f_re * bim_ref[d] + f_im * bre_ref[d]).astype(BF16)
        c_re, c_im = cre_ref[d].astype(BF16), cim_ref[d].astype(BF16)
        ar8 = jnp.broadcast_to(a_re, (SUBLANES, S5_N))
        ai8 = jnp.broadcast_to(a_im, (SUBLANES, S5_N))

        def run_pass(init, store, d=d, bbar_re=bbar_re, bbar_im=bbar_im, c_re=c_re, c_im=c_im,
                     ar8=ar8, ai8=ai8):
            def chunk(ci, carry):
                c = ci if d == 0 else n_chunks - 1 - ci
                rows = pl.ds(pl.multiple_of(c * rows_c, rows_c), rows_c)
                ub = u_ref[0, rows, :].astype(BF16)
                bur_ref[...] = _dot(ub, bbar_re)
                bui_ref[...] = _dot(ub, bbar_im)

                def step(jj, hc):
                    j = jj if d == 0 else tc - 1 - jj
                    r = pl.ds(pl.multiple_of(j * SUBLANES, SUBLANES), SUBLANES)
                    hr, hi = hc
                    nr = ar8 * hr - ai8 * hi + bur_ref[r, :]
                    ni = ar8 * hi + ai8 * hr + bui_ref[r, :]
                    if store:
                        bur_ref[r, :] = nr
                        bui_ref[r, :] = ni
                    return nr, ni

                carry = lax.fori_loop(0, tc, step, carry)
                if store:
                    yacc_ref[rows, :] += (_dot(bur_ref[...].astype(BF16), c_re)
                                          - _dot(bui_ref[...].astype(BF16), c_im))
                return carry

            return lax.fori_loop(0, n_chunks, chunk, init)

        zeros = (jnp.zeros((SUBLANES, S5_N), F32), jnp.zeros((SUBLANES, S5_N), F32))
        if segmented:
            f_r, f_i = run_pass(zeros, False)
            p_re, p_im = a_re, a_im
            for _ in range(int(math.log2(steps))):
                p_re, p_im = p_re * p_re - p_im * p_im, 2.0 * p_re * p_im
            c_r, c_i = h0_ref[0, d, 0:1, :], h0_ref[0, d, 1:2, :]
            order = range(SUBLANES) if d == 0 else range(SUBLANES - 1, -1, -1)
            for s in order:
                inr_ref[s:s + 1, :] = c_r
                ini_ref[s:s + 1, :] = c_i
                c_r, c_i = (f_r[s:s + 1, :] + p_re * c_r - p_im * c_i,
                            f_i[s:s + 1, :] + p_re * c_i + p_im * c_r)
            run_pass((inr_ref[...], ini_ref[...]), True)
        else:
            f_r, f_i = run_pass(zeros, True)
            fin_ref[0, d, 0] = f_r
            fin_ref[0, d, 1] = f_i

    n_rows = steps * SUBLANES
    ep_rows = 512

    def epilogue(ci, carry):
        rows = pl.ds(pl.multiple_of(ci * ep_rows, ep_rows), ep_rows)
        y = yacc_ref[rows, :] + u_ref[0, rows, :] * d_ref[...]
        y = jax.nn.gelu(y, approximate=True)
        y_ref[0, rows, :] = y * jax.nn.sigmoid(_dot(y.astype(BF16), wglu_ref[...]))
        return carry

    lax.fori_loop(0, n_rows // ep_rows, epilogue, 0)


def _s5_call(u_perm, s5w, h0=None):
    nb, steps, _ = u_perm.shape
    n_rows = steps * SUBLANES
    u2 = u_perm.reshape(nb, n_rows, S5_W)
    segmented = h0 is not None
    assert steps % S5_CHUNK_STEPS == 0 and steps & (steps - 1) == 0
    consts = [s5w["b_re"], s5w["b_im"], s5w["c_re"], s5w["c_im"], s5w["lam_re"], s5w["lam_im"],
              s5w["log_dt"], s5w["d"], s5w["w_glu"]]
    in_specs = [pl.BlockSpec((1, n_rows, S5_W), lambda b: (b, 0, 0))] + [_full(c.shape) for c in consts]
    args = [u2] + consts
    out_shape = [jax.ShapeDtypeStruct((nb, n_rows, S5_W), F32)]
    out_specs = [pl.BlockSpec((1, n_rows, S5_W), lambda b: (b, 0, 0))]
    scratch = [pltpu.VMEM((S5_CHUNK_STEPS * SUBLANES, S5_N), F32),
               pltpu.VMEM((S5_CHUNK_STEPS * SUBLANES, S5_N), F32),
               pltpu.VMEM((n_rows, S5_W), F32)]
    if segmented:
        in_specs.append(pl.BlockSpec((1, 2, 2, S5_N), lambda b: (b, 0, 0, 0)))
        args.append(h0)
        scratch += [pltpu.VMEM((SUBLANES, S5_N), F32), pltpu.VMEM((SUBLANES, S5_N), F32)]
    else:
        out_shape.append(jax.ShapeDtypeStruct((nb, 2, 2, SUBLANES, S5_N), F32))
        out_specs.append(pl.BlockSpec((1, 2, 2, SUBLANES, S5_N), lambda b: (b, 0, 0, 0, 0)))
    outs = pl.pallas_call(
        functools.partial(_s5_kernel, steps=steps, segmented=segmented),
        grid=(nb,), in_specs=in_specs, out_specs=out_specs, out_shape=out_shape,
        scratch_shapes=scratch, compiler_params=_cparams(1),
        name="s5_lat" if segmented else "s5_ctx",
    )(*args)
    y = outs[0].reshape(nb, steps, SUBLANES * S5_W)
    return (y, None) if segmented else (y, outs[1])


FFN_CHUNK = FFN_HIDDEN // 2


def _post_kernel(x_ref, a_ref, s_ref, m_ref, g1_ref, sh2_ref, sc2_ref, g2_ref,
                 wo_ref, wg_ref, wu_ref, wd_ref, l1g_ref, l1b_ref, l2g_ref, l2b_ref, o_ref):
    x = x_ref[0]
    mix = (_dot(a_ref[0], wo_ref[0:256, :])
           + _dot(s_ref[0].astype(BF16), wo_ref[256:512, :])
           + _dot(m_ref[0], wo_ref[512:1024, :]))
    x1 = _layer_norm(ALPHA * x + g1_ref[0] * mix, l1g_ref[...], l1b_ref[...])
    hb = (x1 * (1.0 + sc2_ref[0]) + sh2_ref[0]).astype(BF16)
    f = jnp.zeros_like(x)
    for c in range(FFN_HIDDEN // FFN_CHUNK):
        cols = slice(c * FFN_CHUNK, (c + 1) * FFN_CHUNK)
        gate = _dot(hb, wg_ref[:, cols])
        up = _dot(hb, wu_ref[:, cols])
        act = (gate * jax.nn.sigmoid(gate) * up).astype(BF16)
        f = f + _dot(act, wd_ref[cols, :])
    o_ref[0] = _layer_norm(ALPHA * x1 + g2_ref[0] * f, l2g_ref[...], l2b_ref[...])


def _post_call(latent, layer, x, attn_d, y_perm, attn_m, mods3, wts):
    bsz, length, _ = x.shape
    tm = TOKEN_BLOCK
    n_blk = length // tm
    if latent:
        seg_blocks = n_blk // SUBLANES
        row = lambda b: (layer * ADA_ROWS + 1 + b) * 6
        s_spec = pl.BlockSpec((1, tm, S5_W), lambda b, i: (b, i % seg_blocks, i // seg_blocks))
    else:
        row = lambda b: layer * ADA_ROWS * 6
        s_spec = pl.BlockSpec((1, tm, S5_W), lambda b, i: (b // SUBLANES, 0, b % SUBLANES))
    tok = lambda w: pl.BlockSpec((1, tm, w), lambda b, i: (b, i, 0))
    mod = lambda k: pl.BlockSpec((1, 1, D_MODEL), lambda b, i: (row(b) + k, 0, 0))
    weights = [wts["w_out"], wts["w_gate"], wts["w_up"], wts["w_down"],
               wts["ln1_g"], wts["ln1_b"], wts["ln2_g"], wts["ln2_b"]]
    w_specs = [pl.BlockSpec(w.shape, lambda b, i: (0, 0), pipeline_mode=pl.Buffered(1))
               for w in weights]
    return pl.pallas_call(
        _post_kernel,
        grid=(bsz, n_blk),
        in_specs=[tok(D_MODEL), tok(DIFF_W), s_spec, tok(MLA_W), mod(2), mod(3), mod(4), mod(5)]
                 + w_specs,
        out_specs=tok(D_MODEL),
        out_shape=jax.ShapeDtypeStruct((bsz, length, D_MODEL), F32),
        compiler_params=_cparams(2),
        name="post",
    )(x, attn_d, y_perm, attn_m, mods3, mods3, mods3, mods3, *weights)


def _rope_tables(length):
    rows = length // GRID_W
    row = jnp.repeat(jnp.arange(rows, dtype=F32), GRID_W)
    col = jnp.tile(jnp.arange(GRID_W, dtype=F32), rows)
    n_freq = DIFF_HEAD_DIM // 4
    inv = ROPE_BASE ** (-jnp.arange(n_freq, dtype=F32) / n_freq)
    ang = jnp.concatenate([row[:, None] * inv, col[:, None] * inv], -1)
    cos, sin = jnp.cos(ang), jnp.sin(ang)
    zero = jnp.zeros_like(sin)
    cos32 = jnp.concatenate([cos, cos], -1)
    hi32 = jnp.concatenate([zero, sin], -1)
    lo32 = jnp.concatenate([-sin, zero], -1)
    diff = tuple(jnp.tile(t, (1, LANES // 32)) for t in (cos32, hi32, lo32))
    ones64, zeros64 = jnp.ones((length, 64), F32), jnp.zeros((length, 64), F32)
    ones32, zeros32 = jnp.ones((length, 32), F32), jnp.zeros((length, 32), F32)
    mla = (jnp.concatenate([ones64, cos32, ones32], -1),
           jnp.concatenate([zeros64, hi32, zeros32], -1),
           jnp.concatenate([zeros64, lo32, zeros32], -1))
    return diff + mla


def _block_diag(blocks):
    g, r, c = blocks.shape
    eye = jnp.eye(g, dtype=blocks.dtype)
    return (blocks[:, :, None, :] * eye[:, None, :, None]).reshape(g * r, g * c)


def _layer_weights(l, p):
    w_in = p["w_in"][l]
    krope_cols = jnp.pad(w_in[:, 1408:1440], ((0, 0), (64, 32)))
    w_in_ext = jnp.concatenate([w_in[:, :1408], krope_cols], axis=1).astype(BF16)
    w_uq = jnp.pad(p["mla_w_uq"][l].reshape(MLA_Q_RANK, MLA_HEADS, MLA_NOPE + MLA_ROPE),
                   ((0, 0), (0, 0), (0, MLA_HEAD_PAD - MLA_NOPE - MLA_ROPE)))
    w_uk = jnp.pad(p["mla_w_uk"][l].reshape(MLA_KV_RANK, MLA_HEADS, MLA_NOPE),
                   ((0, 0), (0, 0), (0, MLA_HEAD_PAD - MLA_NOPE)))
    prep = {
        "w_in": w_in_ext,
        "gq": p["mla_q_norm_g"][l].reshape(1, MLA_Q_RANK),
        "gkv": p["mla_kv_norm_g"][l].reshape(1, MLA_KV_RANK),
        "w_uq": w_uq.reshape(MLA_Q_RANK, MLA_HEADS * MLA_HEAD_PAD).astype(BF16),
        "w_uk": w_uk.reshape(MLA_KV_RANK, MLA_HEADS * MLA_HEAD_PAD).astype(BF16),
        "w_uv": p["mla_w_uv"][l].astype(BF16),
    }
    lam_params = [p[n][l].reshape(1, DIFF_HEAD_DIM) for n in ("diff_lq1", "diff_lk1", "diff_lq2", "diff_lk2")]
    g_tiled = jnp.tile(p["diff_norm_g"][l], DIFF_HEADS).reshape(1, DIFF_W)
    s5w = {
        "b_re": jnp.stack([_block_diag(jnp.swapaxes(p["s5_b_re"][l, d], 1, 2)) for d in range(2)]),
        "b_im": jnp.stack([_block_diag(jnp.swapaxes(p["s5_b_im"][l, d], 1, 2)) for d in range(2)]),
        "c_re": jnp.stack([_block_diag(jnp.swapaxes(p["s5_c_re"][l, d], 1, 2)) for d in range(2)]),
        "c_im": jnp.stack([_block_diag(jnp.swapaxes(p["s5_c_im"][l, d], 1, 2)) for d in range(2)]),
        "lam_re": p["s5_lam_re"][l].reshape(2, S5_N),
        "lam_im": p["s5_lam_im"][l].reshape(2, S5_N),
        "log_dt": jnp.repeat(p["s5_log_dt"][l], S5_STATE, axis=-1),
        "d": p["s5_d"][l].reshape(1, S5_W),
        "w_glu": p["s5_w_glu"][l].astype(BF16),
    }
    post = {
        "w_out": p["w_out"][l].astype(BF16),
        "w_gate": p["ffn_w_gate"][l].astype(BF16),
        "w_up": p["ffn_w_up"][l].astype(BF16),
        "w_down": p["ffn_w_down"][l].astype(BF16),
        "ln1_g": p["ln1_g"][l].reshape(1, D_MODEL), "ln1_b": p["ln1_b"][l].reshape(1, D_MODEL),
        "ln2_g": p["ln2_g"][l].reshape(1, D_MODEL), "ln2_b": p["ln2_b"][l].reshape(1, D_MODEL),
    }
    return prep, lam_params, g_tiled, s5w, post


def kernel(x_prompt, x_sample, c, cache_diff_k, cache_diff_v, cache_mla_ckv, cache_mla_krope, state_s5, c_ctx, w_ada, b_ada, w_in, w_out, diff_lq1, diff_lk1, diff_lq2, diff_lk2, diff_norm_g, s5_lam_re, s5_lam_im, s5_log_dt, s5_b_re, s5_b_im, s5_c_re, s5_c_im, s5_d, s5_w_glu, mla_q_norm_g, mla_w_uq, mla_kv_norm_g, mla_w_uk, mla_w_uv, ln1_g, ln1_b, ln2_g, ln2_b, ffn_w_gate, ffn_w_up, ffn_w_down):
    p = dict(w_in=w_in, w_out=w_out, diff_lq1=diff_lq1, diff_lk1=diff_lk1, diff_lq2=diff_lq2,
             diff_lk2=diff_lk2, diff_norm_g=diff_norm_g, s5_lam_re=s5_lam_re, s5_lam_im=s5_lam_im,
             s5_log_dt=s5_log_dt, s5_b_re=s5_b_re, s5_b_im=s5_b_im, s5_c_re=s5_c_re, s5_c_im=s5_c_im,
             s5_d=s5_d, s5_w_glu=s5_w_glu, mla_q_norm_g=mla_q_norm_g, mla_w_uq=mla_w_uq,
             mla_kv_norm_g=mla_kv_norm_g, mla_w_uk=mla_w_uk, mla_w_uv=mla_w_uv, ln1_g=ln1_g,
             ln1_b=ln1_b, ln2_g=ln2_g, ln2_b=ln2_b, ffn_w_gate=ffn_w_gate, ffn_w_up=ffn_w_up,
             ffn_w_down=ffn_w_down)
    bsz, seq, _ = x_prompt.shape
    dec_b, dec_seq, _ = x_sample.shape
    past = cache_diff_k.shape[2]

    cond = jnp.concatenate([c_ctx[None, :], c, jnp.zeros((ADA_ROWS - 1 - dec_b, D_MODEL), F32)], 0)
    mods3 = _ada_call(cond, w_ada, b_ada).reshape(DEPTH * ADA_ROWS * 6, 1, D_MODEL)

    tables = _rope_tables(dec_seq)
    caches = (cache_diff_k.reshape(dec_b, DEPTH, past, DIFF_W),
              cache_diff_v.reshape(dec_b, DEPTH, past, DIFF_W),
              cache_mla_ckv,
              jnp.pad(cache_mla_krope, ((0, 0), (0, 0), (0, 0), (MLA_NOPE, LANES - MLA_NOPE - MLA_ROPE))))
    h0_all = jnp.moveaxis(state_s5, -1, 3).reshape(dec_b, DEPTH, 2, 2, S5_N)

    y_ctx, y_lat = x_prompt, x_sample
    new_k, new_v, new_ckv, new_kr, new_st = [], [], [], [], []
    for l in range(DEPTH):
        prep_w, lam_params, g_tiled, s5w, post_w = _layer_weights(l, p)

        qd, kd, vd, u_perm, qc, kc, vm, k32, v32, ckv32, kr32 = _prep_call(
            False, l, y_ctx, mods3, prep_w)
        attn_d = _diff_attn_call(l, qd, kd, vd, lam_params, g_tiled)
        attn_m = _mla_attn_call(qc, kc, vm)
        y_s5, fin = _s5_call(u_perm, s5w)
        y_ctx = _post_call(False, l, y_ctx, attn_d, y_s5, attn_m, mods3, post_w)
        new_k.append(k32.reshape(bsz, seq, DIFF_HEADS, 2 * DIFF_HEAD_DIM))
        new_v.append(v32.reshape(bsz, seq, DIFF_HEADS, DIFF_V_DIM))
        new_ckv.append(ckv32)
        new_kr.append(kr32[:, :, MLA_NOPE:MLA_NOPE + MLA_ROPE])
        st = jnp.transpose(fin, (0, 3, 1, 4, 2)).reshape(bsz, 2, S5_GROUPS, S5_STATE, 2)
        new_st.append(st)

        qd, kd, vd, u_perm, qc, kc, vm = _prep_call(True, l, y_lat, mods3, prep_w, tables, caches)
        attn_d = _diff_attn_call(l, qd, kd, vd, lam_params, g_tiled)
        attn_m = _mla_attn_call(qc, kc, vm)
        y_s5, _ = _s5_call(u_perm, s5w, h0_all[:, l])
        y_lat = _post_call(True, l, y_lat, attn_d, y_s5, attn_m, mods3, post_w)

    return (y_ctx, y_lat, jnp.stack(new_k, 1), jnp.stack(new_v, 1), jnp.stack(new_ckv, 1),
            jnp.stack(new_kr, 1), jnp.stack(new_st, 1))
```

```python
import functools
import math

import jax
import jax.numpy as jnp
from jax import lax
from jax.experimental import pallas as pl
from jax.experimental.pallas import tpu as pltpu

F32 = jnp.float32
BF16 = jnp.bfloat16

D_MODEL = 1024
DEPTH = 2
GRID_W = 64
DIFF_HEADS = 4
DIFF_HEAD_DIM = 32
DIFF_V_DIM = 64
DIFF_W = 256
S5_CH = 16
S5_W = 256
S5_GROUPS = 16
S5_STATE = 64
S5_N = S5_GROUPS * S5_STATE
MLA_HEADS = 8
MLA_NOPE = 64
MLA_ROPE = 32
MLA_V = 64
MLA_Q_RANK = 256
MLA_KV_RANK = 128
MLA_W = 512
MLA_HEAD_PAD = 128
FFN_HIDDEN = 2816
IN_W_EXT = 1536
ALPHA = (2 * DEPTH) ** 0.25
LN_EPS = 1e-5
RMS_EPS = 1e-6
ROPE_BASE = 10000.0
LOG2E = 1.4426950408889634

LANES = 128
SUBLANES = 8
TOKEN_BLOCK = 256
Q_BLOCK = 256
VMEM_LIMIT = 56 * 1024 * 1024
ADA_ROWS = 8


def _cparams(n_axes):
    return pltpu.CompilerParams(dimension_semantics=("arbitrary",) * n_axes,
                                vmem_limit_bytes=VMEM_LIMIT)


def _full(shape):
    nd = len(shape)
    return pl.BlockSpec(shape, lambda *_: (0,) * nd)


def _dot(a, b):
    return jnp.dot(a, b, preferred_element_type=F32)


def _dot_nt(a, b):
    return lax.dot_general(a, b, (((1,), (1,)), ((), ())), preferred_element_type=F32)


def _layer_norm(x, g, b):
    mu = jnp.mean(x, axis=-1, keepdims=True)
    xc = x - mu
    var = jnp.mean(xc * xc, axis=-1, keepdims=True)
    return xc * lax.rsqrt(var + LN_EPS) * g + b


def _rms_norm(x, g):
    return x * lax.rsqrt(jnp.mean(x * x, axis=-1, keepdims=True) + RMS_EPS) * g


def _rope(x, cos, sin_hi, sin_lo):
    outs = []
    for j in range(x.shape[1] // LANES):
        xb = x[:, j * LANES:(j + 1) * LANES]
        outs.append(xb * cos + pltpu.roll(xb, 16, 1) * sin_hi + pltpu.roll(xb, LANES - 16, 1) * sin_lo)
    return outs[0] if len(outs) == 1 else jnp.concatenate(outs, axis=1)


def _ada_kernel(cond_ref, w_ref, b_ref, o_ref):
    c = cond_ref[...]
    s = c * jax.nn.sigmoid(c)
    o_ref[0] = _dot(s.astype(BF16), w_ref[0].astype(BF16)) + b_ref[0]


def _ada_call(cond, w_ada, b_ada):
    n_blk = 6
    return pl.pallas_call(
        _ada_kernel,
        grid=(DEPTH, n_blk),
        in_specs=[
            pl.BlockSpec((ADA_ROWS, D_MODEL), lambda l, j: (0, 0)),
            pl.BlockSpec((1, D_MODEL, D_MODEL), lambda l, j: (l, 0, j)),
            pl.BlockSpec((1, 1, D_MODEL), lambda l, j: (l, 0, j)),
        ],
        out_specs=pl.BlockSpec((1, ADA_ROWS, D_MODEL), lambda l, j: (l, 0, j)),
        out_shape=jax.ShapeDtypeStruct((DEPTH, ADA_ROWS, 6 * D_MODEL), F32),
        compiler_params=_cparams(2),
        name="ada",
    )(cond, w_ada, b_ada.reshape(DEPTH, 1, 6 * D_MODEL))


def _prep_kernel(*refs, latent, n_new, q_scale_diff, q_scale_mla):
    if latent:
        (x_ref, sh_ref, sc_ref, win_ref, gq_ref, gkv_ref, wuq_ref, wuk_ref, wuv_ref,
         cd_ref, shd_ref, sld_ref, cm_ref, shm_ref, slm_ref,
         ck_ref, cv_ref, cckv_ref, ckr_ref,
         qd_ref, kd_ref, vd_ref, u_ref, qc_ref, kc_ref, vm_ref) = refs
    else:
        (x_ref, sh_ref, sc_ref, win_ref, gq_ref, gkv_ref, wuq_ref, wuk_ref, wuv_ref,
         qd_ref, kd_ref, vd_ref, u_ref, qc_ref, kc_ref, vm_ref,
         k32_ref, v32_ref, ckv32_ref, kr32_ref) = refs

    def mla_keys(ckv, kr_wide):
        cb = ckv.astype(BF16)
        kn = _dot(cb, wuk_ref[...])
        for h in range(MLA_HEADS):
            kc_ref[0, h] = (kn[:, h * MLA_HEAD_PAD:(h + 1) * MLA_HEAD_PAD] + kr_wide).astype(BF16)
        vv = _dot(cb, wuv_ref[...]).astype(BF16)
        vm_ref[0, 0] = vv[:, :256]
        vm_ref[0, 1] = vv[:, 256:]

    def new_tokens():
        x = x_ref[0]
        h = x * (1.0 + sc_ref[0]) + sh_ref[0]
        z = _dot(h.astype(BF16), win_ref[...])
        dq, dk, dv = z[:, 0:256], z[:, 256:512], z[:, 512:768]
        u = z[:, 768:1024]
        q_lat, kv_lat, kr = z[:, 1024:1280], z[:, 1280:1408], z[:, 1408:1536]
        ckv = _rms_norm(kv_lat, gkv_ref[...])
        qn = _rms_norm(q_lat, gq_ref[...])
        qc = _dot(qn.astype(BF16), wuq_ref[...])
        if latent:
            cd, shd, sld = cd_ref[...], shd_ref[...], sld_ref[...]
            cm, shm, slm = cm_ref[...], shm_ref[...], slm_ref[...]
            dq = _rope(dq, cd, shd, sld)
            dk = _rope(dk, cd, shd, sld)
            qc = _rope(qc, cm, shm, slm)
            kr = _rope(kr, cm, shm, slm)
        else:
            k32_ref[0] = dk
            v32_ref[0] = dv
            ckv32_ref[0] = ckv
            kr32_ref[0] = kr
        qd_ref[0] = (dq * q_scale_diff).astype(BF16)
        kd_ref[0] = dk.astype(BF16)
        vd_ref[0] = dv.astype(BF16)
        u_ref[0] = u
        qcs = (qc * q_scale_mla).astype(BF16)
        for hh in range(MLA_HEADS):
            qc_ref[0, hh] = qcs[:, hh * MLA_HEAD_PAD:(hh + 1) * MLA_HEAD_PAD]
        mla_keys(ckv, kr)

    if latent:
        i = pl.program_id(1)
        pl.when(i < n_new)(new_tokens)

        @pl.when(i == n_new)
        def _():
            kd_ref[0] = ck_ref[0, 0].astype(BF16)
            vd_ref[0] = cv_ref[0, 0].astype(BF16)
            mla_keys(cckv_ref[0, 0], ckr_ref[0, 0])
    else:
        new_tokens()


def _prep_call(latent, layer, x, mods3, wts, tables=None, caches=None):
    bsz, length, _ = x.shape
    tm = TOKEN_BLOCK
    n_new = length // tm
    weights = [wts["w_in"], wts["gq"], wts["gkv"], wts["w_uq"], wts["w_uk"], wts["w_uv"]]
    w_specs = [_full(w.shape) for w in weights]
    scale_d = DIFF_HEAD_DIM ** -0.5 * LOG2E
    scale_m = (MLA_NOPE + MLA_ROPE) ** -0.5 * LOG2E
    body = functools.partial(_prep_kernel, latent=latent, n_new=n_new,
                             q_scale_diff=scale_d, q_scale_mla=scale_m)

    if latent:
        past = caches[0].shape[2]
        assert past == tm
        n_keys = length + past
        segs = SUBLANES
        seg_blocks = n_new // segs
        nb = lambda i: jnp.minimum(i, n_new - 1)
        row = lambda b: ((layer * ADA_ROWS + 1 + b) * 6)
        in_specs = ([pl.BlockSpec((1, tm, D_MODEL), lambda b, i: (b, nb(i), 0)),
                     pl.BlockSpec((1, 1, D_MODEL), lambda b, i: (row(b), 0, 0)),
                     pl.BlockSpec((1, 1, D_MODEL), lambda b, i: (row(b) + 1, 0, 0))]
                    + w_specs
                    + [pl.BlockSpec((tm, LANES), lambda b, i: (nb(i), 0))] * 6
                    + [pl.BlockSpec((1, 1, past, 256), lambda b, i: (b, layer, 0, 0)),
                       pl.BlockSpec((1, 1, past, 256), lambda b, i: (b, layer, 0, 0)),
                       pl.BlockSpec((1, 1, past, LANES), lambda b, i: (b, layer, 0, 0)),
                       pl.BlockSpec((1, 1, past, LANES), lambda b, i: (b, layer, 0, 0))])
        out_shape = [jax.ShapeDtypeStruct((bsz, length, 256), BF16),
                     jax.ShapeDtypeStruct((bsz, n_keys, 256), BF16),
                     jax.ShapeDtypeStruct((bsz, n_keys, 256), BF16),
                     jax.ShapeDtypeStruct((bsz, length // segs, segs * S5_W), F32),
                     jax.ShapeDtypeStruct((bsz, MLA_HEADS, length, MLA_HEAD_PAD), BF16),
                     jax.ShapeDtypeStruct((bsz, MLA_HEADS, n_keys, MLA_HEAD_PAD), BF16),
                     jax.ShapeDtypeStruct((bsz, 2, n_keys, 256), BF16)]
        out_specs = [pl.BlockSpec((1, tm, 256), lambda b, i: (b, nb(i), 0)),
                     pl.BlockSpec((1, tm, 256), lambda b, i: (b, i, 0)),
                     pl.BlockSpec((1, tm, 256), lambda b, i: (b, i, 0)),
                     pl.BlockSpec((1, tm, S5_W), lambda b, i: (b, nb(i) % seg_blocks, nb(i) // seg_blocks)),
                     pl.BlockSpec((1, MLA_HEADS, tm, MLA_HEAD_PAD), lambda b, i: (b, 0, nb(i), 0)),
                     pl.BlockSpec((1, MLA_HEADS, tm, MLA_HEAD_PAD), lambda b, i: (b, 0, i, 0)),
                     pl.BlockSpec((1, 2, tm, 256), lambda b, i: (b, 0, i, 0))]
        args = [x, mods3, mods3] + weights + list(tables) + list(caches)
        grid = (bsz, n_new + 1)
    else:
        assert length == tm and bsz % SUBLANES == 0
        row0 = layer * ADA_ROWS * 6
        in_specs = ([pl.BlockSpec((1, tm, D_MODEL), lambda b: (b, 0, 0)),
                     pl.BlockSpec((1, 1, D_MODEL), lambda b: (row0, 0, 0)),
                     pl.BlockSpec((1, 1, D_MODEL), lambda b: (row0 + 1, 0, 0))]
                    + w_specs)
        out_shape = [jax.ShapeDtypeStruct((bsz, length, 256), BF16),
                     jax.ShapeDtypeStruct((bsz, length, 256), BF16),
                     jax.ShapeDtypeStruct((bsz, length, 256), BF16),
                     jax.ShapeDtypeStruct((bsz // SUBLANES, length, SUBLANES * S5_W), F32),
                     jax.ShapeDtypeStruct((bsz, MLA_HEADS, length, MLA_HEAD_PAD), BF16),
                     jax.ShapeDtypeStruct((bsz, MLA_HEADS, length, MLA_HEAD_PAD), BF16),
                     jax.ShapeDtypeStruct((bsz, 2, length, 256), BF16),
                     jax.ShapeDtypeStruct((bsz, length, 256), F32),
                     jax.ShapeDtypeStruct((bsz, length, 256), F32),
                     jax.ShapeDtypeStruct((bsz, length, MLA_KV_RANK), F32),
                     jax.ShapeDtypeStruct((bsz, length, LANES), F32)]
        out_specs = [pl.BlockSpec((1, tm, 256), lambda b: (b, 0, 0)),
                     pl.BlockSpec((1, tm, 256), lambda b: (b, 0, 0)),
                     pl.BlockSpec((1, tm, 256), lambda b: (b, 0, 0)),
                     pl.BlockSpec((1, tm, S5_W), lambda b: (b // SUBLANES, 0, b % SUBLANES)),
                     pl.BlockSpec((1, MLA_HEADS, tm, MLA_HEAD_PAD), lambda b: (b, 0, 0, 0)),
                     pl.BlockSpec((1, MLA_HEADS, tm, MLA_HEAD_PAD), lambda b: (b, 0, 0, 0)),
                     pl.BlockSpec((1, 2, tm, 256), lambda b: (b, 0, 0, 0)),
                     pl.BlockSpec((1, tm, 256), lambda b: (b, 0, 0)),
                     pl.BlockSpec((1, tm, 256), lambda b: (b, 0, 0)),
                     pl.BlockSpec((1, tm, MLA_KV_RANK), lambda b: (b, 0, 0)),
                     pl.BlockSpec((1, tm, LANES), lambda b: (b, 0, 0))]
        args = [x, mods3, mods3] + weights
        grid = (bsz,)

    return pl.pallas_call(
        body, grid=grid, in_specs=in_specs, out_specs=out_specs, out_shape=out_shape,
        compiler_params=_cparams(len(grid)),
        name="prep_lat" if latent else "prep_ctx",
    )(*args)


def _diff_lambda(lq1_ref, lk1_ref, lq2_ref, lk2_ref, lam_init):
    s1 = jnp.sum(lq1_ref[...] * lk1_ref[...], axis=-1, keepdims=True)
    s2 = jnp.sum(lq2_ref[...] * lk2_ref[...], axis=-1, keepdims=True)
    return jnp.exp(s1) - jnp.exp(s2) + lam_init


def _softmax_maps(n_maps, scores_fn, values_fn, emit_fn, s_refs):
    def stage_a(i):
        s = scores_fn(i)
        s_refs[i % 2][...] = s
        return jnp.max(s, axis=-1, keepdims=True)

    def stage_b(i, m):
        e = jnp.exp2(s_refs[i % 2][...] - m)
        denom = jnp.sum(e, axis=-1, keepdims=True)
        emit_fn(i, _dot(e.astype(BF16), values_fn(i)), denom)

    m_prev = stage_a(0)
    for i in range(1, n_maps):
        m_cur = stage_a(i)
        stage_b(i - 1, m_prev)
        m_prev = m_cur
    stage_b(n_maps - 1, m_prev)


def _diff_attn_kernel(q_ref, k_ref, v_ref, lq1_ref, lk1_ref, lq2_ref, lk2_ref, g_ref, o_ref,
                      acc_ref, s0_ref, s1_ref, *, lam_init):
    q = q_ref[0]
    lam = _diff_lambda(lq1_ref, lk1_ref, lq2_ref, lk2_ref, lam_init)
    lane = lax.broadcasted_iota(jnp.int32, (1, DIFF_W), 1)
    acc_ref[...] = jnp.zeros_like(acc_ref)

    def scores(hc):
        lo = hc * DIFF_HEAD_DIM
        qmask = jnp.where((lane >= lo) & (lane < lo + DIFF_HEAD_DIM), 1.0, 0.0).astype(BF16)
        return _dot_nt(q * qmask, k_ref[0])

    def emit(hc, pv, denom):
        hlo = (hc // 2) * DIFF_V_DIM
        hmask = (lane >= hlo) & (lane < hlo + DIFF_V_DIM)
        coef = 1.0 if hc % 2 == 0 else -lam
        acc_ref[...] += jnp.where(hmask, pv * (coef / denom), 0.0)

    _softmax_maps(2 * DIFF_HEADS, scores, lambda hc: v_ref[0], emit, (s0_ref, s1_ref))

    o = acc_ref[...]
    sq = o * o
    ms = jnp.zeros_like(o)
    for h in range(DIFF_HEADS):
        hmask = (lane >= h * DIFF_V_DIM) & (lane < (h + 1) * DIFF_V_DIM)
        tot = jnp.sum(jnp.where(hmask, sq, 0.0), axis=-1, keepdims=True)
        ms = jnp.where(hmask, tot * (1.0 / DIFF_V_DIM), ms)
    o_ref[0] = (o * lax.rsqrt(ms + RMS_EPS) * g_ref[...] * (1.0 - lam_init)).astype(BF16)


def _diff_attn_call(layer, qd, kd, vd, lam_params, g_tiled):
    bsz, length, _ = qd.shape
    n_keys = kd.shape[1]
    tq = min(Q_BLOCK, length)
    lam_init = 0.8 - 0.6 * math.exp(-0.3 * layer)
    return pl.pallas_call(
        functools.partial(_diff_attn_kernel, lam_init=lam_init),
        grid=(bsz, length // tq),
        in_specs=[pl.BlockSpec((1, tq, DIFF_W), lambda b, i: (b, i, 0)),
                  pl.BlockSpec((1, n_keys, DIFF_W), lambda b, i: (b, 0, 0)),
                  pl.BlockSpec((1, n_keys, DIFF_W), lambda b, i: (b, 0, 0))]
                 + [_full((1, DIFF_HEAD_DIM))] * 4 + [_full((1, DIFF_W))],
        out_specs=pl.BlockSpec((1, tq, DIFF_W), lambda b, i: (b, i, 0)),
        out_shape=jax.ShapeDtypeStruct((bsz, length, DIFF_W), BF16),
        scratch_shapes=[pltpu.VMEM((tq, DIFF_W), F32),
                        pltpu.VMEM((tq, n_keys), F32), pltpu.VMEM((tq, n_keys), F32)],
        compiler_params=_cparams(2),
        name="diff_attn",
    )(qd, kd, vd, *lam_params, g_tiled)


def _mla_attn_kernel(q_ref, k_ref, v_ref, o_ref, acc_ref, s0_ref, s1_ref):
    lane = lax.broadcasted_iota(jnp.int32, (1, 256), 1)
    acc_ref[...] = jnp.zeros_like(acc_ref)

    def emit(h, pv, denom):
        hlo = (h % 4) * MLA_V
        hmask = (lane >= hlo) & (lane < hlo + MLA_V)
        acc_ref[h // 4] += jnp.where(hmask, pv * (1.0 / denom), 0.0)

    _softmax_maps(MLA_HEADS, lambda h: _dot_nt(q_ref[0, h], k_ref[0, h]),
                  lambda h: v_ref[0, h // 4], emit, (s0_ref, s1_ref))
    o_ref[0] = jnp.concatenate([acc_ref[0], acc_ref[1]], axis=1).astype(BF16)


def _mla_attn_call(qc, kc, vm):
    bsz, _, length, _ = qc.shape
    n_keys = kc.shape[2]
    tq = min(Q_BLOCK, length)
    return pl.pallas_call(
        _mla_attn_kernel,
        grid=(bsz, length // tq),
        in_specs=[pl.BlockSpec((1, MLA_HEADS, tq, MLA_HEAD_PAD), lambda b, i: (b, 0, i, 0)),
                  pl.BlockSpec((1, MLA_HEADS, n_keys, MLA_HEAD_PAD), lambda b, i: (b, 0, 0, 0)),
                  pl.BlockSpec((1, 2, n_keys, 256), lambda b, i: (b, 0, 0, 0))],
        out_specs=pl.BlockSpec((1, tq, MLA_W), lambda b, i: (b, i, 0)),
        out_shape=jax.ShapeDtypeStruct((bsz, length, MLA_W), BF16),
        scratch_shapes=[pltpu.VMEM((2, tq, 256), F32),
                        pltpu.VMEM((tq, n_keys), F32), pltpu.VMEM((tq, n_keys), F32)],
        compiler_params=_cparams(2),
        name="mla_attn",
    )(qc, kc, vm)


S5_CHUNK_STEPS = 64


def _s5_kernel(*refs, steps, segmented):
    if segmented:
        (u_ref, bre_ref, bim_ref, cre_ref, cim_ref, lre_ref, lim_ref, ldt_ref, d_ref, wglu_ref,
         h0_ref, y_ref, bur_ref, bui_ref, yacc_ref, inr_ref, ini_ref) = refs
    else:
        (u_ref, bre_ref, bim_ref, cre_ref, cim_ref, lre_ref, lim_ref, ldt_ref, d_ref, wglu_ref,
         y_ref, fin_ref, bur_ref, bui_ref, yacc_ref) = refs
    tc = S5_CHUNK_STEPS
    rows_c = tc * SUBLANES
    n_chunks = steps // tc
    yacc_ref[...] = jnp.zeros_like(yacc_ref)

    for d in range(2):
        lam_re, lam_im = lre_ref[d:d + 1, :], lim_ref[d:d + 1, :]
        dt = jnp.exp(ldt_ref[d:d + 1, :])
        mag = jnp.exp(lam_re * dt)
        ang = lam_im * dt
        a_re, a_im = mag * jnp.cos(ang), mag * jnp.sin(ang)
        den = lam_re * lam_re + lam_im * lam_im
        n_re, n_im = a_re - 1.0, a_im
        f_re = (n_re * lam_re + n_im * lam_im) / den
        f_im = (n_im * lam_re - n_re * lam_im) / den
        bbar_re = (f_re * bre_ref[d] - f_im * bim_ref[d]).astype(BF16)
        bbar_im = (f_re * bim_ref[d] + f_im * bre_ref[d]).astype(BF16)
        c_re, c_im = cre_ref[d].astype(BF16), cim_ref[d].astype(BF16)
        ar8 = jnp.broadcast_to(a_re, (SUBLANES, S5_N))
        ai8 = jnp.broadcast_to(a_im, (SUBLANES, S5_N))

        def run_pass(init, store, d=d, bbar_re=bbar_re, bbar_im=bbar_im, c_re=c_re, c_im=c_im,
                     ar8=ar8, ai8=ai8):
            def chunk(ci, carry):
                c = ci if d == 0 else n_chunks - 1 - ci
                rows = pl.ds(pl.multiple_of(c * rows_c, rows_c), rows_c)
                ub = u_ref[0, rows, :].astype(BF16)
                bur_ref[...] = _dot(ub, bbar_re)
                bui_ref[...] = _dot(ub, bbar_im)

                def step(jj, hc):
                    j = jj if d == 0 else tc - 1 - jj
                    r = pl.ds(pl.multiple_of(j * SUBLANES, SUBLANES), SUBLANES)
                    hr, hi = hc
                    nr = ar8 * hr - ai8 * hi + bur_ref[r, :]
                    ni = ar8 * hi + ai8 * hr + bui_ref[r, :]
                    if store:
                        bur_ref[r, :] = nr
                        bui_ref[r, :] = ni
                    return nr, ni

                carry = lax.fori_loop(0, tc, step, carry)
                if store:
                    yacc_ref[rows, :] += (_dot(bur_ref[...].astype(BF16), c_re)
                                          - _dot(bui_ref[...].astype(BF16), c_im))
                return carry

            return lax.fori_loop(0, n_chunks, chunk, init)

        zeros = (jnp.zeros((SUBLANES, S5_N), F32), jnp.zeros((SUBLANES, S5_N), F32))
        if segmented:
            f_r, f_i = run_pass(zeros, False)
            p_re, p_im = a_re, a_im
            for _ in range(int(math.log2(steps))):
                p_re, p_im = p_re * p_re - p_im * p_im, 2.0 * p_re * p_im
            c_r, c_i = h0_ref[0, d, 0:1, :], h0_ref[0, d, 1:2, :]
            order = range(SUBLANES) if d == 0 else range(SUBLANES - 1, -1, -1)
            for s in order:
                inr_ref[s:s + 1, :] = c_r
                ini_ref[s:s + 1, :] = c_i
                c_r, c_i = (f_r[s:s + 1, :] + p_re * c_r - p_im * c_i,
                            f_i[s:s + 1, :] + p_re * c_i + p_im * c_r)
            run_pass((inr_ref[...], ini_ref[...]), True)
        else:
            f_r, f_i = run_pass(zeros, True)
            fin_ref[0, d, 0] = f_r
            fin_ref[0, d, 1] = f_i

    n_rows = steps * SUBLANES
    ep_rows = 512

    def epilogue(ci, carry):
        rows = pl.ds(pl.multiple_of(ci * ep_rows, ep_rows), ep_rows)
        y = yacc_ref[rows, :] + u_ref[0, rows, :] * d_ref[...]
        y = jax.nn.gelu(y, approximate=True)
        y_ref[0, rows, :] = y * jax.nn.sigmoid(_dot(y.astype(BF16), wglu_ref[...]))
        return carry

    lax.fori_loop(0, n_rows // ep_rows, epilogue, 0)


def _s5_call(u_perm, s5w, h0=None):
    nb, steps, _ = u_perm.shape
    n_rows = steps * SUBLANES
    u2 = u_perm.reshape(nb, n_rows, S5_W)
    segmented = h0 is not None
    assert steps % S5_CHUNK_STEPS == 0 and steps & (steps - 1) == 0
    consts = [s5w["b_re"], s5w["b_im"], s5w["c_re"], s5w["c_im"], s5w["lam_re"], s5w["lam_im"],
              s5w["log_dt"], s5w["d"], s5w["w_glu"]]
    in_specs = [pl.BlockSpec((1, n_rows, S5_W), lambda b: (b, 0, 0))] + [_full(c.shape) for c in consts]
    args = [u2] + consts
    out_shape = [jax.ShapeDtypeStruct((nb, n_rows, S5_W), F32)]
    out_specs = [pl.BlockSpec((1, n_rows, S5_W), lambda b: (b, 0, 0))]
    scratch = [pltpu.VMEM((S5_CHUNK_STEPS * SUBLANES, S5_N), F32),
               pltpu.VMEM((S5_CHUNK_STEPS * SUBLANES, S5_N), F32),
               pltpu.VMEM((n_rows, S5_W), F32)]
    if segmented:
        in_specs.append(pl.BlockSpec((1, 2, 2, S5_N), lambda b: (b, 0, 0, 0)))
        args.append(h0)
        scratch += [pltpu.VMEM((SUBLANES, S5_N), F32), pltpu.VMEM((SUBLANES, S5_N), F32)]
    else:
        out_shape.append(jax.ShapeDtypeStruct((nb, 2, 2, SUBLANES, S5_N), F32))
        out_specs.append(pl.BlockSpec((1, 2, 2, SUBLANES, S5_N), lambda b: (b, 0, 0, 0, 0)))
    outs = pl.pallas_call(
        functools.partial(_s5_kernel, steps=steps, segmented=segmented),
        grid=(nb,), in_specs=in_specs, out_specs=out_specs, out_shape=out_shape,
        scratch_shapes=scratch, compiler_params=_cparams(1),
        name="s5_lat" if segmented else "s5_ctx",
    )(*args)
    y = outs[0].reshape(nb, steps, SUBLANES * S5_W)
    return (y, None) if segmented else (y, outs[1])


FFN_CHUNK = FFN_HIDDEN // 2


def _post_kernel(x_ref, a_ref, s_ref, m_ref, g1_ref, sh2_ref, sc2_ref, g2_ref,
                 wo_ref, wg_ref, wu_ref, wd_ref, l1g_ref, l1b_ref, l2g_ref, l2b_ref, o_ref):
    x = x_ref[0]
    mix = (_dot(a_ref[0], wo_ref[0:256, :])
           + _dot(s_ref[0].astype(BF16), wo_ref[256:512, :])
           + _dot(m_ref[0], wo_ref[512:1024, :]))
    x1 = _layer_norm(ALPHA * x + g1_ref[0] * mix, l1g_ref[...], l1b_ref[...])
    hb = (x1 * (1.0 + sc2_ref[0]) + sh2_ref[0]).astype(BF16)
    f = jnp.zeros_like(x)
    for c in range(FFN_HIDDEN // FFN_CHUNK):
        cols = slice(c * FFN_CHUNK, (c + 1) * FFN_CHUNK)
        gate = _dot(hb, wg_ref[:, cols])
        up = _dot(hb, wu_ref[:, cols])
        act = (gate * jax.nn.sigmoid(gate) * up).astype(BF16)
        f = f + _dot(act, wd_ref[cols, :])
    o_ref[0] = _layer_norm(ALPHA * x1 + g2_ref[0] * f, l2g_ref[...], l2b_ref[...])


def _post_call(latent, layer, x, attn_d, y_perm, attn_m, mods3, wts):
    bsz, length, _ = x.shape
    tm = TOKEN_BLOCK
    n_blk = length // tm
    if latent:
        seg_blocks = n_blk // SUBLANES
        row = lambda b: (layer * ADA_ROWS + 1 + b) * 6
        s_spec = pl.BlockSpec((1, tm, S5_W), lambda b, i: (b, i % seg_blocks, i // seg_blocks))
    else:
        row = lambda b: layer * ADA_ROWS * 6
        s_spec = pl.BlockSpec((1, tm, S5_W), lambda b, i: (b // SUBLANES, 0, b % SUBLANES))
    tok = lambda w: pl.BlockSpec((1, tm, w), lambda b, i: (b, i, 0))
    mod = lambda k: pl.BlockSpec((1, 1, D_MODEL), lambda b, i: (row(b) + k, 0, 0))
    weights = [wts["w_out"], wts["w_gate"], wts["w_up"], wts["w_down"],
               wts["ln1_g"], wts["ln1_b"], wts["ln2_g"], wts["ln2_b"]]
    w_specs = [pl.BlockSpec(w.shape, lambda b, i: (0, 0), pipeline_mode=pl.Buffered(1))
               for w in weights]
    return pl.pallas_call(
        _post_kernel,
        grid=(bsz, n_blk),
        in_specs=[tok(D_MODEL), tok(DIFF_W), s_spec, tok(MLA_W), mod(2), mod(3), mod(4), mod(5)]
                 + w_specs,
        out_specs=tok(D_MODEL),
        out_shape=jax.ShapeDtypeStruct((bsz, length, D_MODEL), F32),
        compiler_params=_cparams(2),
        name="post",
    )(x, attn_d, y_perm, attn_m, mods3, mods3, mods3, mods3, *weights)


def _rope_tables(length):
    rows = length // GRID_W
    row = jnp.repeat(jnp.arange(rows, dtype=F32), GRID_W)
    col = jnp.tile(jnp.arange(GRID_W, dtype=F32), rows)
    n_freq = DIFF_HEAD_DIM // 4
    inv = ROPE_BASE ** (-jnp.arange(n_freq, dtype=F32) / n_freq)
    ang = jnp.concatenate([row[:, None] * inv, col[:, None] * inv], -1)
    cos, sin = jnp.cos(ang), jnp.sin(ang)
    zero = jnp.zeros_like(sin)
    cos32 = jnp.concatenate([cos, cos], -1)
    hi32 = jnp.concatenate([zero, sin], -1)
    lo32 = jnp.concatenate([-sin, zero], -1)
    diff = tuple(jnp.tile(t, (1, LANES // 32)) for t in (cos32, hi32, lo32))
    ones64, zeros64 = jnp.ones((length, 64), F32), jnp.zeros((length, 64), F32)
    ones32, zeros32 = jnp.ones((length, 32), F32), jnp.zeros((length, 32), F32)
    mla = (jnp.concatenate([ones64, cos32, ones32], -1),
           jnp.concatenate([zeros64, hi32, zeros32], -1),
           jnp.concatenate([zeros64, lo32, zeros32], -1))
    return diff + mla


def _block_diag(blocks):
    g, r, c = blocks.shape
    eye = jnp.eye(g, dtype=blocks.dtype)
    return (blocks[:, :, None, :] * eye[:, None, :, None]).reshape(g * r, g * c)


def _layer_weights(l, p):
    w_in = p["w_in"][l]
    krope_cols = jnp.pad(w_in[:, 1408:1440], ((0, 0), (64, 32)))
    w_in_ext = jnp.concatenate([w_in[:, :1408], krope_cols], axis=1).astype(BF16)
    w_uq = jnp.pad(p["mla_w_uq"][l].reshape(MLA_Q_RANK, MLA_HEADS, MLA_NOPE + MLA_ROPE),
                   ((0, 0), (0, 0), (0, MLA_HEAD_PAD - MLA_NOPE - MLA_ROPE)))
    w_uk = jnp.pad(p["mla_w_uk"][l].reshape(MLA_KV_RANK, MLA_HEADS, MLA_NOPE),
                   ((0, 0), (0, 0), (0, MLA_HEAD_PAD - MLA_NOPE)))
    prep = {
        "w_in": w_in_ext,
        "gq": p["mla_q_norm_g"][l].reshape(1, MLA_Q_RANK),
        "gkv": p["mla_kv_norm_g"][l].reshape(1, MLA_KV_RANK),
        "w_uq": w_uq.reshape(MLA_Q_RANK, MLA_HEADS * MLA_HEAD_PAD).astype(BF16),
        "w_uk": w_uk.reshape(MLA_KV_RANK, MLA_HEADS * MLA_HEAD_PAD).astype(BF16),
        "w_uv": p["mla_w_uv"][l].astype(BF16),
    }
    lam_params = [p[n][l].reshape(1, DIFF_HEAD_DIM) for n in ("diff_lq1", "diff_lk1", "diff_lq2", "diff_lk2")]
    g_tiled = jnp.tile(p["diff_norm_g"][l], DIFF_HEADS).reshape(1, DIFF_W)
    s5w = {
        "b_re": jnp.stack([_block_diag(jnp.swapaxes(p["s5_b_re"][l, d], 1, 2)) for d in range(2)]),
        "b_im": jnp.stack([_block_diag(jnp.swapaxes(p["s5_b_im"][l, d], 1, 2)) for d in range(2)]),
        "c_re": jnp.stack([_block_diag(jnp.swapaxes(p["s5_c_re"][l, d], 1, 2)) for d in range(2)]),
        "c_im": jnp.stack([_block_diag(jnp.swapaxes(p["s5_c_im"][l, d], 1, 2)) for d in range(2)]),
        "lam_re": p["s5_lam_re"][l].reshape(2, S5_N),
        "lam_im": p["s5_lam_im"][l].reshape(2, S5_N),
        "log_dt": jnp.repeat(p["s5_log_dt"][l], S5_STATE, axis=-1),
        "d": p["s5_d"][l].reshape(1, S5_W),
        "w_glu": p["s5_w_glu"][l].astype(BF16),
    }
    post = {
        "w_out": p["w_out"][l].astype(BF16),
        "w_gate": p["ffn_w_gate"][l].astype(BF16),
        "w_up": p["ffn_w_up"][l].astype(BF16),
        "w_down": p["ffn_w_down"][l].astype(BF16),
        "ln1_g": p["ln1_g"][l].reshape(1, D_MODEL), "ln1_b": p["ln1_b"][l].reshape(1, D_MODEL),
        "ln2_g": p["ln2_g"][l].reshape(1, D_MODEL), "ln2_b": p["ln2_b"][l].reshape(1, D_MODEL),
    }
    return prep, lam_params, g_tiled, s5w, post


def kernel(x_prompt, x_sample, c, cache_diff_k, cache_diff_v, cache_mla_ckv, cache_mla_krope, state_s5, c_ctx, w_ada, b_ada, w_in, w_out, diff_lq1, diff_lk1, diff_lq2, diff_lk2, diff_norm_g, s5_lam_re, s5_lam_im, s5_log_dt, s5_b_re, s5_b_im, s5_c_re, s5_c_im, s5_d, s5_w_glu, mla_q_norm_g, mla_w_uq, mla_kv_norm_g, mla_w_uk, mla_w_uv, ln1_g, ln1_b, ln2_g, ln2_b, ffn_w_gate, ffn_w_up, ffn_w_down):
    p = dict(w_in=w_in, w_out=w_out, diff_lq1=diff_lq1, diff_lk1=diff_lk1, diff_lq2=diff_lq2,
             diff_lk2=diff_lk2, diff_norm_g=diff_norm_g, s5_lam_re=s5_lam_re, s5_lam_im=s5_lam_im,
             s5_log_dt=s5_log_dt, s5_b_re=s5_b_re, s5_b_im=s5_b_im, s5_c_re=s5_c_re, s5_c_im=s5_c_im,
             s5_d=s5_d, s5_w_glu=s5_w_glu, mla_q_norm_g=mla_q_norm_g, mla_w_uq=mla_w_uq,
             mla_kv_norm_g=mla_kv_norm_g, mla_w_uk=mla_w_uk, mla_w_uv=mla_w_uv, ln1_g=ln1_g,
             ln1_b=ln1_b, ln2_g=ln2_g, ln2_b=ln2_b, ffn_w_gate=ffn_w_gate, ffn_w_up=ffn_w_up,
             ffn_w_down=ffn_w_down)
    bsz, seq, _ = x_prompt.shape
    dec_b, dec_seq, _ = x_sample.shape
    past = cache_diff_k.shape[2]

    cond = jnp.concatenate([c_ctx[None, :], c, jnp.zeros((ADA_ROWS - 1 - dec_b, D_MODEL), F32)], 0)
    mods3 = _ada_call(cond, w_ada, b_ada).reshape(DEPTH * ADA_ROWS * 6, 1, D_MODEL)

    tables = _rope_tables(dec_seq)
    caches = (cache_diff_k.reshape(dec_b, DEPTH, past, DIFF_W),
              cache_diff_v.reshape(dec_b, DEPTH, past, DIFF_W),
              cache_mla_ckv,
              jnp.pad(cache_mla_krope, ((0, 0), (0, 0), (0, 0), (MLA_NOPE, LANES - MLA_NOPE - MLA_ROPE))))
    h0_all = jnp.moveaxis(state_s5, -1, 3).reshape(dec_b, DEPTH, 2, 2, S5_N)

    y_ctx, y_lat = x_prompt, x_sample
    new_k, new_v, new_ckv, new_kr, new_st = [], [], [], [], []
    for l in range(DEPTH):
        prep_w, lam_params, g_tiled, s5w, post_w = _layer_weights(l, p)

        qd, kd, vd, u_perm, qc, kc, vm, k32, v32, ckv32, kr32 = _prep_call(
            False, l, y_ctx, mods3, prep_w)
        attn_d = _diff_attn_call(l, qd, kd, vd, lam_params, g_tiled)
        attn_m = _mla_attn_call(qc, kc, vm)
        y_s5, fin = _s5_call(u_perm, s5w)
        y_ctx = _post_call(False, l, y_ctx, attn_d, y_s5, attn_m, mods3, post_w)
        new_k.append(k32.reshape(bsz, seq, DIFF_HEADS, 2 * DIFF_HEAD_DIM))
        new_v.append(v32.reshape(bsz, seq, DIFF_HEADS, DIFF_V_DIM))
        new_ckv.append(ckv32)
        new_kr.append(kr32[:, :, MLA_NOPE:MLA_NOPE + MLA_ROPE])
        st = jnp.transpose(fin, (0, 3, 1, 4, 2)).reshape(bsz, 2, S5_GROUPS, S5_STATE, 2)
        new_st.append(st)

        qd, kd, vd, u_perm, qc, kc, vm = _prep_call(True, l, y_lat, mods3, prep_w, tables, caches)
        attn_d = _diff_attn_call(l, qd, kd, vd, lam_params, g_tiled)
        attn_m = _mla_attn_call(qc, kc, vm)
        y_s5, _ = _s5_call(u_perm, s5w, h0_all[:, l])
        y_lat = _post_call(True, l, y_lat, attn_d, y_s5, attn_m, mods3, post_w)

    return (y_ctx, y_lat, jnp.stack(new_k, 1), jnp.stack(new_v, 1), jnp.stack(new_ckv, 1),
            jnp.stack(new_kr, 1), jnp.stack(new_st, 1))
```

```python
import functools
import math

import jax
import jax.numpy as jnp
import numpy as np
from jax import lax
from jax.experimental import pallas as pl
from jax.experimental.pallas import tpu as pltpu

F32 = jnp.float32
BF16 = jnp.bfloat16

D_MODEL = 1024
DEPTH = 2
GRID_W = 64
DIFF_HEADS = 4
DIFF_HEAD_DIM = 32
DIFF_V_DIM = 64
DIFF_W = 256
S5_CH = 16
S5_W = 256
S5_GROUPS = 16
S5_STATE = 64
S5_N = S5_GROUPS * S5_STATE
MLA_HEADS = 8
MLA_NOPE = 64
MLA_ROPE = 32
MLA_V = 64
MLA_Q_RANK = 256
MLA_KV_RANK = 128
MLA_W = 512
MLA_HEAD_PAD = 128
FFN_HIDDEN = 2816
ALPHA = (2 * DEPTH) ** 0.25
LN_EPS = 1e-5
RMS_EPS = 1e-6
ROPE_BASE = 10000.0
LOG2E = 1.4426950408889634

LANES = 128
SUBLANES = 8
TOKEN_BLOCK = 512
TOKEN_SUB = 256
Q_BLOCK = 256
VMEM_LIMIT = 56 * 1024 * 1024
ADA_ROWS = 8


def _cparams(n_axes):
    return pltpu.CompilerParams(dimension_semantics=("arbitrary",) * n_axes,
                                vmem_limit_bytes=VMEM_LIMIT)


def _full(shape):
    nd = len(shape)
    return pl.BlockSpec(shape, lambda *_: (0,) * nd)


def _dot(a, b):
    return jnp.dot(a, b, preferred_element_type=F32)


def _dot_nt(a, b):
    return lax.dot_general(a, b, (((1,), (1,)), ((), ())), preferred_element_type=F32)


def _layer_norm(x, g, b):
    mu = jnp.mean(x, axis=-1, keepdims=True)
    xc = x - mu
    var = jnp.mean(xc * xc, axis=-1, keepdims=True)
    return xc * lax.rsqrt(var + LN_EPS) * g + b


def _rms_norm(x, g):
    return x * lax.rsqrt(jnp.mean(x * x, axis=-1, keepdims=True) + RMS_EPS) * g


def _rope(x, cos, sin_hi, sin_lo):
    outs = []
    for j in range(x.shape[1] // LANES):
        xb = x[:, j * LANES:(j + 1) * LANES]
        outs.append(xb * cos + pltpu.roll(xb, 16, 1) * sin_hi + pltpu.roll(xb, LANES - 16, 1) * sin_lo)
    return outs[0] if len(outs) == 1 else jnp.concatenate(outs, axis=1)


def _ada_kernel(cond_ref, w_ref, b_ref, o_ref):
    c = cond_ref[...]
    s = c * jax.nn.sigmoid(c)
    o_ref[0] = _dot(s.astype(BF16), w_ref[0].astype(BF16)) + b_ref[0]


def _ada_call(cond, w_ada, b_ada):
    n_blk = 6
    return pl.pallas_call(
        _ada_kernel,
        grid=(DEPTH, n_blk),
        in_specs=[
            pl.BlockSpec((ADA_ROWS, D_MODEL), lambda l, j: (0, 0)),
            pl.BlockSpec((1, D_MODEL, D_MODEL), lambda l, j: (l, 0, j)),
            pl.BlockSpec((1, 1, D_MODEL), lambda l, j: (l, 0, j)),
        ],
        out_specs=pl.BlockSpec((1, ADA_ROWS, D_MODEL), lambda l, j: (l, 0, j)),
        out_shape=jax.ShapeDtypeStruct((DEPTH, ADA_ROWS, 6 * D_MODEL), F32),
        compiler_params=_cparams(2),
        name="ada",
    )(cond, w_ada, b_ada.reshape(DEPTH, 1, 6 * D_MODEL))


def _mod_row(latent, layer, seq_len, tm):
    base = layer * ADA_ROWS
    if latent:
        return lambda i: (base + 1 + i // (seq_len // tm)) * 6
    return lambda i: base * 6


def _mla_keys(ckv, kr_wide, wuk_ref, wuv_ref):
    cb = ckv.astype(BF16)
    kn = _dot(cb, wuk_ref[...])
    keys = [(kn[:, h * MLA_HEAD_PAD:(h + 1) * MLA_HEAD_PAD] + kr_wide).astype(BF16)
            for h in range(MLA_HEADS)]
    vv = _dot(cb, wuv_ref[...]).astype(BF16)
    return keys, (vv[:, :256], vv[:, 256:])


def _prep_kernel(*refs, latent, n_new, q_scale_diff, q_scale_mla):
    if latent:
        (x_ref, sh_ref, sc_ref, win_ref, gq_ref, gkv_ref, wuq_ref, wuk_ref, wuv_ref,
         cd_ref, shd_ref, sld_ref, cm_ref, shm_ref, slm_ref,
         ck_ref, cv_ref, cckv_ref, ckr_ref,
         qd_ref, kd_ref, vd_ref, u_ref, qc_ref, kc_ref, vm_ref) = refs
    else:
        (x_ref, sh_ref, sc_ref, win_ref, gq_ref, gkv_ref, wuq_ref, wuk_ref, wuv_ref,
         qd_ref, kd_ref, vd_ref, u_ref, qc_ref, kc_ref, vm_ref,
         k32_ref, v32_ref, ckv32_ref, kr32_ref) = refs

    def put_keys(where, dk_b, dv_b, keys, vals):
        lead, rows = where
        kd_ref[lead, rows, :] = dk_b
        vd_ref[lead, rows, :] = dv_b
        for hh in range(MLA_HEADS):
            kc_ref[lead, hh, rows, :] = keys[hh]
        vm_ref[lead, 0, rows, :] = vals[0]
        vm_ref[lead, 1, rows, :] = vals[1]

    def new_tokens(sub):
        rows = slice(sub * TOKEN_SUB, (sub + 1) * TOKEN_SUB)
        h = x_ref[rows, :] * (1.0 + sc_ref[0]) + sh_ref[0]
        z = _dot(h.astype(BF16), win_ref[...])
        dq, dk, dv = z[:, 0:256], z[:, 256:512], z[:, 512:768]
        u = z[:, 768:1024]
        q_lat, kv_lat, kr = z[:, 1024:1280], z[:, 1280:1408], z[:, 1408:1536]
        ckv = _rms_norm(kv_lat, gkv_ref[...])
        qn = _rms_norm(q_lat, gq_ref[...])
        qc = _dot(qn.astype(BF16), wuq_ref[...])
        if latent:
            cd, shd, sld = cd_ref[rows, :], shd_ref[rows, :], sld_ref[rows, :]
            cm, shm, slm = cm_ref[rows, :], shm_ref[rows, :], slm_ref[rows, :]
            dq = _rope(dq, cd, shd, sld)
            dk = _rope(dk, cd, shd, sld)
            qc = _rope(qc, cm, shm, slm)
            kr = _rope(kr, cm, shm, slm)
        else:
            k32_ref[rows, :] = dk
            v32_ref[rows, :] = dv
            ckv32_ref[rows, :] = ckv
            kr32_ref[rows, :] = kr
        qd_ref[rows, :] = (dq * q_scale_diff).astype(BF16)
        u_ref[0, rows, :] = u[:, :LANES]
        u_ref[1, rows, :] = u[:, LANES:]
        qcs = (qc * q_scale_mla).astype(BF16)
        for hh in range(MLA_HEADS):
            qc_ref[hh, rows, :] = qcs[:, hh * MLA_HEAD_PAD:(hh + 1) * MLA_HEAD_PAD]
        keys, vals = _mla_keys(ckv, kr, wuk_ref, wuv_ref)
        where = (0, rows) if latent else (sub, slice(None))
        put_keys(where, dk.astype(BF16), dv.astype(BF16), keys, vals)

    n_sub = x_ref.shape[0] // TOKEN_SUB
    if latent:
        i = pl.program_id(1)

        @pl.when(i < n_new)
        def _():
            for sub in range(n_sub):
                new_tokens(sub)

        @pl.when(i == n_new)
        def _():
            past = ck_ref.shape[2]
            keys, vals = _mla_keys(cckv_ref[0, 0], ckr_ref[0, 0], wuk_ref, wuv_ref)
            put_keys((0, slice(0, past)), ck_ref[0, 0].astype(BF16), cv_ref[0, 0].astype(BF16),
                     keys, vals)
            pad = slice(past, kd_ref.shape[1])
            zk = jnp.zeros((kd_ref.shape[1] - past, DIFF_W), BF16)
            zh = jnp.zeros((kd_ref.shape[1] - past, MLA_HEAD_PAD), BF16)
            put_keys((0, pad), zk, zk, [zh] * MLA_HEADS, (zk, zk))
    else:
        for sub in range(n_sub):
            new_tokens(sub)


def _prep_call(latent, layer, x, seq_len, mods3, wts, tables=None, caches=None):
    n_tok = x.shape[0]
    tm = TOKEN_BLOCK
    n_req = n_tok // seq_len
    n_new = seq_len // tm if latent else 1
    weights = [wts["w_in"], wts["gq"], wts["gkv"], wts["w_uq"], wts["w_uk"], wts["w_uv"]]
    if latent:
        grid = (n_req, n_new + 1)
        key_rows = (n_new + 1) * tm
        blk = lambda b, i: b * n_new + jnp.minimum(i, n_new - 1)
        kblk = lambda b, i: (b, i)
        key_block = (1, tm)
    else:
        assert tm % seq_len == 0 and TOKEN_SUB == seq_len
        grid = (n_tok // tm,)
        key_rows = seq_len
        blk = lambda i: i
        kblk = lambda i: (i, 0)
        key_block = (tm // seq_len, seq_len)
    row = _mod_row(latent, layer, seq_len, tm)
    tok = lambda w: pl.BlockSpec((tm, w), lambda *g: (blk(*g), 0))
    heads = lambda n, w: pl.BlockSpec((n, tm, w), lambda *g: (0, blk(*g), 0))
    keys2 = lambda w: pl.BlockSpec(key_block + (w,), lambda *g: kblk(*g) + (0,))
    keys3 = lambda n, w: pl.BlockSpec((key_block[0], n, key_block[1], w),
                                      lambda *g: (kblk(*g)[0], 0, kblk(*g)[1], 0))
    in_specs = ([tok(D_MODEL),
                 pl.BlockSpec((1, 1, D_MODEL), lambda *g: (row(blk(*g)), 0, 0)),
                 pl.BlockSpec((1, 1, D_MODEL), lambda *g: (row(blk(*g)) + 1, 0, 0))]
                + [_full(w.shape) for w in weights])
    args = [x, mods3, mods3] + weights
    out_shape = [jax.ShapeDtypeStruct((n_tok, DIFF_W), BF16),
                 jax.ShapeDtypeStruct((n_req, key_rows, DIFF_W), BF16),
                 jax.ShapeDtypeStruct((n_req, key_rows, DIFF_W), BF16),
                 jax.ShapeDtypeStruct((S5_W // LANES, n_tok, LANES), F32),
                 jax.ShapeDtypeStruct((MLA_HEADS, n_tok, MLA_HEAD_PAD), BF16),
                 jax.ShapeDtypeStruct((n_req, MLA_HEADS, key_rows, MLA_HEAD_PAD), BF16),
                 jax.ShapeDtypeStruct((n_req, 2, key_rows, 256), BF16)]
    out_specs = [tok(DIFF_W), keys2(DIFF_W), keys2(DIFF_W), heads(S5_W // LANES, LANES),
                 heads(MLA_HEADS, MLA_HEAD_PAD), keys3(MLA_HEADS, MLA_HEAD_PAD), keys3(2, 256)]
    if latent:
        past = caches[0].shape[2]
        assert past < tm
        in_specs += [pl.BlockSpec((tm, LANES), lambda b, i: (jnp.minimum(i, n_new - 1), 0))] * 6
        in_specs += [pl.BlockSpec((1, 1, past, w), lambda b, i: (b, layer, 0, 0))
                     for w in (DIFF_W, DIFF_W, LANES, LANES)]
        args += list(tables) + list(caches)
    else:
        out_shape += [jax.ShapeDtypeStruct((n_tok, DIFF_W), F32),
                      jax.ShapeDtypeStruct((n_tok, DIFF_W), F32),
                      jax.ShapeDtypeStruct((n_tok, MLA_KV_RANK), F32),
                      jax.ShapeDtypeStruct((n_tok, LANES), F32)]
        out_specs += [tok(DIFF_W), tok(DIFF_W), tok(MLA_KV_RANK), tok(LANES)]
    body = functools.partial(_prep_kernel, latent=latent, n_new=n_new,
                             q_scale_diff=DIFF_HEAD_DIM ** -0.5 * LOG2E,
                             q_scale_mla=(MLA_NOPE + MLA_ROPE) ** -0.5 * LOG2E)
    return pl.pallas_call(
        body, grid=grid, in_specs=in_specs, out_specs=out_specs, out_shape=out_shape,
        compiler_params=_cparams(len(grid)),
        name="prep_lat" if latent else "prep_ctx",
    )(*args)


def _softmax_maps(n_maps, scores_fn, values_fn, emit_fn, s_refs):
    def stage_a(i):
        s = scores_fn(i)
        s_refs[i % 2][...] = s
        return jnp.max(s, axis=-1, keepdims=True)

    def stage_b(i, m):
        e = jnp.exp2(s_refs[i % 2][...] - m)
        denom = jnp.sum(e, axis=-1, keepdims=True)
        emit_fn(i, _dot(e.astype(BF16), values_fn(i)), denom)

    m_prev = stage_a(0)
    for i in range(1, n_maps):
        m_cur = stage_a(i)
        stage_b(i - 1, m_prev)
        m_prev = m_cur
    stage_b(n_maps - 1, m_prev)


def _diff_lambda(lq1_ref, lk1_ref, lq2_ref, lk2_ref, lam_init):
    s1 = jnp.sum(lq1_ref[...] * lk1_ref[...], axis=-1, keepdims=True)
    s2 = jnp.sum(lq2_ref[...] * lk2_ref[...], axis=-1, keepdims=True)
    return jnp.exp(s1) - jnp.exp(s2) + lam_init


def _diff_attn_kernel(q_ref, k_ref, v_ref, lq1_ref, lk1_ref, lq2_ref, lk2_ref, g_ref, o_ref,
                      acc_ref, s0_ref, s1_ref, *, lam_init):
    q = q_ref[...]
    lam = _diff_lambda(lq1_ref, lk1_ref, lq2_ref, lk2_ref, lam_init)
    lane = lax.broadcasted_iota(jnp.int32, (1, DIFF_W), 1)
    acc_ref[...] = jnp.zeros_like(acc_ref)

    def scores(hc):
        lo = hc * DIFF_HEAD_DIM
        qm = q * jnp.where((lane >= lo) & (lane < lo + DIFF_HEAD_DIM), 1.0, 0.0).astype(BF16)
        return _dot_nt(qm, k_ref[0])

    def emit(hc, pv, denom):
        hlo = (hc // 2) * DIFF_V_DIM
        hmask = (lane >= hlo) & (lane < hlo + DIFF_V_DIM)
        coef = 1.0 if hc % 2 == 0 else -lam
        acc_ref[...] += jnp.where(hmask, pv * (coef / denom), 0.0)

    _softmax_maps(2 * DIFF_HEADS, scores, lambda hc: v_ref[0], emit, (s0_ref, s1_ref))

    o = acc_ref[...]
    sq = o * o
    ms = jnp.zeros_like(o)
    for h in range(DIFF_HEADS):
        hmask = (lane >= h * DIFF_V_DIM) & (lane < (h + 1) * DIFF_V_DIM)
        tot = jnp.sum(jnp.where(hmask, sq, 0.0), axis=-1, keepdims=True)
        ms = jnp.where(hmask, tot * (1.0 / DIFF_V_DIM), ms)
    o_ref[...] = (o * lax.rsqrt(ms + RMS_EPS) * g_ref[...] * (1.0 - lam_init)).astype(BF16)


def _diff_attn_call(layer, seq_len, n_keys, qd, kd, vd, lam_params, g_tiled):
    n_tok = qd.shape[0]
    tq = min(Q_BLOCK, seq_len)
    nq = seq_len // tq
    lam_init = 0.8 - 0.6 * math.exp(-0.3 * layer)
    keys = pl.BlockSpec((1, n_keys, DIFF_W), lambda b, i: (b, 0, 0))
    in_specs = ([pl.BlockSpec((tq, DIFF_W), lambda b, i: (b * nq + i, 0)), keys, keys]
                + [_full((1, DIFF_HEAD_DIM))] * 4 + [_full((1, DIFF_W))])
    return pl.pallas_call(
        functools.partial(_diff_attn_kernel, lam_init=lam_init),
        grid=(n_tok // seq_len, nq),
        in_specs=in_specs,
        out_specs=pl.BlockSpec((tq, DIFF_W), lambda b, i: (b * nq + i, 0)),
        out_shape=jax.ShapeDtypeStruct((n_tok, DIFF_W), BF16),
        scratch_shapes=[pltpu.VMEM((tq, DIFF_W), F32),
                        pltpu.VMEM((tq, n_keys), F32), pltpu.VMEM((tq, n_keys), F32)],
        compiler_params=_cparams(2),
        name="diff_attn",
    )(qd, kd, vd, *lam_params, g_tiled)


def _mla_attn_kernel(q_ref, k_ref, v_ref, o_ref, acc_ref, s0_ref, s1_ref):
    lane = lax.broadcasted_iota(jnp.int32, (1, 256), 1)
    acc_ref[...] = jnp.zeros_like(acc_ref)

    def emit(h, pv, denom):
        hlo = (h % 4) * MLA_V
        hmask = (lane >= hlo) & (lane < hlo + MLA_V)
        acc_ref[h // 4] += jnp.where(hmask, pv * (1.0 / denom), 0.0)

    _softmax_maps(MLA_HEADS, lambda h: _dot_nt(q_ref[h], k_ref[0, h]),
                  lambda h: v_ref[0, h // 4], emit, (s0_ref, s1_ref))
    o_ref[...] = jnp.concatenate([acc_ref[0], acc_ref[1]], axis=1).astype(BF16)


def _mla_attn_call(seq_len, n_keys, qc, kc, vm):
    n_tok = qc.shape[1]
    tq = min(Q_BLOCK, seq_len)
    nq = seq_len // tq
    in_specs = [pl.BlockSpec((MLA_HEADS, tq, MLA_HEAD_PAD), lambda b, i: (0, b * nq + i, 0)),
                pl.BlockSpec((1, MLA_HEADS, n_keys, MLA_HEAD_PAD), lambda b, i: (b, 0, 0, 0)),
                pl.BlockSpec((1, 2, n_keys, 256), lambda b, i: (b, 0, 0, 0))]
    return pl.pallas_call(
        _mla_attn_kernel,
        grid=(n_tok // seq_len, nq),
        in_specs=in_specs,
        out_specs=pl.BlockSpec((tq, MLA_W), lambda b, i: (b * nq + i, 0)),
        out_shape=jax.ShapeDtypeStruct((n_tok, MLA_W), BF16),
        scratch_shapes=[pltpu.VMEM((2, tq, 256), F32),
                        pltpu.VMEM((tq, n_keys), F32), pltpu.VMEM((tq, n_keys), F32)],
        compiler_params=_cparams(2),
        name="mla_attn",
    )(qc, kc, vm)


S5_CHUNK_STEPS = 64
S5_EPILOGUE_ROWS = 256


def _s5_kernel(*refs, steps, segmented):
    if segmented:
        (u_ref, bre_ref, bim_ref, cre_ref, cim_ref, lre_ref, lim_ref, ldt_ref, d_ref, wglu_ref,
         h0_ref, y_ref, up_ref, bur_ref, bui_ref, yacc_ref, inr_ref, ini_ref) = refs
    else:
        (u_ref, bre_ref, bim_ref, cre_ref, cim_ref, lre_ref, lim_ref, ldt_ref, d_ref, wglu_ref,
         y_ref, fin_ref, up_ref, bur_ref, bui_ref, yacc_ref) = refs
    tc = S5_CHUNK_STEPS
    rows_c = tc * SUBLANES
    n_chunks = steps // tc
    yacc_ref[...] = jnp.zeros_like(yacc_ref)

    def regroup(j, carry):
        dst = pl.ds(pl.multiple_of(j * SUBLANES, SUBLANES), SUBLANES)
        for half in range(S5_W // LANES):
            up_ref[dst, half * LANES:(half + 1) * LANES] = (
                u_ref[half, pl.ds(j, SUBLANES, stride=steps), :])
        return carry

    lax.fori_loop(0, steps, regroup, 0)

    for d in range(2):
        lam_re, lam_im = lre_ref[d:d + 1, :], lim_ref[d:d + 1, :]
        dt = jnp.exp(ldt_ref[d:d + 1, :])
        mag = jnp.exp(lam_re * dt)
        ang = lam_im * dt
        a_re, a_im = mag * jnp.cos(ang), mag * jnp.sin(ang)
        den = lam_re * lam_re + lam_im * lam_im
        n_re, n_im = a_re - 1.0, a_im
        f_re = (n_re * lam_re + n_im * lam_im) / den
        f_im = (n_im * lam_re - n_re * lam_im) / den
        bbar_re = (f_re * bre_ref[d] - f_im * bim_ref[d]).astype(BF16)
        bbar_im = (f_re * bim_ref[d] + f_im * bre_ref[d]).astype(BF16)
        c_re, c_im = cre_ref[d].astype(BF16), cim_ref[d].astype(BF16)
        ar8 = jnp.broadcast_to(a_re, (SUBLANES, S5_N))
        ai8 = jnp.broadcast_to(a_im, (SUBLANES, S5_N))

        def run_pass(init, store, d=d, bbar_re=bbar_re, bbar_im=bbar_im, c_re=c_re, c_im=c_im,
                     ar8=ar8, ai8=ai8):
            def chunk(ci, carry):
                c = ci if d == 0 else n_chunks - 1 - ci
                rows = pl.ds(pl.multiple_of(c * rows_c, rows_c), rows_c)
                ub = up_ref[rows, :].astype(BF16)
                bur_ref[...] = _dot(ub, bbar_re)
                bui_ref[...] = _dot(ub, bbar_im)

                def step(jj, hc):
                    j = jj if d == 0 else tc - 1 - jj
                    r = pl.ds(pl.multiple_of(j * SUBLANES, SUBLANES), SUBLANES)
                    hr, hi = hc
                    nr = ar8 * hr - ai8 * hi + bur_ref[r, :]
                    ni = ar8 * hi + ai8 * hr + bui_ref[r, :]
                    if store:
                        bur_ref[r, :] = nr
                        bui_ref[r, :] = ni
                    return nr, ni

                carry = lax.fori_loop(0, tc, step, carry)
                if store:
                    yc = (_dot(bur_ref[...].astype(BF16), c_re)
                          - _dot(bui_ref[...].astype(BF16), c_im))
                    for half in range(S5_W // LANES):
                        yacc_ref[half, rows, :] += yc[:, half * LANES:(half + 1) * LANES]
                return carry

            return lax.fori_loop(0, n_chunks, chunk, init)

        zeros = (jnp.zeros((SUBLANES, S5_N), F32), jnp.zeros((SUBLANES, S5_N), F32))
        if segmented:
            f_r, f_i = run_pass(zeros, False)
            p_re, p_im = a_re, a_im
            for _ in range(int(math.log2(steps))):
                p_re, p_im = p_re * p_re - p_im * p_im, 2.0 * p_re * p_im
            c_r, c_i = h0_ref[0, d, 0:1, :], h0_ref[0, d, 1:2, :]
            order = range(SUBLANES) if d == 0 else range(SUBLANES - 1, -1, -1)
            for s in order:
                inr_ref[s:s + 1, :] = c_r
                ini_ref[s:s + 1, :] = c_i
                c_r, c_i = (f_r[s:s + 1, :] + p_re * c_r - p_im * c_i,
                            f_i[s:s + 1, :] + p_re * c_i + p_im * c_r)
            run_pass((inr_ref[...], ini_ref[...]), True)
        else:
            f_r, f_i = run_pass(zeros, True)
            fin_ref[0, d, 0] = f_r
            fin_ref[0, d, 1] = f_i

    ep = S5_EPILOGUE_ROWS
    per_chain = steps // ep

    def epilogue(e, carry):
        chain, jc = e // per_chain, e % per_chain
        rows = pl.ds(pl.multiple_of(e * ep, ep), ep)
        src = pl.ds(jc * ep * SUBLANES + chain, ep, stride=SUBLANES)
        halves = range(S5_W // LANES)
        acc = jnp.concatenate([yacc_ref[half, src, :] for half in halves], axis=1)
        u = jnp.concatenate([u_ref[half, rows, :] for half in halves], axis=1)
        y = acc + u * d_ref[...]
        y = jax.nn.gelu(y, approximate=True)
        y_ref[rows, :] = y * jax.nn.sigmoid(_dot(y.astype(BF16), wglu_ref[...]))
        return carry

    lax.fori_loop(0, SUBLANES * per_chain, epilogue, 0)


def _s5_call(u, steps, s5w, h0=None):
    n_tok = u.shape[1]
    n_rows = steps * SUBLANES
    nb = n_tok // n_rows
    segmented = h0 is not None
    assert steps % S5_CHUNK_STEPS == 0 and steps % S5_EPILOGUE_ROWS == 0 and steps & (steps - 1) == 0
    consts = [s5w["b_re"], s5w["b_im"], s5w["c_re"], s5w["c_im"], s5w["lam_re"], s5w["lam_im"],
              s5w["log_dt"], s5w["d"], s5w["w_glu"]]
    in_specs = ([pl.BlockSpec((S5_W // LANES, n_rows, LANES), lambda b: (0, b, 0))]
                + [_full(c.shape) for c in consts])
    args = [u] + consts
    out_shape = [jax.ShapeDtypeStruct((n_tok, S5_W), F32)]
    out_specs = [pl.BlockSpec((n_rows, S5_W), lambda b: (b, 0))]
    scratch = [pltpu.VMEM((n_rows, S5_W), F32),
               pltpu.VMEM((S5_CHUNK_STEPS * SUBLANES, S5_N), F32),
               pltpu.VMEM((S5_CHUNK_STEPS * SUBLANES, S5_N), F32),
               pltpu.VMEM((S5_W // LANES, n_rows, LANES), F32)]
    if segmented:
        in_specs.append(pl.BlockSpec((1, 2, 2, S5_N), lambda b: (b, 0, 0, 0)))
        args.append(h0)
        scratch += [pltpu.VMEM((SUBLANES, S5_N), F32), pltpu.VMEM((SUBLANES, S5_N), F32)]
    else:
        out_shape.append(jax.ShapeDtypeStruct((nb, 2, 2, SUBLANES, S5_N), F32))
        out_specs.append(pl.BlockSpec((1, 2, 2, SUBLANES, S5_N), lambda b: (b, 0, 0, 0, 0)))
    outs = pl.pallas_call(
        functools.partial(_s5_kernel, steps=steps, segmented=segmented),
        grid=(nb,), in_specs=in_specs, out_specs=out_specs, out_shape=out_shape,
        scratch_shapes=scratch, compiler_params=_cparams(1),
        name="s5_lat" if segmented else "s5_ctx",
    )(*args)
    return (outs[0], None) if segmented else (outs[0], outs[1])


FFN_CHUNK = FFN_HIDDEN // 2


def _post_kernel(x_ref, a_ref, s_ref, m_ref, g1_ref, sh2_ref, sc2_ref, g2_ref,
                 wo_ref, wg_ref, wu_ref, wd_ref, l1g_ref, l1b_ref, l2g_ref, l2b_ref, o_ref):
    for sub in range(x_ref.shape[0] // TOKEN_SUB):
        rows = slice(sub * TOKEN_SUB, (sub + 1) * TOKEN_SUB)
        x = x_ref[rows, :]
        mix = (_dot(a_ref[rows, :], wo_ref[0:256, :])
               + _dot(s_ref[rows, :].astype(BF16), wo_ref[256:512, :])
               + _dot(m_ref[rows, :], wo_ref[512:1024, :]))
        x1 = _layer_norm(ALPHA * x + g1_ref[0] * mix, l1g_ref[...], l1b_ref[...])
        hb = (x1 * (1.0 + sc2_ref[0]) + sh2_ref[0]).astype(BF16)
        f = jnp.zeros_like(x)
        for c in range(FFN_HIDDEN // FFN_CHUNK):
            cols = slice(c * FFN_CHUNK, (c + 1) * FFN_CHUNK)
            gate = _dot(hb, wg_ref[:, cols])
            up = _dot(hb, wu_ref[:, cols])
            act = (gate * jax.nn.sigmoid(gate) * up).astype(BF16)
            f = f + _dot(act, wd_ref[cols, :])
        o_ref[rows, :] = _layer_norm(ALPHA * x1 + g2_ref[0] * f, l2g_ref[...], l2b_ref[...])


def _post_call(latent, layer, seq_len, x, attn_d, y_s5, attn_m, mods3, wts):
    n_tok = x.shape[0]
    tm = TOKEN_BLOCK
    row = _mod_row(latent, layer, seq_len, tm)
    tok = lambda w: pl.BlockSpec((tm, w), lambda i: (i, 0))
    mod = lambda k: pl.BlockSpec((1, 1, D_MODEL), lambda i: (row(i) + k, 0, 0))
    weights = [wts["w_out"], wts["w_gate"], wts["w_up"], wts["w_down"],
               wts["ln1_g"], wts["ln1_b"], wts["ln2_g"], wts["ln2_b"]]
    w_specs = [pl.BlockSpec(w.shape, lambda i: (0, 0), pipeline_mode=pl.Buffered(1))
               for w in weights]
    return pl.pallas_call(
        _post_kernel,
        grid=(n_tok // tm,),
        in_specs=[tok(D_MODEL), tok(DIFF_W), tok(S5_W), tok(MLA_W), mod(2), mod(3), mod(4), mod(5)]
                 + w_specs,
        out_specs=tok(D_MODEL),
        out_shape=jax.ShapeDtypeStruct((n_tok, D_MODEL), F32),
        compiler_params=_cparams(1),
        name="post",
    )(x, attn_d, y_s5, attn_m, mods3, mods3, mods3, mods3, *weights)


def _rope_tables(length):
    rows = length // GRID_W
    row = np.repeat(np.arange(rows, dtype=np.float64), GRID_W)
    col = np.tile(np.arange(GRID_W, dtype=np.float64), rows)
    n_freq = DIFF_HEAD_DIM // 4
    inv = (ROPE_BASE ** (-np.arange(n_freq, dtype=np.float32) / np.float32(n_freq))).astype(np.float32)
    ang = np.concatenate([(row[:, None] * inv).astype(np.float32),
                          (col[:, None] * inv).astype(np.float32)], -1).astype(np.float64)
    cos, sin = np.cos(ang).astype(np.float32), np.sin(ang).astype(np.float32)
    zero = np.zeros_like(sin)
    cos32 = np.concatenate([cos, cos], -1)
    hi32 = np.concatenate([zero, sin], -1)
    lo32 = np.concatenate([-sin, zero], -1)
    diff = tuple(np.tile(t, (1, LANES // 32)) for t in (cos32, hi32, lo32))
    ones64, zeros64 = np.ones((length, 64), np.float32), np.zeros((length, 64), np.float32)
    ones32, zeros32 = np.ones((length, 32), np.float32), np.zeros((length, 32), np.float32)
    mla = (np.concatenate([ones64, cos32, ones32], -1),
           np.concatenate([zeros64, hi32, zeros32], -1),
           np.concatenate([zeros64, lo32, zeros32], -1))
    return tuple(jnp.asarray(t) for t in diff + mla)


def _block_diag(blocks):
    g, r, c = blocks.shape
    eye = jnp.eye(g, dtype=blocks.dtype)
    return (blocks[:, :, None, :] * eye[:, None, :, None]).reshape(g * r, g * c)


def _layer_weights(l, p):
    w_in = p["w_in"][l]
    krope_cols = jnp.pad(w_in[:, 1408:1440], ((0, 0), (64, 32)))
    w_in_ext = jnp.concatenate([w_in[:, :1408], krope_cols], axis=1).astype(BF16)
    w_uq = jnp.pad(p["mla_w_uq"][l].reshape(MLA_Q_RANK, MLA_HEADS, MLA_NOPE + MLA_ROPE),
                   ((0, 0), (0, 0), (0, MLA_HEAD_PAD - MLA_NOPE - MLA_ROPE)))
    w_uk = jnp.pad(p["mla_w_uk"][l].reshape(MLA_KV_RANK, MLA_HEADS, MLA_NOPE),
                   ((0, 0), (0, 0), (0, MLA_HEAD_PAD - MLA_NOPE)))
    prep = {
        "w_in": w_in_ext,
        "gq": p["mla_q_norm_g"][l].reshape(1, MLA_Q_RANK),
        "gkv": p["mla_kv_norm_g"][l].reshape(1, MLA_KV_RANK),
        "w_uq": w_uq.reshape(MLA_Q_RANK, MLA_HEADS * MLA_HEAD_PAD).astype(BF16),
        "w_uk": w_uk.reshape(MLA_KV_RANK, MLA_HEADS * MLA_HEAD_PAD).astype(BF16),
        "w_uv": p["mla_w_uv"][l].astype(BF16),
    }
    lam_params = [p[n][l].reshape(1, DIFF_HEAD_DIM) for n in ("diff_lq1", "diff_lk1", "diff_lq2", "diff_lk2")]
    g_tiled = jnp.tile(p["diff_norm_g"][l], DIFF_HEADS).reshape(1, DIFF_W)
    s5w = {
        "b_re": jnp.stack([_block_diag(jnp.swapaxes(p["s5_b_re"][l, d], 1, 2)) for d in range(2)]),
        "b_im": jnp.stack([_block_diag(jnp.swapaxes(p["s5_b_im"][l, d], 1, 2)) for d in range(2)]),
        "c_re": jnp.stack([_block_diag(jnp.swapaxes(p["s5_c_re"][l, d], 1, 2)) for d in range(2)]),
        "c_im": jnp.stack([_block_diag(jnp.swapaxes(p["s5_c_im"][l, d], 1, 2)) for d in range(2)]),
        "lam_re": p["s5_lam_re"][l].reshape(2, S5_N),
        "lam_im": p["s5_lam_im"][l].reshape(2, S5_N),
        "log_dt": jnp.repeat(p["s5_log_dt"][l], S5_STATE, axis=-1),
        "d": p["s5_d"][l].reshape(1, S5_W),
        "w_glu": p["s5_w_glu"][l].astype(BF16),
    }
    post = {
        "w_out": p["w_out"][l].astype(BF16),
        "w_gate": p["ffn_w_gate"][l].astype(BF16),
        "w_up": p["ffn_w_up"][l].astype(BF16),
        "w_down": p["ffn_w_down"][l].astype(BF16),
        "ln1_g": p["ln1_g"][l].reshape(1, D_MODEL), "ln1_b": p["ln1_b"][l].reshape(1, D_MODEL),
        "ln2_g": p["ln2_g"][l].reshape(1, D_MODEL), "ln2_b": p["ln2_b"][l].reshape(1, D_MODEL),
    }
    return prep, lam_params, g_tiled, s5w, post


def kernel(x_prompt, x_sample, c, cache_diff_k, cache_diff_v, cache_mla_ckv, cache_mla_krope, state_s5, c_ctx, w_ada, b_ada, w_in, w_out, diff_lq1, diff_lk1, diff_lq2, diff_lk2, diff_norm_g, s5_lam_re, s5_lam_im, s5_log_dt, s5_b_re, s5_b_im, s5_c_re, s5_c_im, s5_d, s5_w_glu, mla_q_norm_g, mla_w_uq, mla_kv_norm_g, mla_w_uk, mla_w_uv, ln1_g, ln1_b, ln2_g, ln2_b, ffn_w_gate, ffn_w_up, ffn_w_down):
    p = dict(w_in=w_in, w_out=w_out, diff_lq1=diff_lq1, diff_lk1=diff_lk1, diff_lq2=diff_lq2,
             diff_lk2=diff_lk2, diff_norm_g=diff_norm_g, s5_lam_re=s5_lam_re, s5_lam_im=s5_lam_im,
             s5_log_dt=s5_log_dt, s5_b_re=s5_b_re, s5_b_im=s5_b_im, s5_c_re=s5_c_re, s5_c_im=s5_c_im,
             s5_d=s5_d, s5_w_glu=s5_w_glu, mla_q_norm_g=mla_q_norm_g, mla_w_uq=mla_w_uq,
             mla_kv_norm_g=mla_kv_norm_g, mla_w_uk=mla_w_uk, mla_w_uv=mla_w_uv, ln1_g=ln1_g,
             ln1_b=ln1_b, ln2_g=ln2_g, ln2_b=ln2_b, ffn_w_gate=ffn_w_gate, ffn_w_up=ffn_w_up,
             ffn_w_down=ffn_w_down)
    bsz, seq, _ = x_prompt.shape
    dec_b, dec_seq, _ = x_sample.shape
    past = cache_diff_k.shape[2]
    assert bsz % SUBLANES == 0 and dec_seq % SUBLANES == 0

    cond = jnp.concatenate([c_ctx[None, :], c, jnp.zeros((ADA_ROWS - 1 - dec_b, D_MODEL), F32)], 0)
    mods3 = _ada_call(cond, w_ada, b_ada).reshape(DEPTH * ADA_ROWS * 6, 1, D_MODEL)

    tables = _rope_tables(dec_seq)
    caches = (cache_diff_k.reshape(dec_b, DEPTH, past, DIFF_W),
              cache_diff_v.reshape(dec_b, DEPTH, past, DIFF_W),
              cache_mla_ckv,
              jnp.pad(cache_mla_krope, ((0, 0), (0, 0), (0, 0), (MLA_NOPE, LANES - MLA_NOPE - MLA_ROPE))))
    h0_all = jnp.moveaxis(state_s5, -1, 3).reshape(dec_b, DEPTH, 2, 2, S5_N)

    y_ctx = x_prompt.reshape(bsz * seq, D_MODEL)
    y_lat = x_sample.reshape(dec_b * dec_seq, D_MODEL)
    new_k, new_v, new_ckv, new_kr, new_st = [], [], [], [], []
    for l in range(DEPTH):
        prep_w, lam_params, g_tiled, s5w, post_w = _layer_weights(l, p)

        qd, kd, vd, u, qc, kc, vm, k32, v32, ckv32, kr32 = _prep_call(
            False, l, y_ctx, seq, mods3, prep_w)
        attn_d = _diff_attn_call(l, seq, seq, qd, kd, vd, lam_params, g_tiled)
        attn_m = _mla_attn_call(seq, seq, qc, kc, vm)
        y_s5, fin = _s5_call(u, seq, s5w)
        y_ctx = _post_call(False, l, seq, y_ctx, attn_d, y_s5, attn_m, mods3, post_w)
        new_k.append(k32.reshape(bsz, seq, DIFF_HEADS, 2 * DIFF_HEAD_DIM))
        new_v.append(v32.reshape(bsz, seq, DIFF_HEADS, DIFF_V_DIM))
        new_ckv.append(ckv32.reshape(bsz, seq, MLA_KV_RANK))
        new_kr.append(kr32.reshape(bsz, seq, LANES)[:, :, MLA_NOPE:MLA_NOPE + MLA_ROPE])
        st = jnp.transpose(fin, (0, 3, 1, 4, 2)).reshape(bsz, 2, S5_GROUPS, S5_STATE, 2)
        new_st.append(st)

        qd, kd, vd, u, qc, kc, vm = _prep_call(True, l, y_lat, dec_seq, mods3, prep_w, tables, caches)
        attn_d = _diff_attn_call(l, dec_seq, dec_seq + past, qd, kd, vd, lam_params, g_tiled)
        attn_m = _mla_attn_call(dec_seq, dec_seq + past, qc, kc, vm)
        y_s5, _ = _s5_call(u, dec_seq // SUBLANES, s5w, h0_all[:, l])
        y_lat = _post_call(True, l, dec_seq, y_lat, attn_d, y_s5, attn_m, mods3, post_w)

    return (y_ctx.reshape(bsz, seq, D_MODEL), y_lat.reshape(dec_b, dec_seq, D_MODEL),
            jnp.stack(new_k, 1), jnp.stack(new_v, 1), jnp.stack(new_ckv, 1),
            jnp.stack(new_kr, 1), jnp.stack(new_st, 1))
```

```python
import functools
import math

import jax
import jax.numpy as jnp
import numpy as np
from jax import lax
from jax.experimental import pallas as pl
from jax.experimental.pallas import tpu as pltpu

F32 = jnp.float32
BF16 = jnp.bfloat16

D_MODEL = 1024
DEPTH = 2
GRID_W = 64
DIFF_HEADS = 4
DIFF_HEAD_DIM = 32
DIFF_V_DIM = 64
DIFF_W = 256
S5_CH = 16
S5_W = 256
S5_GROUPS = 16
S5_STATE = 64
S5_N = S5_GROUPS * S5_STATE
MLA_HEADS = 8
MLA_NOPE = 64
MLA_ROPE = 32
MLA_V = 64
MLA_Q_RANK = 256
MLA_KV_RANK = 128
MLA_W = 512
MLA_HEAD_PAD = 128
FFN_HIDDEN = 2816
ALPHA = (2 * DEPTH) ** 0.25
LN_EPS = 1e-5
RMS_EPS = 1e-6
ROPE_BASE = 10000.0
LOG2E = 1.4426950408889634

LANES = 128
SUBLANES = 8
TOKEN_BLOCK = 512
TOKEN_SUB = 256
Q_BLOCK = 256
KEY_PARTS = 4
VMEM_LIMIT = 56 * 1024 * 1024
ADA_ROWS = 8


def _cparams(n_axes):
    return pltpu.CompilerParams(dimension_semantics=("arbitrary",) * n_axes,
                                vmem_limit_bytes=VMEM_LIMIT)


def _full(shape):
    nd = len(shape)
    return pl.BlockSpec(shape, lambda *_: (0,) * nd)


def _dot(a, b):
    return jnp.dot(a, b, preferred_element_type=F32)


def _dot_nt(a, b):
    return lax.dot_general(a, b, (((1,), (1,)), ((), ())), preferred_element_type=F32)


def _layer_norm(x, g, b):
    mu = jnp.mean(x, axis=-1, keepdims=True)
    xc = x - mu
    var = jnp.mean(xc * xc, axis=-1, keepdims=True)
    return xc * lax.rsqrt(var + LN_EPS) * g + b


def _rms_norm(x, g):
    return x * lax.rsqrt(jnp.mean(x * x, axis=-1, keepdims=True) + RMS_EPS) * g


def _rope(x, cos, sin_hi, sin_lo):
    outs = []
    for j in range(x.shape[1] // LANES):
        xb = x[:, j * LANES:(j + 1) * LANES]
        outs.append(xb * cos + pltpu.roll(xb, 16, 1) * sin_hi + pltpu.roll(xb, LANES - 16, 1) * sin_lo)
    return outs[0] if len(outs) == 1 else jnp.concatenate(outs, axis=1)


def _ada_kernel(cond_ref, w_ref, b_ref, o_ref):
    c = cond_ref[...]
    s = c * jax.nn.sigmoid(c)
    o_ref[0] = _dot(s.astype(BF16), w_ref[0].astype(BF16)) + b_ref[0]


def _ada_call(cond, w_ada, b_ada):
    n_blk = 6
    return pl.pallas_call(
        _ada_kernel,
        grid=(DEPTH, n_blk),
        in_specs=[
            pl.BlockSpec((ADA_ROWS, D_MODEL), lambda l, j: (0, 0)),
            pl.BlockSpec((1, D_MODEL, D_MODEL), lambda l, j: (l, 0, j)),
            pl.BlockSpec((1, 1, D_MODEL), lambda l, j: (l, 0, j)),
        ],
        out_specs=pl.BlockSpec((1, ADA_ROWS, D_MODEL), lambda l, j: (l, 0, j)),
        out_shape=jax.ShapeDtypeStruct((DEPTH, ADA_ROWS, 6 * D_MODEL), F32),
        compiler_params=_cparams(2),
        name="ada",
    )(cond, w_ada, b_ada.reshape(DEPTH, 1, 6 * D_MODEL))


def _mod_row(latent, layer, seq_len, tm):
    base = layer * ADA_ROWS
    if latent:
        return lambda i: (base + 1 + i // (seq_len // tm)) * 6
    return lambda i: base * 6


def _mla_keys(ckv, kr_wide, wuk_ref, wuv_ref):
    cb = ckv.astype(BF16)
    kn = _dot(cb, wuk_ref[...])
    keys = [(kn[:, h * MLA_HEAD_PAD:(h + 1) * MLA_HEAD_PAD] + kr_wide).astype(BF16)
            for h in range(MLA_HEADS)]
    vv_t = _dot(cb, wuv_ref[...]).T.astype(BF16)
    return keys, vv_t


def _prep_kernel(*refs, latent, n_new, q_scale_diff, q_scale_mla):
    if latent:
        (x_ref, sh_ref, sc_ref, win_ref, gq_ref, gkv_ref, wuq_ref, wuk_ref, wuv_ref,
         cd_ref, shd_ref, sld_ref, cm_ref, shm_ref, slm_ref,
         ck_ref, cv_ref, cckv_ref, ckr_ref,
         qd_ref, kd_ref, vd_ref, u_ref, qc_ref, kc_ref, vm_ref) = refs
    else:
        (x_ref, sh_ref, sc_ref, win_ref, gq_ref, gkv_ref, wuq_ref, wuk_ref, wuv_ref,
         qd_ref, kd_ref, vd_ref, u_ref, qc_ref, kc_ref, vm_ref,
         k32_ref, v32_ref, ckv32_ref, kr32_ref) = refs

    def put_keys(where, dk_b, dv_t, keys, vv_t):
        lead, rows = where
        kd_ref[lead, rows, :] = dk_b
        vd_ref[lead, :, rows] = dv_t
        for hh in range(MLA_HEADS):
            kc_ref[lead, hh, rows, :] = keys[hh]
        vm_ref[lead, :, rows] = vv_t

    def new_tokens(sub):
        rows = slice(sub * TOKEN_SUB, (sub + 1) * TOKEN_SUB)
        h = x_ref[rows, :] * (1.0 + sc_ref[0]) + sh_ref[0]
        z = _dot(h.astype(BF16), win_ref[...])
        dq, dk, dv = z[:, 0:256], z[:, 256:512], z[:, 512:768]
        u = z[:, 768:1024]
        q_lat, kv_lat, kr = z[:, 1024:1280], z[:, 1280:1408], z[:, 1408:1536]
        ckv = _rms_norm(kv_lat, gkv_ref[...])
        qn = _rms_norm(q_lat, gq_ref[...])
        qc = _dot(qn.astype(BF16), wuq_ref[...])
        if latent:
            cd, shd, sld = cd_ref[rows, :], shd_ref[rows, :], sld_ref[rows, :]
            cm, shm, slm = cm_ref[rows, :], shm_ref[rows, :], slm_ref[rows, :]
            dq = _rope(dq, cd, shd, sld)
            dk = _rope(dk, cd, shd, sld)
            qc = _rope(qc, cm, shm, slm)
            kr = _rope(kr, cm, shm, slm)
        else:
            k32_ref[rows, :] = dk
            v32_ref[rows, :] = dv
            ckv32_ref[rows, :] = ckv
            kr32_ref[rows, :] = kr
        qd_ref[rows, :] = (dq * q_scale_diff).astype(BF16)
        u_ref[0, rows, :] = u[:, :LANES]
        u_ref[1, rows, :] = u[:, LANES:]
        qcs = (qc * q_scale_mla).astype(BF16)
        for hh in range(MLA_HEADS):
            qc_ref[hh, rows, :] = qcs[:, hh * MLA_HEAD_PAD:(hh + 1) * MLA_HEAD_PAD]
        keys, vv_t = _mla_keys(ckv, kr, wuk_ref, wuv_ref)
        where = (0, rows) if latent else (sub, slice(None))
        put_keys(where, dk.astype(BF16), dv.T.astype(BF16), keys, vv_t)

    n_sub = x_ref.shape[0] // TOKEN_SUB
    if latent:
        i = pl.program_id(1)

        @pl.when(i < n_new)
        def _():
            for sub in range(n_sub):
                new_tokens(sub)

        @pl.when(i == n_new)
        def _():
            past = ck_ref.shape[2]
            keys, vv_t = _mla_keys(cckv_ref[0, 0], ckr_ref[0, 0], wuk_ref, wuv_ref)
            put_keys((0, slice(0, past)), ck_ref[0, 0].astype(BF16),
                     cv_ref[0, 0].T.astype(BF16), keys, vv_t)
            n_pad = kd_ref.shape[1] - past
            put_keys((0, slice(past, kd_ref.shape[1])),
                     jnp.zeros((n_pad, DIFF_W), BF16), jnp.zeros((DIFF_W, n_pad), BF16),
                     [jnp.zeros((n_pad, MLA_HEAD_PAD), BF16)] * MLA_HEADS,
                     jnp.zeros((MLA_W, n_pad), BF16))
    else:
        for sub in range(n_sub):
            new_tokens(sub)


def _prep_call(latent, layer, x, seq_len, mods3, wts, tables=None, caches=None):
    n_tok = x.shape[0]
    tm = TOKEN_BLOCK
    n_req = n_tok // seq_len
    n_new = seq_len // tm if latent else 1
    weights = [wts["w_in"], wts["gq"], wts["gkv"], wts["w_uq"], wts["w_uk"], wts["w_uv"]]
    if latent:
        grid = (n_req, n_new + 1)
        key_rows = (n_new + 1) * tm
        blk = lambda b, i: b * n_new + jnp.minimum(i, n_new - 1)
        kblk = lambda b, i: (b, i)
        key_block = (1, tm)
    else:
        assert tm % seq_len == 0 and TOKEN_SUB == seq_len
        grid = (n_tok // tm,)
        key_rows = seq_len
        blk = lambda i: i
        kblk = lambda i: (i, 0)
        key_block = (tm // seq_len, seq_len)
    row = _mod_row(latent, layer, seq_len, tm)
    tok = lambda w: pl.BlockSpec((tm, w), lambda *g: (blk(*g), 0))
    heads = lambda n, w: pl.BlockSpec((n, tm, w), lambda *g: (0, blk(*g), 0))
    keys2 = lambda w: pl.BlockSpec(key_block + (w,), lambda *g: kblk(*g) + (0,))
    keys3 = lambda n, w: pl.BlockSpec((key_block[0], n, key_block[1], w),
                                      lambda *g: (kblk(*g)[0], 0, kblk(*g)[1], 0))
    vals_t = lambda w: pl.BlockSpec((key_block[0], w, key_block[1]),
                                    lambda *g: (kblk(*g)[0], 0, kblk(*g)[1]))
    in_specs = ([tok(D_MODEL),
                 pl.BlockSpec((1, 1, D_MODEL), lambda *g: (row(blk(*g)), 0, 0)),
                 pl.BlockSpec((1, 1, D_MODEL), lambda *g: (row(blk(*g)) + 1, 0, 0))]
                + [_full(w.shape) for w in weights])
    args = [x, mods3, mods3] + weights
    out_shape = [jax.ShapeDtypeStruct((n_tok, DIFF_W), BF16),
                 jax.ShapeDtypeStruct((n_req, key_rows, DIFF_W), BF16),
                 jax.ShapeDtypeStruct((n_req, DIFF_W, key_rows), BF16),
                 jax.ShapeDtypeStruct((S5_W // LANES, n_tok, LANES), F32),
                 jax.ShapeDtypeStruct((MLA_HEADS, n_tok, MLA_HEAD_PAD), BF16),
                 jax.ShapeDtypeStruct((n_req, MLA_HEADS, key_rows, MLA_HEAD_PAD), BF16),
                 jax.ShapeDtypeStruct((n_req, MLA_W, key_rows), BF16)]
    out_specs = [tok(DIFF_W), keys2(DIFF_W), vals_t(DIFF_W), heads(S5_W // LANES, LANES),
                 heads(MLA_HEADS, MLA_HEAD_PAD), keys3(MLA_HEADS, MLA_HEAD_PAD), vals_t(MLA_W)]
    if latent:
        past = caches[0].shape[2]
        assert past < tm
        in_specs += [pl.BlockSpec((tm, LANES), lambda b, i: (jnp.minimum(i, n_new - 1), 0))] * 6
        in_specs += [pl.BlockSpec((1, 1, past, w), lambda b, i: (b, layer, 0, 0))
                     for w in (DIFF_W, DIFF_W, LANES, LANES)]
        args += list(tables) + list(caches)
    else:
        out_shape += [jax.ShapeDtypeStruct((n_tok, DIFF_W), F32),
                      jax.ShapeDtypeStruct((n_tok, DIFF_W), F32),
                      jax.ShapeDtypeStruct((n_tok, MLA_KV_RANK), F32),
                      jax.ShapeDtypeStruct((n_tok, LANES), F32)]
        out_specs += [tok(DIFF_W), tok(DIFF_W), tok(MLA_KV_RANK), tok(LANES)]
    body = functools.partial(_prep_kernel, latent=latent, n_new=n_new,
                             q_scale_diff=DIFF_HEAD_DIM ** -0.5 * LOG2E,
                             q_scale_mla=(MLA_NOPE + MLA_ROPE) ** -0.5 * LOG2E)
    return pl.pallas_call(
        body, grid=grid, in_specs=in_specs, out_specs=out_specs, out_shape=out_shape,
        compiler_params=_cparams(len(grid)),
        name="prep_lat" if latent else "prep_ctx",
    )(*args)


def _softmax_maps(n_maps, scores_fn, values_fn, emit_fn, s_refs):
    n_keys = s_refs[0].shape[0]
    units = n_keys // LANES
    n_parts = min(KEY_PARTS, units)
    cuts = [LANES * (units * p // n_parts) for p in range(n_parts + 1)]
    key_parts = list(zip(cuts[:-1], cuts[1:]))

    def stage_a(i):
        m = None
        for lo, hi in key_parts:
            s = scores_fn(i, lo, hi)
            s_refs[i % 2][lo:hi, :] = s
            pm = jnp.max(s, axis=0, keepdims=True)
            m = pm if m is None else jnp.maximum(m, pm)
        return m

    def stage_b(i, m):
        v = values_fn(i)
        pv, denom = None, None
        for lo, hi in key_parts:
            e = jnp.exp2(s_refs[i % 2][lo:hi, :] - m)
            ps = jnp.sum(e, axis=0, keepdims=True)
            pp = _dot(v[:, lo:hi], e.astype(BF16))
            pv, denom = (pp, ps) if pv is None else (pv + pp, denom + ps)
        emit_fn(i, pv, denom)

    m_prev = stage_a(0)
    for i in range(1, n_maps):
        m_cur = stage_a(i)
        stage_b(i - 1, m_prev)
        m_prev = m_cur
    stage_b(n_maps - 1, m_prev)


def _diff_lambda(lq1_ref, lk1_ref, lq2_ref, lk2_ref, lam_init):
    s1 = jnp.sum(lq1_ref[...] * lk1_ref[...], axis=-1, keepdims=True)
    s2 = jnp.sum(lq2_ref[...] * lk2_ref[...], axis=-1, keepdims=True)
    return jnp.exp(s1) - jnp.exp(s2) + lam_init


def _diff_attn_kernel(q_ref, k_ref, v_ref, lq1_ref, lk1_ref, lq2_ref, lk2_ref, g_ref, o_ref,
                      acc_ref, s0_ref, s1_ref, *, lam_init):
    q = q_ref[...]
    lam = _diff_lambda(lq1_ref, lk1_ref, lq2_ref, lk2_ref, lam_init)
    lane = lax.broadcasted_iota(jnp.int32, (1, DIFF_W), 1)

    def scores(hc, lo_key, hi_key):
        lo = hc * DIFF_HEAD_DIM
        qm = q * jnp.where((lane >= lo) & (lane < lo + DIFF_HEAD_DIM), 1.0, 0.0).astype(BF16)
        return _dot_nt(k_ref[0, lo_key:hi_key, :], qm)

    def head_rows(hc):
        return slice((hc // 2) * DIFF_V_DIM, (hc // 2 + 1) * DIFF_V_DIM)

    def emit(hc, pv, denom):
        if hc % 2 == 0:
            acc_ref[head_rows(hc), :] = pv * (1.0 / denom)
        else:
            o = acc_ref[head_rows(hc), :] - pv * (lam / denom)
            ms = jnp.mean(o * o, axis=0, keepdims=True)
            acc_ref[head_rows(hc), :] = o * lax.rsqrt(ms + RMS_EPS)

    _softmax_maps(2 * DIFF_HEADS, scores, lambda hc: v_ref[0, head_rows(hc), :], emit,
                  (s0_ref, s1_ref))
    o_ref[...] = (acc_ref[...].T * g_ref[...] * (1.0 - lam_init)).astype(BF16)


def _diff_attn_call(layer, seq_len, n_keys, qd, kd, vd, lam_params, g_tiled):
    n_tok = qd.shape[0]
    tq = min(Q_BLOCK, seq_len)
    nq = seq_len // tq
    lam_init = 0.8 - 0.6 * math.exp(-0.3 * layer)
    in_specs = ([pl.BlockSpec((tq, DIFF_W), lambda b, i: (b * nq + i, 0)),
                 pl.BlockSpec((1, n_keys, DIFF_W), lambda b, i: (b, 0, 0)),
                 pl.BlockSpec((1, DIFF_W, n_keys), lambda b, i: (b, 0, 0))]
                + [_full((1, DIFF_HEAD_DIM))] * 4 + [_full((1, DIFF_W))])
    return pl.pallas_call(
        functools.partial(_diff_attn_kernel, lam_init=lam_init),
        grid=(n_tok // seq_len, nq),
        in_specs=in_specs,
        out_specs=pl.BlockSpec((tq, DIFF_W), lambda b, i: (b * nq + i, 0)),
        out_shape=jax.ShapeDtypeStruct((n_tok, DIFF_W), BF16),
        scratch_shapes=[pltpu.VMEM((DIFF_W, tq), F32),
                        pltpu.VMEM((n_keys, tq), F32), pltpu.VMEM((n_keys, tq), F32)],
        compiler_params=_cparams(2),
        name="diff_attn",
    )(qd, kd, vd, *lam_params, g_tiled)


def _mla_attn_kernel(q_ref, k_ref, v_ref, o_ref, acc_ref, s0_ref, s1_ref):
    def head_rows(h):
        return slice(h * MLA_V, (h + 1) * MLA_V)

    def emit(h, pv, denom):
        acc_ref[head_rows(h), :] = pv * (1.0 / denom)

    _softmax_maps(MLA_HEADS, lambda h, lo, hi: _dot_nt(k_ref[0, h, lo:hi, :], q_ref[h]),
                  lambda h: v_ref[0, head_rows(h), :], emit, (s0_ref, s1_ref))
    o_ref[...] = acc_ref[...].T.astype(BF16)


def _mla_attn_call(seq_len, n_keys, qc, kc, vm):
    n_tok = qc.shape[1]
    tq = min(Q_BLOCK, seq_len)
    nq = seq_len // tq
    in_specs = [pl.BlockSpec((MLA_HEADS, tq, MLA_HEAD_PAD), lambda b, i: (0, b * nq + i, 0)),
                pl.BlockSpec((1, MLA_HEADS, n_keys, MLA_HEAD_PAD), lambda b, i: (b, 0, 0, 0)),
                pl.BlockSpec((1, MLA_W, n_keys), lambda b, i: (b, 0, 0))]
    return pl.pallas_call(
        _mla_attn_kernel,
        grid=(n_tok // seq_len, nq),
        in_specs=in_specs,
        out_specs=pl.BlockSpec((tq, MLA_W), lambda b, i: (b * nq + i, 0)),
        out_shape=jax.ShapeDtypeStruct((n_tok, MLA_W), BF16),
        scratch_shapes=[pltpu.VMEM((MLA_W, tq), F32),
                        pltpu.VMEM((n_keys, tq), F32), pltpu.VMEM((n_keys, tq), F32)],
        compiler_params=_cparams(2),
        name="mla_attn",
    )(qc, kc, vm)


S5_CHUNK_STEPS = 64
S5_EPILOGUE_ROWS = 256


def _s5_kernel(*refs, steps, segmented):
    if segmented:
        (u_ref, bre_ref, bim_ref, cre_ref, cim_ref, lre_ref, lim_ref, ldt_ref, d_ref, wglu_ref,
         h0_ref, y_ref, up_ref, bur_ref, bui_ref, yacc_ref, inr_ref, ini_ref) = refs
    else:
        (u_ref, bre_ref, bim_ref, cre_ref, cim_ref, lre_ref, lim_ref, ldt_ref, d_ref, wglu_ref,
         y_ref, fin_ref, up_ref, bur_ref, bui_ref, yacc_ref) = refs
    tc = S5_CHUNK_STEPS
    rows_c = tc * SUBLANES
    n_chunks = steps // tc
    yacc_ref[...] = jnp.zeros_like(yacc_ref)

    def regroup(j, carry):
        dst = pl.ds(pl.multiple_of(j * SUBLANES, SUBLANES), SUBLANES)
        for half in range(S5_W // LANES):
            up_ref[dst, half * LANES:(half + 1) * LANES] = (
                u_ref[half, pl.ds(j, SUBLANES, stride=steps), :])
        return carry

    lax.fori_loop(0, steps, regroup, 0)

    for d in range(2):
        lam_re, lam_im = lre_ref[d:d + 1, :], lim_ref[d:d + 1, :]
        dt = jnp.exp(ldt_ref[d:d + 1, :])
        mag = jnp.exp(lam_re * dt)
        ang = lam_im * dt
        a_re, a_im = mag * jnp.cos(ang), mag * jnp.sin(ang)
        den = lam_re * lam_re + lam_im * lam_im
        n_re, n_im = a_re - 1.0, a_im
        f_re = (n_re * lam_re + n_im * lam_im) / den
        f_im = (n_im * lam_re - n_re * lam_im) / den
        bbar_re = (f_re * bre_ref[d] - f_im * bim_ref[d]).astype(BF16)
        bbar_im = (f_re * bim_ref[d] + f_im * bre_ref[d]).astype(BF16)
        c_re, c_im = cre_ref[d].astype(BF16), cim_ref[d].astype(BF16)
        ar8 = jnp.broadcast_to(a_re, (SUBLANES, S5_N))
        ai8 = jnp.broadcast_to(a_im, (SUBLANES, S5_N))

        def run_pass(init, store, d=d, bbar_re=bbar_re, bbar_im=bbar_im, c_re=c_re, c_im=c_im,
                     ar8=ar8, ai8=ai8):
            def chunk(ci, carry):
                c = ci if d == 0 else n_chunks - 1 - ci
                rows = pl.ds(pl.multiple_of(c * rows_c, rows_c), rows_c)
                ub = up_ref[rows, :].astype(BF16)
                bur_ref[...] = _dot(ub, bbar_re)
                bui_ref[...] = _dot(ub, bbar_im)

                def step(jj, hc):
                    j = jj if d == 0 else tc - 1 - jj
                    r = pl.ds(pl.multiple_of(j * SUBLANES, SUBLANES), SUBLANES)
                    hr, hi = hc
                    nr = ar8 * hr - ai8 * hi + bur_ref[r, :]
                    ni = ar8 * hi + ai8 * hr + bui_ref[r, :]
                    if store:
                        bur_ref[r, :] = nr
                        bui_ref[r, :] = ni
                    return nr, ni

                carry = lax.fori_loop(0, tc, step, carry)
                if store:
                    yc = (_dot(bur_ref[...].astype(BF16), c_re)
                          - _dot(bui_ref[...].astype(BF16), c_im))
                    for half in range(S5_W // LANES):
                        yacc_ref[half, rows, :] += yc[:, half * LANES:(half + 1) * LANES]
                return carry

            return lax.fori_loop(0, n_chunks, chunk, init)

        zeros = (jnp.zeros((SUBLANES, S5_N), F32), jnp.zeros((SUBLANES, S5_N), F32))
        if segmented:
            f_r, f_i = run_pass(zeros, False)
            p_re, p_im = a_re, a_im
            for _ in range(int(math.log2(steps))):
                p_re, p_im = p_re * p_re - p_im * p_im, 2.0 * p_re * p_im
            c_r, c_i = h0_ref[0, d, 0:1, :], h0_ref[0, d, 1:2, :]
            order = range(SUBLANES) if d == 0 else range(SUBLANES - 1, -1, -1)
            for s in order:
                inr_ref[s:s + 1, :] = c_r
                ini_ref[s:s + 1, :] = c_i
                c_r, c_i = (f_r[s:s + 1, :] + p_re * c_r - p_im * c_i,
                            f_i[s:s + 1, :] + p_re * c_i + p_im * c_r)
            run_pass((inr_ref[...], ini_ref[...]), True)
        else:
            f_r, f_i = run_pass(zeros, True)
            fin_ref[0, d, 0] = f_r
            fin_ref[0, d, 1] = f_i

    ep = S5_EPILOGUE_ROWS
    per_chain = steps // ep

    def epilogue(e, carry):
        chain, jc = e // per_chain, e % per_chain
        rows = pl.ds(pl.multiple_of(e * ep, ep), ep)
        src = pl.ds(jc * ep * SUBLANES + chain, ep, stride=SUBLANES)
        halves = range(S5_W // LANES)
        acc = jnp.concatenate([yacc_ref[half, src, :] for half in halves], axis=1)
        u = jnp.concatenate([u_ref[half, rows, :] for half in halves], axis=1)
        y = acc + u * d_ref[...]
        y = jax.nn.gelu(y, approximate=True)
        y_ref[rows, :] = y * jax.nn.sigmoid(_dot(y.astype(BF16), wglu_ref[...]))
        return carry

    lax.fori_loop(0, SUBLANES * per_chain, epilogue, 0)


def _s5_call(u, steps, s5w, h0=None):
    n_tok = u.shape[1]
    n_rows = steps * SUBLANES
    nb = n_tok // n_rows
    segmented = h0 is not None
    assert steps % S5_CHUNK_STEPS == 0 and steps % S5_EPILOGUE_ROWS == 0 and steps & (steps - 1) == 0
    consts = [s5w["b_re"], s5w["b_im"], s5w["c_re"], s5w["c_im"], s5w["lam_re"], s5w["lam_im"],
              s5w["log_dt"], s5w["d"], s5w["w_glu"]]
    in_specs = ([pl.BlockSpec((S5_W // LANES, n_rows, LANES), lambda b: (0, b, 0))]
                + [_full(c.shape) for c in consts])
    args = [u] + consts
    out_shape = [jax.ShapeDtypeStruct((n_tok, S5_W), F32)]
    out_specs = [pl.BlockSpec((n_rows, S5_W), lambda b: (b, 0))]
    scratch = [pltpu.VMEM((n_rows, S5_W), F32),
               pltpu.VMEM((S5_CHUNK_STEPS * SUBLANES, S5_N), F32),
               pltpu.VMEM((S5_CHUNK_STEPS * SUBLANES, S5_N), F32),
               pltpu.VMEM((S5_W // LANES, n_rows, LANES), F32)]
    if segmented:
        in_specs.append(pl.BlockSpec((1, 2, 2, S5_N), lambda b: (b, 0, 0, 0)))
        args.append(h0)
        scratch += [pltpu.VMEM((SUBLANES, S5_N), F32), pltpu.VMEM((SUBLANES, S5_N), F32)]
    else:
        out_shape.append(jax.ShapeDtypeStruct((nb, 2, 2, SUBLANES, S5_N), F32))
        out_specs.append(pl.BlockSpec((1, 2, 2, SUBLANES, S5_N), lambda b: (b, 0, 0, 0, 0)))
    outs = pl.pallas_call(
        functools.partial(_s5_kernel, steps=steps, segmented=segmented),
        grid=(nb,), in_specs=in_specs, out_specs=out_specs, out_shape=out_shape,
        scratch_shapes=scratch, compiler_params=_cparams(1),
        name="s5_lat" if segmented else "s5_ctx",
    )(*args)
    return (outs[0], None) if segmented else (outs[0], outs[1])


FFN_CHUNK = FFN_HIDDEN // 2


def _post_kernel(x_ref, a_ref, s_ref, m_ref, g1_ref, sh2_ref, sc2_ref, g2_ref,
                 wo_ref, wg_ref, wu_ref, wd_ref, l1g_ref, l1b_ref, l2g_ref, l2b_ref, o_ref):
    for sub in range(x_ref.shape[0] // TOKEN_SUB):
        rows = slice(sub * TOKEN_SUB, (sub + 1) * TOKEN_SUB)
        x = x_ref[rows, :]
        mix = (_dot(a_ref[rows, :], wo_ref[0:256, :])
               + _dot(s_ref[rows, :].astype(BF16), wo_ref[256:512, :])
               + _dot(m_ref[rows, :], wo_ref[512:1024, :]))
        x1 = _layer_norm(ALPHA * x + g1_ref[0] * mix, l1g_ref[...], l1b_ref[...])
        hb = (x1 * (1.0 + sc2_ref[0]) + sh2_ref[0]).astype(BF16)
        f = jnp.zeros_like(x)
        for c in range(FFN_HIDDEN // FFN_CHUNK):
            cols = slice(c * FFN_CHUNK, (c + 1) * FFN_CHUNK)
            gate = _dot(hb, wg_ref[:, cols])
            up = _dot(hb, wu_ref[:, cols])
            act = (gate * jax.nn.sigmoid(gate) * up).astype(BF16)
            f = f + _dot(act, wd_ref[cols, :])
        o_ref[rows, :] = _layer_norm(ALPHA * x1 + g2_ref[0] * f, l2g_ref[...], l2b_ref[...])


def _post_call(latent, layer, seq_len, x, attn_d, y_s5, attn_m, mods3, wts):
    n_tok = x.shape[0]
    tm = TOKEN_BLOCK
    row = _mod_row(latent, layer, seq_len, tm)
    tok = lambda w: pl.BlockSpec((tm, w), lambda i: (i, 0))
    mod = lambda k: pl.BlockSpec((1, 1, D_MODEL), lambda i: (row(i) + k, 0, 0))
    weights = [wts["w_out"], wts["w_gate"], wts["w_up"], wts["w_down"],
               wts["ln1_g"], wts["ln1_b"], wts["ln2_g"], wts["ln2_b"]]
    w_specs = [pl.BlockSpec(w.shape, lambda i: (0, 0), pipeline_mode=pl.Buffered(1))
               for w in weights]
    return pl.pallas_call(
        _post_kernel,
        grid=(n_tok // tm,),
        in_specs=[tok(D_MODEL), tok(DIFF_W), tok(S5_W), tok(MLA_W), mod(2), mod(3), mod(4), mod(5)]
                 + w_specs,
        out_specs=tok(D_MODEL),
        out_shape=jax.ShapeDtypeStruct((n_tok, D_MODEL), F32),
        compiler_params=_cparams(1),
        name="post",
    )(x, attn_d, y_s5, attn_m, mods3, mods3, mods3, mods3, *weights)


def _rope_tables(length):
    rows = length // GRID_W
    row = np.repeat(np.arange(rows, dtype=np.float64), GRID_W)
    col = np.tile(np.arange(GRID_W, dtype=np.float64), rows)
    n_freq = DIFF_HEAD_DIM // 4
    inv = (ROPE_BASE ** (-np.arange(n_freq, dtype=np.float32) / np.float32(n_freq))).astype(np.float32)
    ang = np.concatenate([(row[:, None] * inv).astype(np.float32),
                          (col[:, None] * inv).astype(np.float32)], -1).astype(np.float64)
    cos, sin = np.cos(ang).astype(np.float32), np.sin(ang).astype(np.float32)
    zero = np.zeros_like(sin)
    cos32 = np.concatenate([cos, cos], -1)
    hi32 = np.concatenate([zero, sin], -1)
    lo32 = np.concatenate([-sin, zero], -1)
    diff = tuple(np.tile(t, (1, LANES // 32)) for t in (cos32, hi32, lo32))
    ones64, zeros64 = np.ones((length, 64), np.float32), np.zeros((length, 64), np.float32)
    ones32, zeros32 = np.ones((length, 32), np.float32), np.zeros((length, 32), np.float32)
    mla = (np.concatenate([ones64, cos32, ones32], -1),
           np.concatenate([zeros64, hi32, zeros32], -1),
           np.concatenate([zeros64, lo32, zeros32], -1))
    return tuple(jnp.asarray(t) for t in diff + mla)


def _block_diag(blocks):
    g, r, c = blocks.shape
    eye = jnp.eye(g, dtype=blocks.dtype)
    return (blocks[:, :, None, :] * eye[:, None, :, None]).reshape(g * r, g * c)


def _layer_weights(l, p):
    w_in = p["w_in"][l]
    krope_cols = jnp.pad(w_in[:, 1408:1440], ((0, 0), (64, 32)))
    w_in_ext = jnp.concatenate([w_in[:, :1408], krope_cols], axis=1).astype(BF16)
    w_uq = jnp.pad(p["mla_w_uq"][l].reshape(MLA_Q_RANK, MLA_HEADS, MLA_NOPE + MLA_ROPE),
                   ((0, 0), (0, 0), (0, MLA_HEAD_PAD - MLA_NOPE - MLA_ROPE)))
    w_uk = jnp.pad(p["mla_w_uk"][l].reshape(MLA_KV_RANK, MLA_HEADS, MLA_NOPE),
                   ((0, 0), (0, 0), (0, MLA_HEAD_PAD - MLA_NOPE)))
    prep = {
        "w_in": w_in_ext,
        "gq": p["mla_q_norm_g"][l].reshape(1, MLA_Q_RANK),
        "gkv": p["mla_kv_norm_g"][l].reshape(1, MLA_KV_RANK),
        "w_uq": w_uq.reshape(MLA_Q_RANK, MLA_HEADS * MLA_HEAD_PAD).astype(BF16),
        "w_uk": w_uk.reshape(MLA_KV_RANK, MLA_HEADS * MLA_HEAD_PAD).astype(BF16),
        "w_uv": p["mla_w_uv"][l].astype(BF16),
    }
    lam_params = [p[n][l].reshape(1, DIFF_HEAD_DIM) for n in ("diff_lq1", "diff_lk1", "diff_lq2", "diff_lk2")]
    g_tiled = jnp.tile(p["diff_norm_g"][l], DIFF_HEADS).reshape(1, DIFF_W)
    s5w = {
        "b_re": jnp.stack([_block_diag(jnp.swapaxes(p["s5_b_re"][l, d], 1, 2)) for d in range(2)]),
        "b_im": jnp.stack([_block_diag(jnp.swapaxes(p["s5_b_im"][l, d], 1, 2)) for d in range(2)]),
        "c_re": jnp.stack([_block_diag(jnp.swapaxes(p["s5_c_re"][l, d], 1, 2)) for d in range(2)]),
        "c_im": jnp.stack([_block_diag(jnp.swapaxes(p["s5_c_im"][l, d], 1, 2)) for d in range(2)]),
        "lam_re": p["s5_lam_re"][l].reshape(2, S5_N),
        "lam_im": p["s5_lam_im"][l].reshape(2, S5_N),
        "log_dt": jnp.repeat(p["s5_log_dt"][l], S5_STATE, axis=-1),
        "d": p["s5_d"][l].reshape(1, S5_W),
        "w_glu": p["s5_w_glu"][l].astype(BF16),
    }
    post = {
        "w_out": p["w_out"][l].astype(BF16),
        "w_gate": p["ffn_w_gate"][l].astype(BF16),
        "w_up": p["ffn_w_up"][l].astype(BF16),
        "w_down": p["ffn_w_down"][l].astype(BF16),
        "ln1_g": p["ln1_g"][l].reshape(1, D_MODEL), "ln1_b": p["ln1_b"][l].reshape(1, D_MODEL),
        "ln2_g": p["ln2_g"][l].reshape(1, D_MODEL), "ln2_b": p["ln2_b"][l].reshape(1, D_MODEL),
    }
    return prep, lam_params, g_tiled, s5w, post


def kernel(x_prompt, x_sample, c, cache_diff_k, cache_diff_v, cache_mla_ckv, cache_mla_krope, state_s5, c_ctx, w_ada, b_ada, w_in, w_out, diff_lq1, diff_lk1, diff_lq2, diff_lk2, diff_norm_g, s5_lam_re, s5_lam_im, s5_log_dt, s5_b_re, s5_b_im, s5_c_re, s5_c_im, s5_d, s5_w_glu, mla_q_norm_g, mla_w_uq, mla_kv_norm_g, mla_w_uk, mla_w_uv, ln1_g, ln1_b, ln2_g, ln2_b, ffn_w_gate, ffn_w_up, ffn_w_down):
    p = dict(w_in=w_in, w_out=w_out, diff_lq1=diff_lq1, diff_lk1=diff_lk1, diff_lq2=diff_lq2,
             diff_lk2=diff_lk2, diff_norm_g=diff_norm_g, s5_lam_re=s5_lam_re, s5_lam_im=s5_lam_im,
             s5_log_dt=s5_log_dt, s5_b_re=s5_b_re, s5_b_im=s5_b_im, s5_c_re=s5_c_re, s5_c_im=s5_c_im,
             s5_d=s5_d, s5_w_glu=s5_w_glu, mla_q_norm_g=mla_q_norm_g, mla_w_uq=mla_w_uq,
             mla_kv_norm_g=mla_kv_norm_g, mla_w_uk=mla_w_uk, mla_w_uv=mla_w_uv, ln1_g=ln1_g,
             ln1_b=ln1_b, ln2_g=ln2_g, ln2_b=ln2_b, ffn_w_gate=ffn_w_gate, ffn_w_up=ffn_w_up,
             ffn_w_down=ffn_w_down)
    bsz, seq, _ = x_prompt.shape
    dec_b, dec_seq, _ = x_sample.shape
    past = cache_diff_k.shape[2]
    assert bsz % SUBLANES == 0 and dec_seq % SUBLANES == 0

    cond = jnp.concatenate([c_ctx[None, :], c, jnp.zeros((ADA_ROWS - 1 - dec_b, D_MODEL), F32)], 0)
    mods3 = _ada_call(cond, w_ada, b_ada).reshape(DEPTH * ADA_ROWS * 6, 1, D_MODEL)

    tables = _rope_tables(dec_seq)
    caches = (cache_diff_k.reshape(dec_b, DEPTH, past, DIFF_W),
              cache_diff_v.reshape(dec_b, DEPTH, past, DIFF_W),
              cache_mla_ckv,
              jnp.pad(cache_mla_krope, ((0, 0), (0, 0), (0, 0), (MLA_NOPE, LANES - MLA_NOPE - MLA_ROPE))))
    h0_all = jnp.moveaxis(state_s5, -1, 3).reshape(dec_b, DEPTH, 2, 2, S5_N)

    y_ctx = x_prompt.reshape(bsz * seq, D_MODEL)
    y_lat = x_sample.reshape(dec_b * dec_seq, D_MODEL)
    new_k, new_v, new_ckv, new_kr, new_st = [], [], [], [], []
    for l in range(DEPTH):
        prep_w, lam_params, g_tiled, s5w, post_w = _layer_weights(l, p)

        qd, kd, vd, u, qc, kc, vm, k32, v32, ckv32, kr32 = _prep_call(
            False, l, y_ctx, seq, mods3, prep_w)
        attn_d = _diff_attn_call(l, seq, seq, qd, kd, vd, lam_params, g_tiled)
        attn_m = _mla_attn_call(seq, seq, qc, kc, vm)
        y_s5, fin = _s5_call(u, seq, s5w)
        y_ctx = _post_call(False, l, seq, y_ctx, attn_d, y_s5, attn_m, mods3, post_w)
        new_k.append(k32.reshape(bsz, seq, DIFF_HEADS, 2 * DIFF_HEAD_DIM))
        new_v.append(v32.reshape(bsz, seq, DIFF_HEADS, DIFF_V_DIM))
        new_ckv.append(ckv32.reshape(bsz, seq, MLA_KV_RANK))
        new_kr.append(kr32.reshape(bsz, seq, LANES)[:, :, MLA_NOPE:MLA_NOPE + MLA_ROPE])
        st = jnp.transpose(fin, (0, 3, 1, 4, 2)).reshape(bsz, 2, S5_GROUPS, S5_STATE, 2)
        new_st.append(st)

        qd, kd, vd, u, qc, kc, vm = _prep_call(True, l, y_lat, dec_seq, mods3, prep_w, tables, caches)
        attn_d = _diff_attn_call(l, dec_seq, dec_seq + past, qd, kd, vd, lam_params, g_tiled)
        attn_m = _mla_attn_call(dec_seq, dec_seq + past, qc, kc, vm)
        y_s5, _ = _s5_call(u, dec_seq // SUBLANES, s5w, h0_all[:, l])
        y_lat = _post_call(True, l, dec_seq, y_lat, attn_d, y_s5, attn_m, mods3, post_w)

    return (y_ctx.reshape(bsz, seq, D_MODEL), y_lat.reshape(dec_b, dec_seq, D_MODEL),
            jnp.stack(new_k, 1), jnp.stack(new_v, 1), jnp.stack(new_ckv, 1),
            jnp.stack(new_kr, 1), jnp.stack(new_st, 1))
```

```python
import functools
import math

import jax
import jax.numpy as jnp
import numpy as np
from jax import lax
from jax.experimental import pallas as pl
from jax.experimental.pallas import tpu as pltpu

F32 = jnp.float32
BF16 = jnp.bfloat16

D_MODEL = 1024
DEPTH = 2
GRID_W = 64
DIFF_HEADS = 4
DIFF_HEAD_DIM = 32
DIFF_V_DIM = 64
DIFF_W = 256
S5_CH = 16
S5_W = 256
S5_GROUPS = 16
S5_STATE = 64
S5_N = S5_GROUPS * S5_STATE
MLA_HEADS = 8
MLA_NOPE = 64
MLA_ROPE = 32
MLA_V = 64
MLA_Q_RANK = 256
MLA_KV_RANK = 128
MLA_W = 512
MLA_HEAD_PAD = 128
FFN_HIDDEN = 2816
ALPHA = (2 * DEPTH) ** 0.25
LN_EPS = 1e-5
RMS_EPS = 1e-6
ROPE_BASE = 10000.0
LOG2E = 1.4426950408889634

LANES = 128
SUBLANES = 8
TOKEN_BLOCK = 512
TOKEN_SUB = 256
Q_BLOCK = 256
KEY_PARTS = 4
VMEM_LIMIT = 56 * 1024 * 1024
ADA_ROWS = 8


def _cparams(n_axes):
    return pltpu.CompilerParams(dimension_semantics=("arbitrary",) * n_axes,
                                vmem_limit_bytes=VMEM_LIMIT)


def _full(shape):
    nd = len(shape)
    return pl.BlockSpec(shape, lambda *_: (0,) * nd)


def _dot(a, b):
    return jnp.dot(a, b, preferred_element_type=F32)


def _dot_nt(a, b):
    return lax.dot_general(a, b, (((1,), (1,)), ((), ())), preferred_element_type=F32)


def _layer_norm(x, g, b):
    mu = jnp.mean(x, axis=-1, keepdims=True)
    xc = x - mu
    var = jnp.mean(xc * xc, axis=-1, keepdims=True)
    return xc * lax.rsqrt(var + LN_EPS) * g + b


def _rms_norm(x, g):
    return x * lax.rsqrt(jnp.mean(x * x, axis=-1, keepdims=True) + RMS_EPS) * g


def _rope(x, cos, sin_hi, sin_lo):
    outs = []
    for j in range(x.shape[1] // LANES):
        xb = x[:, j * LANES:(j + 1) * LANES]
        outs.append(xb * cos + pltpu.roll(xb, 16, 1) * sin_hi + pltpu.roll(xb, LANES - 16, 1) * sin_lo)
    return outs[0] if len(outs) == 1 else jnp.concatenate(outs, axis=1)


def _ada_kernel(cond_ref, w_ref, b_ref, o_ref):
    c = cond_ref[...]
    s = c * jax.nn.sigmoid(c)
    o_ref[0] = _dot(s.astype(BF16), w_ref[0].astype(BF16)) + b_ref[0]


def _ada_call(cond, w_ada, b_ada):
    n_blk = 6
    return pl.pallas_call(
        _ada_kernel,
        grid=(DEPTH, n_blk),
        in_specs=[
            pl.BlockSpec((ADA_ROWS, D_MODEL), lambda l, j: (0, 0)),
            pl.BlockSpec((1, D_MODEL, D_MODEL), lambda l, j: (l, 0, j)),
            pl.BlockSpec((1, 1, D_MODEL), lambda l, j: (l, 0, j)),
        ],
        out_specs=pl.BlockSpec((1, ADA_ROWS, D_MODEL), lambda l, j: (l, 0, j)),
        out_shape=jax.ShapeDtypeStruct((DEPTH, ADA_ROWS, 6 * D_MODEL), F32),
        compiler_params=_cparams(2),
        name="ada",
    )(cond, w_ada, b_ada.reshape(DEPTH, 1, 6 * D_MODEL))


def _mod_row(latent, layer, seq_len, tm):
    base = layer * ADA_ROWS
    if latent:
        return lambda i: (base + 1 + i // (seq_len // tm)) * 6
    return lambda i: base * 6


def _mla_keys(ckv, kr_wide, wuk_ref, wuv_ref):
    cb = ckv.astype(BF16)
    kn = _dot(cb, wuk_ref[...])
    keys = [(kn[:, h * MLA_HEAD_PAD:(h + 1) * MLA_HEAD_PAD] + kr_wide).astype(BF16)
            for h in range(MLA_HEADS)]
    vv_t = _dot(cb, wuv_ref[...]).T.astype(BF16)
    return keys, vv_t


def _prep_kernel(*refs, latent, n_new, q_scale_diff, q_scale_mla):
    if latent:
        (x_ref, sh_ref, sc_ref, win_ref, gq_ref, gkv_ref, wuq_ref, wuk_ref, wuv_ref,
         cd_ref, shd_ref, sld_ref, cm_ref, shm_ref, slm_ref,
         ck_ref, cv_ref, cckv_ref, ckr_ref,
         qd_ref, kd_ref, vd_ref, u_ref, qc_ref, kc_ref, vm_ref) = refs
    else:
        (x_ref, sh_ref, sc_ref, win_ref, gq_ref, gkv_ref, wuq_ref, wuk_ref, wuv_ref,
         qd_ref, kd_ref, vd_ref, u_ref, qc_ref, kc_ref, vm_ref,
         k32_ref, v32_ref, ckv32_ref, kr32_ref) = refs

    def put_keys(where, dk_b, dv_t, keys, vv_t):
        lead, rows = where
        kd_ref[lead, rows, :] = dk_b
        vd_ref[lead, :, rows] = dv_t
        for hh in range(MLA_HEADS):
            kc_ref[lead, hh, rows, :] = keys[hh]
        vm_ref[lead, :, rows] = vv_t

    def new_tokens(sub):
        rows = slice(sub * TOKEN_SUB, (sub + 1) * TOKEN_SUB)
        h = x_ref[rows, :] * (1.0 + sc_ref[0]) + sh_ref[0]
        z = _dot(h.astype(BF16), win_ref[...])
        dq, dk, dv = z[:, 0:256], z[:, 256:512], z[:, 512:768]
        u = z[:, 768:1024]
        q_lat, kv_lat, kr = z[:, 1024:1280], z[:, 1280:1408], z[:, 1408:1536]
        ckv = _rms_norm(kv_lat, gkv_ref[...])
        qn = _rms_norm(q_lat, gq_ref[...])
        qc = _dot(qn.astype(BF16), wuq_ref[...])
        if latent:
            cd, shd, sld = cd_ref[rows, :], shd_ref[rows, :], sld_ref[rows, :]
            cm, shm, slm = cm_ref[rows, :], shm_ref[rows, :], slm_ref[rows, :]
            dq = _rope(dq, cd, shd, sld)
            dk = _rope(dk, cd, shd, sld)
            qc = _rope(qc, cm, shm, slm)
            kr = _rope(kr, cm, shm, slm)
        else:
            k32_ref[rows, :] = dk
            v32_ref[rows, :] = dv
            ckv32_ref[rows, :] = ckv
            kr32_ref[rows, :] = kr
        qd_ref[rows, :] = (dq * q_scale_diff).astype(BF16)
        u_ref[0, rows, :] = u[:, :LANES]
        u_ref[1, rows, :] = u[:, LANES:]
        qcs = (qc * q_scale_mla).astype(BF16)
        for hh in range(MLA_HEADS):
            qc_ref[hh, rows, :] = qcs[:, hh * MLA_HEAD_PAD:(hh + 1) * MLA_HEAD_PAD]
        keys, vv_t = _mla_keys(ckv, kr, wuk_ref, wuv_ref)
        where = (0, rows) if latent else (sub, slice(None))
        put_keys(where, dk.astype(BF16), dv.T.astype(BF16), keys, vv_t)

    n_sub = x_ref.shape[0] // TOKEN_SUB
    if latent:
        i = pl.program_id(1)

        @pl.when(i < n_new)
        def _():
            for sub in range(n_sub):
                new_tokens(sub)

        @pl.when(i == n_new)
        def _():
            past = ck_ref.shape[2]
            keys, vv_t = _mla_keys(cckv_ref[0, 0], ckr_ref[0, 0], wuk_ref, wuv_ref)
            put_keys((0, slice(0, past)), ck_ref[0, 0].astype(BF16),
                     cv_ref[0, 0].T.astype(BF16), keys, vv_t)
            n_pad = kd_ref.shape[1] - past
            put_keys((0, slice(past, kd_ref.shape[1])),
                     jnp.zeros((n_pad, DIFF_W), BF16), jnp.zeros((DIFF_W, n_pad), BF16),
                     [jnp.zeros((n_pad, MLA_HEAD_PAD), BF16)] * MLA_HEADS,
                     jnp.zeros((MLA_W, n_pad), BF16))
    else:
        for sub in range(n_sub):
            new_tokens(sub)


def _prep_call(latent, layer, x, seq_len, mods3, wts, tables=None, caches=None):
    n_tok = x.shape[0]
    tm = TOKEN_BLOCK
    n_req = n_tok // seq_len
    n_new = seq_len // tm if latent else 1
    weights = [wts["w_in"], wts["gq"], wts["gkv"], wts["w_uq"], wts["w_uk"], wts["w_uv"]]
    if latent:
        grid = (n_req, n_new + 1)
        key_rows = (n_new + 1) * tm
        blk = lambda b, i: b * n_new + jnp.minimum(i, n_new - 1)
        kblk = lambda b, i: (b, i)
        key_block = (1, tm)
    else:
        assert tm % seq_len == 0 and TOKEN_SUB == seq_len
        grid = (n_tok // tm,)
        key_rows = seq_len
        blk = lambda i: i
        kblk = lambda i: (i, 0)
        key_block = (tm // seq_len, seq_len)
    row = _mod_row(latent, layer, seq_len, tm)
    tok = lambda w: pl.BlockSpec((tm, w), lambda *g: (blk(*g), 0))
    heads = lambda n, w: pl.BlockSpec((n, tm, w), lambda *g: (0, blk(*g), 0))
    keys2 = lambda w: pl.BlockSpec(key_block + (w,), lambda *g: kblk(*g) + (0,))
    keys3 = lambda n, w: pl.BlockSpec((key_block[0], n, key_block[1], w),
                                      lambda *g: (kblk(*g)[0], 0, kblk(*g)[1], 0))
    vals_t = lambda w: pl.BlockSpec((key_block[0], w, key_block[1]),
                                    lambda *g: (kblk(*g)[0], 0, kblk(*g)[1]))
    in_specs = ([tok(D_MODEL),
                 pl.BlockSpec((1, 1, D_MODEL), lambda *g: (row(blk(*g)), 0, 0)),
                 pl.BlockSpec((1, 1, D_MODEL), lambda *g: (row(blk(*g)) + 1, 0, 0))]
                + [_full(w.shape) for w in weights])
    args = [x, mods3, mods3] + weights
    out_shape = [jax.ShapeDtypeStruct((n_tok, DIFF_W), BF16),
                 jax.ShapeDtypeStruct((n_req, key_rows, DIFF_W), BF16),
                 jax.ShapeDtypeStruct((n_req, DIFF_W, key_rows), BF16),
                 jax.ShapeDtypeStruct((S5_W // LANES, n_tok, LANES), F32),
                 jax.ShapeDtypeStruct((MLA_HEADS, n_tok, MLA_HEAD_PAD), BF16),
                 jax.ShapeDtypeStruct((n_req, MLA_HEADS, key_rows, MLA_HEAD_PAD), BF16),
                 jax.ShapeDtypeStruct((n_req, MLA_W, key_rows), BF16)]
    out_specs = [tok(DIFF_W), keys2(DIFF_W), vals_t(DIFF_W), heads(S5_W // LANES, LANES),
                 heads(MLA_HEADS, MLA_HEAD_PAD), keys3(MLA_HEADS, MLA_HEAD_PAD), vals_t(MLA_W)]
    if latent:
        past = caches[0].shape[2]
        assert past < tm
        in_specs += [pl.BlockSpec((tm, LANES), lambda b, i: (jnp.minimum(i, n_new - 1), 0))] * 6
        in_specs += [pl.BlockSpec((1, 1, past, w), lambda b, i: (b, layer, 0, 0))
                     for w in (DIFF_W, DIFF_W, LANES, LANES)]
        args += list(tables) + list(caches)
    else:
        out_shape += [jax.ShapeDtypeStruct((n_tok, DIFF_W), F32),
                      jax.ShapeDtypeStruct((n_tok, DIFF_W), F32),
                      jax.ShapeDtypeStruct((n_tok, MLA_KV_RANK), F32),
                      jax.ShapeDtypeStruct((n_tok, LANES), F32)]
        out_specs += [tok(DIFF_W), tok(DIFF_W), tok(MLA_KV_RANK), tok(LANES)]
    body = functools.partial(_prep_kernel, latent=latent, n_new=n_new,
                             q_scale_diff=DIFF_HEAD_DIM ** -0.5 * LOG2E,
                             q_scale_mla=(MLA_NOPE + MLA_ROPE) ** -0.5 * LOG2E)
    return pl.pallas_call(
        body, grid=grid, in_specs=in_specs, out_specs=out_specs, out_shape=out_shape,
        compiler_params=_cparams(len(grid)),
        name="prep_lat" if latent else "prep_ctx",
    )(*args)


def _softmax_maps(n_maps, scores_fn, values_fn, emit_fn, s_refs):
    n_keys = s_refs[0].shape[0]
    units = n_keys // LANES
    n_parts = min(KEY_PARTS, units)
    cuts = [LANES * (units * p // n_parts) for p in range(n_parts + 1)]
    key_parts = list(zip(cuts[:-1], cuts[1:]))

    def stage_a(i):
        m = None
        for lo, hi in key_parts:
            s = scores_fn(i, lo, hi)
            s_refs[i % 2][lo:hi, :] = s
            pm = jnp.max(s, axis=0, keepdims=True)
            m = pm if m is None else jnp.maximum(m, pm)
        return m

    def stage_b(i, m):
        v = values_fn(i)
        pv, denom = None, None
        for lo, hi in key_parts:
            e = jnp.exp2(s_refs[i % 2][lo:hi, :] - m)
            ps = jnp.sum(e, axis=0, keepdims=True)
            pp = _dot(v[:, lo:hi], e.astype(BF16))
            pv, denom = (pp, ps) if pv is None else (pv + pp, denom + ps)
        emit_fn(i, pv, denom)

    m_prev = stage_a(0)
    for i in range(1, n_maps):
        m_cur = stage_a(i)
        stage_b(i - 1, m_prev)
        m_prev = m_cur
    stage_b(n_maps - 1, m_prev)


def _diff_lambda(lq1_ref, lk1_ref, lq2_ref, lk2_ref, lam_init):
    s1 = jnp.sum(lq1_ref[...] * lk1_ref[...], axis=-1, keepdims=True)
    s2 = jnp.sum(lq2_ref[...] * lk2_ref[...], axis=-1, keepdims=True)
    return jnp.exp(s1) - jnp.exp(s2) + lam_init


def _diff_attn_kernel(q_ref, k_ref, v_ref, lq1_ref, lk1_ref, lq2_ref, lk2_ref, g_ref, o_ref,
                      acc_ref, s0_ref, s1_ref, *, lam_init):
    q = q_ref[...]
    lam = _diff_lambda(lq1_ref, lk1_ref, lq2_ref, lk2_ref, lam_init)
    lane = lax.broadcasted_iota(jnp.int32, (1, DIFF_W), 1)

    def scores(hc, lo_key, hi_key):
        lo = hc * DIFF_HEAD_DIM
        qm = q * jnp.where((lane >= lo) & (lane < lo + DIFF_HEAD_DIM), 1.0, 0.0).astype(BF16)
        return _dot_nt(k_ref[0, lo_key:hi_key, :], qm)

    def head_rows(hc):
        return slice((hc // 2) * DIFF_V_DIM, (hc // 2 + 1) * DIFF_V_DIM)

    def emit(hc, pv, denom):
        if hc % 2 == 0:
            acc_ref[head_rows(hc), :] = pv * (1.0 / denom)
        else:
            o = acc_ref[head_rows(hc), :] - pv * (lam / denom)
            ms = jnp.mean(o * o, axis=0, keepdims=True)
            acc_ref[head_rows(hc), :] = o * lax.rsqrt(ms + RMS_EPS)

    _softmax_maps(2 * DIFF_HEADS, scores, lambda hc: v_ref[0, head_rows(hc), :], emit,
                  (s0_ref, s1_ref))
    o_ref[...] = (acc_ref[...].T * g_ref[...] * (1.0 - lam_init)).astype(BF16)


def _diff_attn_call(layer, seq_len, n_keys, qd, kd, vd, lam_params, g_tiled):
    n_tok = qd.shape[0]
    tq = min(Q_BLOCK, seq_len)
    nq = seq_len // tq
    lam_init = 0.8 - 0.6 * math.exp(-0.3 * layer)
    in_specs = ([pl.BlockSpec((tq, DIFF_W), lambda b, i: (b * nq + i, 0)),
                 pl.BlockSpec((1, n_keys, DIFF_W), lambda b, i: (b, 0, 0)),
                 pl.BlockSpec((1, DIFF_W, n_keys), lambda b, i: (b, 0, 0))]
                + [_full((1, DIFF_HEAD_DIM))] * 4 + [_full((1, DIFF_W))])
    return pl.pallas_call(
        functools.partial(_diff_attn_kernel, lam_init=lam_init),
        grid=(n_tok // seq_len, nq),
        in_specs=in_specs,
        out_specs=pl.BlockSpec((tq, DIFF_W), lambda b, i: (b * nq + i, 0)),
        out_shape=jax.ShapeDtypeStruct((n_tok, DIFF_W), BF16),
        scratch_shapes=[pltpu.VMEM((DIFF_W, tq), F32),
                        pltpu.VMEM((n_keys, tq), F32), pltpu.VMEM((n_keys, tq), F32)],
        compiler_params=_cparams(2),
        name="diff_attn",
    )(qd, kd, vd, *lam_params, g_tiled)


def _mla_attn_kernel(q_ref, k_ref, v_ref, o_ref, acc_ref, s0_ref, s1_ref):
    def head_rows(h):
        return slice(h * MLA_V, (h + 1) * MLA_V)

    def emit(h, pv, denom):
        acc_ref[head_rows(h), :] = pv * (1.0 / denom)

    _softmax_maps(MLA_HEADS, lambda h, lo, hi: _dot_nt(k_ref[0, h, lo:hi, :], q_ref[h]),
                  lambda h: v_ref[0, head_rows(h), :], emit, (s0_ref, s1_ref))
    o_ref[...] = acc_ref[...].T.astype(BF16)


def _mla_attn_call(seq_len, n_keys, qc, kc, vm):
    n_tok = qc.shape[1]
    tq = min(Q_BLOCK, seq_len)
    nq = seq_len // tq
    in_specs = [pl.BlockSpec((MLA_HEADS, tq, MLA_HEAD_PAD), lambda b, i: (0, b * nq + i, 0)),
                pl.BlockSpec((1, MLA_HEADS, n_keys, MLA_HEAD_PAD), lambda b, i: (b, 0, 0, 0)),
                pl.BlockSpec((1, MLA_W, n_keys), lambda b, i: (b, 0, 0))]
    return pl.pallas_call(
        _mla_attn_kernel,
        grid=(n_tok // seq_len, nq),
        in_specs=in_specs,
        out_specs=pl.BlockSpec((tq, MLA_W), lambda b, i: (b * nq + i, 0)),
        out_shape=jax.ShapeDtypeStruct((n_tok, MLA_W), BF16),
        scratch_shapes=[pltpu.VMEM((MLA_W, tq), F32),
                        pltpu.VMEM((n_keys, tq), F32), pltpu.VMEM((n_keys, tq), F32)],
        compiler_params=_cparams(2),
        name="mla_attn",
    )(qc, kc, vm)


S5_CHUNK_STEPS = 64
S5_EPILOGUE_ROWS = 256


def _s5_kernel(*refs, steps, segmented):
    if segmented:
        (u_ref, bre_ref, bim_ref, cre_ref, cim_ref, lre_ref, lim_ref, ldt_ref, d_ref, wglu_ref,
         h0_ref, y_ref, up_ref, bur0_ref, bui0_ref, bur1_ref, bui1_ref, yacc_ref,
         inr_ref, ini_ref) = refs
    else:
        (u_ref, bre_ref, bim_ref, cre_ref, cim_ref, lre_ref, lim_ref, ldt_ref, d_ref, wglu_ref,
         y_ref, fin_ref, up_ref, bur0_ref, bui0_ref, bur1_ref, bui1_ref, yacc_ref) = refs
    bu_refs = ((bur0_ref, bui0_ref), (bur1_ref, bui1_ref))
    tc = S5_CHUNK_STEPS
    rows_c = tc * SUBLANES
    n_chunks = steps // tc
    yacc_ref[...] = jnp.zeros_like(yacc_ref)

    def regroup(j, carry):
        dst = pl.ds(pl.multiple_of(j * SUBLANES, SUBLANES), SUBLANES)
        for half in range(S5_W // LANES):
            up_ref[dst, half * LANES:(half + 1) * LANES] = (
                u_ref[half, pl.ds(j, SUBLANES, stride=steps), :])
        return carry

    lax.fori_loop(0, steps, regroup, 0, unroll=8)

    for d in range(2):
        lam_re, lam_im = lre_ref[d:d + 1, :], lim_ref[d:d + 1, :]
        dt = jnp.exp(ldt_ref[d:d + 1, :])
        mag = jnp.exp(lam_re * dt)
        ang = lam_im * dt
        a_re, a_im = mag * jnp.cos(ang), mag * jnp.sin(ang)
        den = lam_re * lam_re + lam_im * lam_im
        n_re, n_im = a_re - 1.0, a_im
        f_re = (n_re * lam_re + n_im * lam_im) / den
        f_im = (n_im * lam_re - n_re * lam_im) / den
        bbar_re = (f_re * bre_ref[d] - f_im * bim_ref[d]).astype(BF16)
        bbar_im = (f_re * bim_ref[d] + f_im * bre_ref[d]).astype(BF16)
        c_re, c_im = cre_ref[d].astype(BF16), cim_ref[d].astype(BF16)
        ar8 = jnp.broadcast_to(a_re, (SUBLANES, S5_N))
        ai8 = jnp.broadcast_to(a_im, (SUBLANES, S5_N))

        def run_pass(init, store, d=d, bbar_re=bbar_re, bbar_im=bbar_im, c_re=c_re, c_im=c_im,
                     ar8=ar8, ai8=ai8):
            def chunk_rows(ci):
                c = jnp.minimum(ci, n_chunks - 1)
                c = c if d == 0 else n_chunks - 1 - c
                return pl.ds(pl.multiple_of(c * rows_c, rows_c), rows_c)

            def project_in(ci, slot):
                ub = up_ref[chunk_rows(ci), :].astype(BF16)
                bu_refs[slot][0][...] = _dot(ub, bbar_re)
                bu_refs[slot][1][...] = _dot(ub, bbar_im)

            def scan(slot, hc):
                bur_ref, bui_ref = bu_refs[slot]
                hr, hi = hc
                for jj in range(tc):
                    j = jj if d == 0 else tc - 1 - jj
                    r = slice(j * SUBLANES, (j + 1) * SUBLANES)
                    hr, hi = (ar8 * hr - ai8 * hi + bur_ref[r, :],
                              ar8 * hi + ai8 * hr + bui_ref[r, :])
                    if store:
                        bur_ref[r, :] = hr
                        bui_ref[r, :] = hi
                return hr, hi

            def project_out(ci, slot):
                bur_ref, bui_ref = bu_refs[slot]
                yc = (_dot(bur_ref[...].astype(BF16), c_re)
                      - _dot(bui_ref[...].astype(BF16), c_im))
                for half in range(S5_W // LANES):
                    yacc_ref[half, chunk_rows(ci), :] += yc[:, half * LANES:(half + 1) * LANES]

            def chunk_pair(cp, carry):
                for slot in range(2):
                    ci = 2 * cp + slot
                    project_in(ci + 1, 1 - slot)
                    carry = scan(slot, carry)
                    if store:
                        project_out(ci, slot)
                return carry

            project_in(0, 0)
            return lax.fori_loop(0, n_chunks // 2, chunk_pair, init)

        zeros = (jnp.zeros((SUBLANES, S5_N), F32), jnp.zeros((SUBLANES, S5_N), F32))
        if segmented:
            f_r, f_i = run_pass(zeros, False)
            p_re, p_im = a_re, a_im
            for _ in range(int(math.log2(steps))):
                p_re, p_im = p_re * p_re - p_im * p_im, 2.0 * p_re * p_im
            c_r, c_i = h0_ref[0, d, 0:1, :], h0_ref[0, d, 1:2, :]
            order = range(SUBLANES) if d == 0 else range(SUBLANES - 1, -1, -1)
            for s in order:
                inr_ref[s:s + 1, :] = c_r
                ini_ref[s:s + 1, :] = c_i
                c_r, c_i = (f_r[s:s + 1, :] + p_re * c_r - p_im * c_i,
                            f_i[s:s + 1, :] + p_re * c_i + p_im * c_r)
            run_pass((inr_ref[...], ini_ref[...]), True)
        else:
            f_r, f_i = run_pass(zeros, True)
            fin_ref[0, d, 0] = f_r
            fin_ref[0, d, 1] = f_i

    ep = S5_EPILOGUE_ROWS
    per_chain = steps // ep

    def epilogue(e, carry):
        chain, jc = e // per_chain, e % per_chain
        rows = pl.ds(pl.multiple_of(e * ep, ep), ep)
        src = pl.ds(jc * ep * SUBLANES + chain, ep, stride=SUBLANES)
        halves = range(S5_W // LANES)
        acc = jnp.concatenate([yacc_ref[half, src, :] for half in halves], axis=1)
        u = jnp.concatenate([u_ref[half, rows, :] for half in halves], axis=1)
        y = acc + u * d_ref[...]
        y = jax.nn.gelu(y, approximate=True)
        y_ref[rows, :] = y * jax.nn.sigmoid(_dot(y.astype(BF16), wglu_ref[...]))
        return carry

    lax.fori_loop(0, SUBLANES * per_chain, epilogue, 0)


def _s5_call(u, steps, s5w, h0=None):
    n_tok = u.shape[1]
    n_rows = steps * SUBLANES
    nb = n_tok // n_rows
    segmented = h0 is not None
    assert steps % (2 * S5_CHUNK_STEPS) == 0 and steps % S5_EPILOGUE_ROWS == 0
    assert steps & (steps - 1) == 0
    consts = [s5w["b_re"], s5w["b_im"], s5w["c_re"], s5w["c_im"], s5w["lam_re"], s5w["lam_im"],
              s5w["log_dt"], s5w["d"], s5w["w_glu"]]
    in_specs = ([pl.BlockSpec((S5_W // LANES, n_rows, LANES), lambda b: (0, b, 0))]
                + [pl.BlockSpec(c.shape, lambda b, nd=c.ndim: (0,) * nd, pipeline_mode=pl.Buffered(1))
                   for c in consts])
    args = [u] + consts
    out_shape = [jax.ShapeDtypeStruct((n_tok, S5_W), F32)]
    out_specs = [pl.BlockSpec((n_rows, S5_W), lambda b: (b, 0))]
    scratch = ([pltpu.VMEM((n_rows, S5_W), F32)]
               + [pltpu.VMEM((S5_CHUNK_STEPS * SUBLANES, S5_N), F32)] * 4
               + [pltpu.VMEM((S5_W // LANES, n_rows, LANES), F32)])
    if segmented:
        in_specs.append(pl.BlockSpec((1, 2, 2, S5_N), lambda b: (b, 0, 0, 0)))
        args.append(h0)
        scratch += [pltpu.VMEM((SUBLANES, S5_N), F32), pltpu.VMEM((SUBLANES, S5_N), F32)]
    else:
        out_shape.append(jax.ShapeDtypeStruct((nb, 2, 2, SUBLANES, S5_N), F32))
        out_specs.append(pl.BlockSpec((1, 2, 2, SUBLANES, S5_N), lambda b: (b, 0, 0, 0, 0)))
    outs = pl.pallas_call(
        functools.partial(_s5_kernel, steps=steps, segmented=segmented),
        grid=(nb,), in_specs=in_specs, out_specs=out_specs, out_shape=out_shape,
        scratch_shapes=scratch, compiler_params=_cparams(1),
        name="s5_lat" if segmented else "s5_ctx",
    )(*args)
    return (outs[0], None) if segmented else (outs[0], outs[1])


FFN_CHUNK = FFN_HIDDEN // 2


def _post_kernel(x_ref, a_ref, s_ref, m_ref, g1_ref, sh2_ref, sc2_ref, g2_ref,
                 wo_ref, wg_ref, wu_ref, wd_ref, l1g_ref, l1b_ref, l2g_ref, l2b_ref, o_ref):
    for sub in range(x_ref.shape[0] // TOKEN_SUB):
        rows = slice(sub * TOKEN_SUB, (sub + 1) * TOKEN_SUB)
        x = x_ref[rows, :]
        mix = (_dot(a_ref[rows, :], wo_ref[0:256, :])
               + _dot(s_ref[rows, :].astype(BF16), wo_ref[256:512, :])
               + _dot(m_ref[rows, :], wo_ref[512:1024, :]))
        x1 = _layer_norm(ALPHA * x + g1_ref[0] * mix, l1g_ref[...], l1b_ref[...])
        hb = (x1 * (1.0 + sc2_ref[0]) + sh2_ref[0]).astype(BF16)
        f = jnp.zeros_like(x)
        for c in range(FFN_HIDDEN // FFN_CHUNK):
            cols = slice(c * FFN_CHUNK, (c + 1) * FFN_CHUNK)
            gate = _dot(hb, wg_ref[:, cols])
            up = _dot(hb, wu_ref[:, cols])
            act = (gate * jax.nn.sigmoid(gate) * up).astype(BF16)
            f = f + _dot(act, wd_ref[cols, :])
        o_ref[rows, :] = _layer_norm(ALPHA * x1 + g2_ref[0] * f, l2g_ref[...], l2b_ref[...])


def _post_call(latent, layer, seq_len, x, attn_d, y_s5, attn_m, mods3, wts):
    n_tok = x.shape[0]
    tm = TOKEN_BLOCK
    row = _mod_row(latent, layer, seq_len, tm)
    tok = lambda w: pl.BlockSpec((tm, w), lambda i: (i, 0))
    mod = lambda k: pl.BlockSpec((1, 1, D_MODEL), lambda i: (row(i) + k, 0, 0))
    weights = [wts["w_out"], wts["w_gate"], wts["w_up"], wts["w_down"],
               wts["ln1_g"], wts["ln1_b"], wts["ln2_g"], wts["ln2_b"]]
    w_specs = [pl.BlockSpec(w.shape, lambda i: (0, 0), pipeline_mode=pl.Buffered(1))
               for w in weights]
    return pl.pallas_call(
        _post_kernel,
        grid=(n_tok // tm,),
        in_specs=[tok(D_MODEL), tok(DIFF_W), tok(S5_W), tok(MLA_W), mod(2), mod(3), mod(4), mod(5)]
                 + w_specs,
        out_specs=tok(D_MODEL),
        out_shape=jax.ShapeDtypeStruct((n_tok, D_MODEL), F32),
        compiler_params=_cparams(1),
        name="post",
    )(x, attn_d, y_s5, attn_m, mods3, mods3, mods3, mods3, *weights)


def _rope_tables(length):
    rows = length // GRID_W
    row = np.repeat(np.arange(rows, dtype=np.float64), GRID_W)
    col = np.tile(np.arange(GRID_W, dtype=np.float64), rows)
    n_freq = DIFF_HEAD_DIM // 4
    inv = (ROPE_BASE ** (-np.arange(n_freq, dtype=np.float32) / np.float32(n_freq))).astype(np.float32)
    ang = np.concatenate([(row[:, None] * inv).astype(np.float32),
                          (col[:, None] * inv).astype(np.float32)], -1).astype(np.float64)
    cos, sin = np.cos(ang).astype(np.float32), np.sin(ang).astype(np.float32)
    zero = np.zeros_like(sin)
    cos32 = np.concatenate([cos, cos], -1)
    hi32 = np.concatenate([zero, sin], -1)
    lo32 = np.concatenate([-sin, zero], -1)
    diff = tuple(np.tile(t, (1, LANES // 32)) for t in (cos32, hi32, lo32))
    ones64, zeros64 = np.ones((length, 64), np.float32), np.zeros((length, 64), np.float32)
    ones32, zeros32 = np.ones((length, 32), np.float32), np.zeros((length, 32), np.float32)
    mla = (np.concatenate([ones64, cos32, ones32], -1),
           np.concatenate([zeros64, hi32, zeros32], -1),
           np.concatenate([zeros64, lo32, zeros32], -1))
    return tuple(jnp.asarray(t) for t in diff + mla)


def _block_diag(blocks):
    g, r, c = blocks.shape
    eye = jnp.eye(g, dtype=blocks.dtype)
    return (blocks[:, :, None, :] * eye[:, None, :, None]).reshape(g * r, g * c)


def _layer_weights(l, p):
    w_in = p["w_in"][l]
    krope_cols = jnp.pad(w_in[:, 1408:1440], ((0, 0), (64, 32)))
    w_in_ext = jnp.concatenate([w_in[:, :1408], krope_cols], axis=1).astype(BF16)
    w_uq = jnp.pad(p["mla_w_uq"][l].reshape(MLA_Q_RANK, MLA_HEADS, MLA_NOPE + MLA_ROPE),
                   ((0, 0), (0, 0), (0, MLA_HEAD_PAD - MLA_NOPE - MLA_ROPE)))
    w_uk = jnp.pad(p["mla_w_uk"][l].reshape(MLA_KV_RANK, MLA_HEADS, MLA_NOPE),
                   ((0, 0), (0, 0), (0, MLA_HEAD_PAD - MLA_NOPE)))
    prep = {
        "w_in": w_in_ext,
        "gq": p["mla_q_norm_g"][l].reshape(1, MLA_Q_RANK),
        "gkv": p["mla_kv_norm_g"][l].reshape(1, MLA_KV_RANK),
        "w_uq": w_uq.reshape(MLA_Q_RANK, MLA_HEADS * MLA_HEAD_PAD).astype(BF16),
        "w_uk": w_uk.reshape(MLA_KV_RANK, MLA_HEADS * MLA_HEAD_PAD).astype(BF16),
        "w_uv": p["mla_w_uv"][l].astype(BF16),
    }
    lam_params = [p[n][l].reshape(1, DIFF_HEAD_DIM) for n in ("diff_lq1", "diff_lk1", "diff_lq2", "diff_lk2")]
    g_tiled = jnp.tile(p["diff_norm_g"][l], DIFF_HEADS).reshape(1, DIFF_W)
    s5w = {
        "b_re": jnp.stack([_block_diag(jnp.swapaxes(p["s5_b_re"][l, d], 1, 2)) for d in range(2)]),
        "b_im": jnp.stack([_block_diag(jnp.swapaxes(p["s5_b_im"][l, d], 1, 2)) for d in range(2)]),
        "c_re": jnp.stack([_block_diag(jnp.swapaxes(p["s5_c_re"][l, d], 1, 2)) for d in range(2)]),
        "c_im": jnp.stack([_block_diag(jnp.swapaxes(p["s5_c_im"][l, d], 1, 2)) for d in range(2)]),
        "lam_re": p["s5_lam_re"][l].reshape(2, S5_N),
        "lam_im": p["s5_lam_im"][l].reshape(2, S5_N),
        "log_dt": jnp.repeat(p["s5_log_dt"][l], S5_STATE, axis=-1),
        "d": p["s5_d"][l].reshape(1, S5_W),
        "w_glu": p["s5_w_glu"][l].astype(BF16),
    }
    post = {
        "w_out": p["w_out"][l].astype(BF16),
        "w_gate": p["ffn_w_gate"][l].astype(BF16),
        "w_up": p["ffn_w_up"][l].astype(BF16),
        "w_down": p["ffn_w_down"][l].astype(BF16),
        "ln1_g": p["ln1_g"][l].reshape(1, D_MODEL), "ln1_b": p["ln1_b"][l].reshape(1, D_MODEL),
        "ln2_g": p["ln2_g"][l].reshape(1, D_MODEL), "ln2_b": p["ln2_b"][l].reshape(1, D_MODEL),
    }
    return prep, lam_params, g_tiled, s5w, post


def kernel(x_prompt, x_sample, c, cache_diff_k, cache_diff_v, cache_mla_ckv, cache_mla_krope, state_s5, c_ctx, w_ada, b_ada, w_in, w_out, diff_lq1, diff_lk1, diff_lq2, diff_lk2, diff_norm_g, s5_lam_re, s5_lam_im, s5_log_dt, s5_b_re, s5_b_im, s5_c_re, s5_c_im, s5_d, s5_w_glu, mla_q_norm_g, mla_w_uq, mla_kv_norm_g, mla_w_uk, mla_w_uv, ln1_g, ln1_b, ln2_g, ln2_b, ffn_w_gate, ffn_w_up, ffn_w_down):
    p = dict(w_in=w_in, w_out=w_out, diff_lq1=diff_lq1, diff_lk1=diff_lk1, diff_lq2=diff_lq2,
             diff_lk2=diff_lk2, diff_norm_g=diff_norm_g, s5_lam_re=s5_lam_re, s5_lam_im=s5_lam_im,
             s5_log_dt=s5_log_dt, s5_b_re=s5_b_re, s5_b_im=s5_b_im, s5_c_re=s5_c_re, s5_c_im=s5_c_im,
             s5_d=s5_d, s5_w_glu=s5_w_glu, mla_q_norm_g=mla_q_norm_g, mla_w_uq=mla_w_uq,
             mla_kv_norm_g=mla_kv_norm_g, mla_w_uk=mla_w_uk, mla_w_uv=mla_w_uv, ln1_g=ln1_g,
             ln1_b=ln1_b, ln2_g=ln2_g, ln2_b=ln2_b, ffn_w_gate=ffn_w_gate, ffn_w_up=ffn_w_up,
             ffn_w_down=ffn_w_down)
    bsz, seq, _ = x_prompt.shape
    dec_b, dec_seq, _ = x_sample.shape
    past = cache_diff_k.shape[2]
    assert bsz % SUBLANES == 0 and dec_seq % SUBLANES == 0

    cond = jnp.concatenate([c_ctx[None, :], c, jnp.zeros((ADA_ROWS - 1 - dec_b, D_MODEL), F32)], 0)
    mods3 = _ada_call(cond, w_ada, b_ada).reshape(DEPTH * ADA_ROWS * 6, 1, D_MODEL)

    tables = _rope_tables(dec_seq)
    caches = (cache_diff_k.reshape(dec_b, DEPTH, past, DIFF_W),
              cache_diff_v.reshape(dec_b, DEPTH, past, DIFF_W),
              cache_mla_ckv,
              jnp.pad(cache_mla_krope, ((0, 0), (0, 0), (0, 0), (MLA_NOPE, LANES - MLA_NOPE - MLA_ROPE))))
    h0_all = jnp.moveaxis(state_s5, -1, 3).reshape(dec_b, DEPTH, 2, 2, S5_N)

    y_ctx = x_prompt.reshape(bsz * seq, D_MODEL)
    y_lat = x_sample.reshape(dec_b * dec_seq, D_MODEL)
    new_k, new_v, new_ckv, new_kr, new_st = [], [], [], [], []
    for l in range(DEPTH):
        prep_w, lam_params, g_tiled, s5w, post_w = _layer_weights(l, p)

        qd, kd, vd, u, qc, kc, vm, k32, v32, ckv32, kr32 = _prep_call(
            False, l, y_ctx, seq, mods3, prep_w)
        attn_d = _diff_attn_call(l, seq, seq, qd, kd, vd, lam_params, g_tiled)
        attn_m = _mla_attn_call(seq, seq, qc, kc, vm)
        y_s5, fin = _s5_call(u, seq, s5w)
        y_ctx = _post_call(False, l, seq, y_ctx, attn_d, y_s5, attn_m, mods3, post_w)
        new_k.append(k32.reshape(bsz, seq, DIFF_HEADS, 2 * DIFF_HEAD_DIM))
        new_v.append(v32.reshape(bsz, seq, DIFF_HEADS, DIFF_V_DIM))
        new_ckv.append(ckv32.reshape(bsz, seq, MLA_KV_RANK))
        new_kr.append(kr32.reshape(bsz, seq, LANES)[:, :, MLA_NOPE:MLA_NOPE + MLA_ROPE])
        st = jnp.transpose(fin, (0, 3, 1, 4, 2)).reshape(bsz, 2, S5_GROUPS, S5_STATE, 2)
        new_st.append(st)

        qd, kd, vd, u, qc, kc, vm = _prep_call(True, l, y_lat, dec_seq, mods3, prep_w, tables, caches)
        attn_d = _diff_attn_call(l, dec_seq, dec_seq + past, qd, kd, vd, lam_params, g_tiled)
        attn_m = _mla_attn_call(dec_seq, dec_seq + past, qc, kc, vm)
        y_s5, _ = _s5_call(u, dec_seq // SUBLANES, s5w, h0_all[:, l])
        y_lat = _post_call(True, l, dec_seq, y_lat, attn_d, y_s5, attn_m, mods3, post_w)

    return (y_ctx.reshape(bsz, seq, D_MODEL), y_lat.reshape(dec_b, dec_seq, D_MODEL),
            jnp.stack(new_k, 1), jnp.stack(new_v, 1), jnp.stack(new_ckv, 1),
            jnp.stack(new_kr, 1), jnp.stack(new_st, 1))
```

```python
import functools
import math

import jax
import jax.numpy as jnp
import numpy as np
from jax import lax
from jax.experimental import pallas as pl
from jax.experimental.pallas import tpu as pltpu

F32 = jnp.float32
BF16 = jnp.bfloat16

D_MODEL = 1024
DEPTH = 2
GRID_W = 64
DIFF_HEADS = 4
DIFF_HEAD_DIM = 32
DIFF_V_DIM = 64
DIFF_W = 256
S5_CH = 16
S5_W = 256
S5_GROUPS = 16
S5_STATE = 64
S5_N = S5_GROUPS * S5_STATE
MLA_HEADS = 8
MLA_NOPE = 64
MLA_ROPE = 32
MLA_V = 64
MLA_Q_RANK = 256
MLA_KV_RANK = 128
MLA_W = 512
MLA_HEAD_PAD = 128
FFN_HIDDEN = 2816
ALPHA = (2 * DEPTH) ** 0.25
LN_EPS = 1e-5
RMS_EPS = 1e-6
ROPE_BASE = 10000.0
LOG2E = 1.4426950408889634

LANES = 128
SUBLANES = 8
TOKEN_BLOCK = 512
TOKEN_SUB = 256
Q_BLOCK = 256
KEY_PARTS = 4
VMEM_LIMIT = 56 * 1024 * 1024
ADA_ROWS = 8


def _cparams(n_axes):
    return pltpu.CompilerParams(dimension_semantics=("arbitrary",) * n_axes,
                                vmem_limit_bytes=VMEM_LIMIT)


def _layer_spec(arr, layer, single_buffer=False):
    rest = arr.shape[1:]
    mode = dict(pipeline_mode=pl.Buffered(1)) if single_buffer else {}
    return pl.BlockSpec((None,) + rest, lambda *_: (layer,) + (0,) * len(rest), **mode)


def _dot(a, b):
    return jnp.dot(a, b, preferred_element_type=F32)


def _dot_nt(a, b):
    return lax.dot_general(a, b, (((1,), (1,)), ((), ())), preferred_element_type=F32)


def _layer_norm(x, g, b):
    mu = jnp.mean(x, axis=-1, keepdims=True)
    xc = x - mu
    var = jnp.mean(xc * xc, axis=-1, keepdims=True)
    return xc * lax.rsqrt(var + LN_EPS) * g + b


def _rms_norm(x, g):
    return x * lax.rsqrt(jnp.mean(x * x, axis=-1, keepdims=True) + RMS_EPS) * g


def _rope(x, cos, sin_hi, sin_lo):
    outs = []
    for j in range(x.shape[1] // LANES):
        xb = x[:, j * LANES:(j + 1) * LANES]
        outs.append(xb * cos + pltpu.roll(xb, 16, 1) * sin_hi + pltpu.roll(xb, LANES - 16, 1) * sin_lo)
    return outs[0] if len(outs) == 1 else jnp.concatenate(outs, axis=1)


def _ada_kernel(cond_ref, w_ref, b_ref, o_ref):
    c = cond_ref[...]
    s = c * jax.nn.sigmoid(c)
    o_ref[0] = _dot(s.astype(BF16), w_ref[0].astype(BF16)) + b_ref[0]


def _ada_call(cond, w_ada, b_ada):
    n_blk = 6
    return pl.pallas_call(
        _ada_kernel,
        grid=(DEPTH, n_blk),
        in_specs=[
            pl.BlockSpec((ADA_ROWS, D_MODEL), lambda l, j: (0, 0)),
            pl.BlockSpec((1, D_MODEL, D_MODEL), lambda l, j: (l, 0, j)),
            pl.BlockSpec((1, 1, D_MODEL), lambda l, j: (l, 0, j)),
        ],
        out_specs=pl.BlockSpec((1, ADA_ROWS, D_MODEL), lambda l, j: (l, 0, j)),
        out_shape=jax.ShapeDtypeStruct((DEPTH, ADA_ROWS, 6 * D_MODEL), F32),
        compiler_params=_cparams(2),
        name="ada",
    )(cond, w_ada, b_ada.reshape(DEPTH, 1, 6 * D_MODEL))


def _mod_row(latent, layer, seq_len, tm):
    base = layer * ADA_ROWS
    if latent:
        return lambda i: (base + 1 + i // (seq_len // tm)) * 6
    return lambda i: base * 6


def _mla_keys(ckv, kr_wide, wuk_ref, wuv_ref):
    cb = ckv.astype(BF16)
    kn = _dot(cb, wuk_ref[...])
    keys = [(kn[:, h * MLA_HEAD_PAD:(h + 1) * MLA_HEAD_PAD] + kr_wide).astype(BF16)
            for h in range(MLA_HEADS)]
    vv_t = _dot(cb, wuv_ref[...]).T.astype(BF16)
    return keys, vv_t


def _prep_kernel(*refs, latent, n_new, q_scale_diff, q_scale_mla):
    if latent:
        (x_ref, sh_ref, sc_ref, win_ref, gq_ref, gkv_ref, wuq_ref, wuk_ref, wuv_ref,
         cd_ref, shd_ref, sld_ref, cm_ref, shm_ref, slm_ref,
         ck_ref, cv_ref, cckv_ref, ckr_ref,
         qd_ref, kd_ref, vd_ref, u_ref, qc_ref, kc_ref, vm_ref) = refs
    else:
        (x_ref, sh_ref, sc_ref, win_ref, gq_ref, gkv_ref, wuq_ref, wuk_ref, wuv_ref,
         qd_ref, kd_ref, vd_ref, u_ref, qc_ref, kc_ref, vm_ref,
         k32_ref, v32_ref, ckv32_ref, kr32_ref) = refs

    def put_keys(where, dk_b, dv_t, keys, vv_t):
        lead, rows = where
        kd_ref[lead, rows, :] = dk_b
        vd_ref[lead, :, rows] = dv_t
        for hh in range(MLA_HEADS):
            kc_ref[lead, hh, rows, :] = keys[hh]
        vm_ref[lead, :, rows] = vv_t

    def new_tokens(sub):
        rows = slice(sub * TOKEN_SUB, (sub + 1) * TOKEN_SUB)
        h = x_ref[rows, :] * (1.0 + sc_ref[0]) + sh_ref[0]
        z = _dot(h.astype(BF16), win_ref[...])
        dq, dk, dv = z[:, 0:256], z[:, 256:512], z[:, 512:768]
        u = z[:, 768:1024]
        q_lat, kv_lat, kr = z[:, 1024:1280], z[:, 1280:1408], z[:, 1408:1536]
        ckv = _rms_norm(kv_lat, gkv_ref[...])
        qn = _rms_norm(q_lat, gq_ref[...])
        qc = _dot(qn.astype(BF16), wuq_ref[...])
        if latent:
            cd, shd, sld = cd_ref[rows, :], shd_ref[rows, :], sld_ref[rows, :]
            cm, shm, slm = cm_ref[rows, :], shm_ref[rows, :], slm_ref[rows, :]
            dq = _rope(dq, cd, shd, sld)
            dk = _rope(dk, cd, shd, sld)
            qc = _rope(qc, cm, shm, slm)
            kr = _rope(kr, cm, shm, slm)
        else:
            k32_ref[rows, :] = dk
            v32_ref[rows, :] = dv
            ckv32_ref[rows, :] = ckv
            kr32_ref[rows, :] = kr
        qd_ref[rows, :] = (dq * q_scale_diff).astype(BF16)
        u_ref[0, rows, :] = u[:, :LANES]
        u_ref[1, rows, :] = u[:, LANES:]
        qcs = (qc * q_scale_mla).astype(BF16)
        for hh in range(MLA_HEADS):
            qc_ref[hh, rows, :] = qcs[:, hh * MLA_HEAD_PAD:(hh + 1) * MLA_HEAD_PAD]
        keys, vv_t = _mla_keys(ckv, kr, wuk_ref, wuv_ref)
        where = (0, rows) if latent else (sub, slice(None))
        put_keys(where, dk.astype(BF16), dv.T.astype(BF16), keys, vv_t)

    n_sub = x_ref.shape[0] // TOKEN_SUB
    if latent:
        i = pl.program_id(1)

        @pl.when(i < n_new)
        def _():
            for sub in range(n_sub):
                new_tokens(sub)

        @pl.when(i == n_new)
        def _():
            past = ck_ref.shape[2]
            keys, vv_t = _mla_keys(cckv_ref[0, 0], ckr_ref[0, 0], wuk_ref, wuv_ref)
            put_keys((0, slice(0, past)), ck_ref[0, 0].astype(BF16),
                     cv_ref[0, 0].T.astype(BF16), keys, vv_t)
            n_pad = kd_ref.shape[1] - past
            put_keys((0, slice(past, kd_ref.shape[1])),
                     jnp.zeros((n_pad, DIFF_W), BF16), jnp.zeros((DIFF_W, n_pad), BF16),
                     [jnp.zeros((n_pad, MLA_HEAD_PAD), BF16)] * MLA_HEADS,
                     jnp.zeros((MLA_W, n_pad), BF16))
    else:
        for sub in range(n_sub):
            new_tokens(sub)


def _prep_call(latent, layer, x, seq_len, mods3, wts, tables=None, caches=None):
    n_tok = x.shape[0]
    tm = TOKEN_BLOCK
    n_req = n_tok // seq_len
    n_new = seq_len // tm if latent else 1
    weights = [wts["w_in"], wts["gq"], wts["gkv"], wts["w_uq"], wts["w_uk"], wts["w_uv"]]
    if latent:
        grid = (n_req, n_new + 1)
        key_rows = (n_new + 1) * tm
        blk = lambda b, i: b * n_new + jnp.minimum(i, n_new - 1)
        kblk = lambda b, i: (b, i)
        key_block = (1, tm)
    else:
        assert tm % seq_len == 0 and TOKEN_SUB == seq_len
        grid = (n_tok // tm,)
        key_rows = seq_len
        blk = lambda i: i
        kblk = lambda i: (i, 0)
        key_block = (tm // seq_len, seq_len)
    row = _mod_row(latent, layer, seq_len, tm)
    tok = lambda w: pl.BlockSpec((tm, w), lambda *g: (blk(*g), 0))
    heads = lambda n, w: pl.BlockSpec((n, tm, w), lambda *g: (0, blk(*g), 0))
    keys2 = lambda w: pl.BlockSpec(key_block + (w,), lambda *g: kblk(*g) + (0,))
    keys3 = lambda n, w: pl.BlockSpec((key_block[0], n, key_block[1], w),
                                      lambda *g: (kblk(*g)[0], 0, kblk(*g)[1], 0))
    vals_t = lambda w: pl.BlockSpec((key_block[0], w, key_block[1]),
                                    lambda *g: (kblk(*g)[0], 0, kblk(*g)[1]))
    in_specs = ([tok(D_MODEL),
                 pl.BlockSpec((1, 1, D_MODEL), lambda *g: (row(blk(*g)), 0, 0)),
                 pl.BlockSpec((1, 1, D_MODEL), lambda *g: (row(blk(*g)) + 1, 0, 0))]
                + [_layer_spec(w, layer) for w in weights])
    args = [x, mods3, mods3] + weights
    out_shape = [jax.ShapeDtypeStruct((n_tok, DIFF_W), BF16),
                 jax.ShapeDtypeStruct((n_req, key_rows, DIFF_W), BF16),
                 jax.ShapeDtypeStruct((n_req, DIFF_W, key_rows), BF16),
                 jax.ShapeDtypeStruct((S5_W // LANES, n_tok, LANES), F32),
                 jax.ShapeDtypeStruct((MLA_HEADS, n_tok, MLA_HEAD_PAD), BF16),
                 jax.ShapeDtypeStruct((n_req, MLA_HEADS, key_rows, MLA_HEAD_PAD), BF16),
                 jax.ShapeDtypeStruct((n_req, MLA_W, key_rows), BF16)]
    out_specs = [tok(DIFF_W), keys2(DIFF_W), vals_t(DIFF_W), heads(S5_W // LANES, LANES),
                 heads(MLA_HEADS, MLA_HEAD_PAD), keys3(MLA_HEADS, MLA_HEAD_PAD), vals_t(MLA_W)]
    if latent:
        past = caches[0].shape[2]
        assert past < tm
        in_specs += [pl.BlockSpec((tm, LANES), lambda b, i: (jnp.minimum(i, n_new - 1), 0))] * 6
        in_specs += [pl.BlockSpec((1, 1, past, w), lambda b, i: (b, layer, 0, 0))
                     for w in (DIFF_W, DIFF_W, LANES, LANES)]
        args += list(tables) + list(caches)
    else:
        out_shape += [jax.ShapeDtypeStruct((n_tok, DIFF_W), F32),
                      jax.ShapeDtypeStruct((n_tok, DIFF_W), F32),
                      jax.ShapeDtypeStruct((n_tok, MLA_KV_RANK), F32),
                      jax.ShapeDtypeStruct((n_tok, LANES), F32)]
        out_specs += [tok(DIFF_W), tok(DIFF_W), tok(MLA_KV_RANK), tok(LANES)]
    body = functools.partial(_prep_kernel, latent=latent, n_new=n_new,
                             q_scale_diff=DIFF_HEAD_DIM ** -0.5 * LOG2E,
                             q_scale_mla=(MLA_NOPE + MLA_ROPE) ** -0.5 * LOG2E)
    return pl.pallas_call(
        body, grid=grid, in_specs=in_specs, out_specs=out_specs, out_shape=out_shape,
        compiler_params=_cparams(len(grid)),
        name="prep_lat" if latent else "prep_ctx",
    )(*args)


def _softmax_maps(n_maps, scores_fn, values_fn, emit_fn, s_refs):
    n_keys = s_refs[0].shape[0]
    units = n_keys // LANES
    n_parts = min(KEY_PARTS, units)
    cuts = [LANES * (units * p // n_parts) for p in range(n_parts + 1)]
    key_parts = list(zip(cuts[:-1], cuts[1:]))

    def stage_a(i):
        m = None
        for lo, hi in key_parts:
            s = scores_fn(i, lo, hi)
            s_refs[i % 2][lo:hi, :] = s
            pm = jnp.max(s, axis=0, keepdims=True)
            m = pm if m is None else jnp.maximum(m, pm)
        return m

    def stage_b(i, m):
        v = values_fn(i)
        pv, denom = None, None
        for lo, hi in key_parts:
            e = jnp.exp2(s_refs[i % 2][lo:hi, :] - m)
            ps = jnp.sum(e, axis=0, keepdims=True)
            pp = _dot(v[:, lo:hi], e.astype(BF16))
            pv, denom = (pp, ps) if pv is None else (pv + pp, denom + ps)
        emit_fn(i, pv, denom)

    m_prev = stage_a(0)
    for i in range(1, n_maps):
        m_cur = stage_a(i)
        stage_b(i - 1, m_prev)
        m_prev = m_cur
    stage_b(n_maps - 1, m_prev)


def _diff_lambda(lq1_ref, lk1_ref, lq2_ref, lk2_ref, lam_init):
    s1 = jnp.sum(lq1_ref[...] * lk1_ref[...], axis=-1, keepdims=True)
    s2 = jnp.sum(lq2_ref[...] * lk2_ref[...], axis=-1, keepdims=True)
    return jnp.exp(s1) - jnp.exp(s2) + lam_init


def _diff_attn_kernel(q_ref, k_ref, v_ref, lq1_ref, lk1_ref, lq2_ref, lk2_ref, g_ref, o_ref,
                      acc_ref, s0_ref, s1_ref, *, lam_init):
    q = q_ref[...]
    lam = _diff_lambda(lq1_ref, lk1_ref, lq2_ref, lk2_ref, lam_init)
    lane = lax.broadcasted_iota(jnp.int32, (1, DIFF_W), 1)

    def scores(hc, lo_key, hi_key):
        lo = hc * DIFF_HEAD_DIM
        qm = q * jnp.where((lane >= lo) & (lane < lo + DIFF_HEAD_DIM), 1.0, 0.0).astype(BF16)
        return _dot_nt(k_ref[0, lo_key:hi_key, :], qm)

    def head_rows(hc):
        return slice((hc // 2) * DIFF_V_DIM, (hc // 2 + 1) * DIFF_V_DIM)

    def emit(hc, pv, denom):
        if hc % 2 == 0:
            acc_ref[head_rows(hc), :] = pv * (1.0 / denom)
        else:
            o = acc_ref[head_rows(hc), :] - pv * (lam / denom)
            ms = jnp.mean(o * o, axis=0, keepdims=True)
            acc_ref[head_rows(hc), :] = o * lax.rsqrt(ms + RMS_EPS)

    _softmax_maps(2 * DIFF_HEADS, scores, lambda hc: v_ref[0, head_rows(hc), :], emit,
                  (s0_ref, s1_ref))
    o_ref[...] = (acc_ref[...].T * g_ref[...] * (1.0 - lam_init)).astype(BF16)


def _diff_attn_call(layer, seq_len, n_keys, qd, kd, vd, lam_params, g_tiled):
    n_tok = qd.shape[0]
    tq = min(Q_BLOCK, seq_len)
    nq = seq_len // tq
    lam_init = 0.8 - 0.6 * math.exp(-0.3 * layer)
    in_specs = ([pl.BlockSpec((tq, DIFF_W), lambda b, i: (b * nq + i, 0)),
                 pl.BlockSpec((1, n_keys, DIFF_W), lambda b, i: (b, 0, 0)),
                 pl.BlockSpec((1, DIFF_W, n_keys), lambda b, i: (b, 0, 0))]
                + [_layer_spec(w, layer) for w in lam_params] + [_layer_spec(g_tiled, layer)])
    return pl.pallas_call(
        functools.partial(_diff_attn_kernel, lam_init=lam_init),
        grid=(n_tok // seq_len, nq),
        in_specs=in_specs,
        out_specs=pl.BlockSpec((tq, DIFF_W), lambda b, i: (b * nq + i, 0)),
        out_shape=jax.ShapeDtypeStruct((n_tok, DIFF_W), BF16),
        scratch_shapes=[pltpu.VMEM((DIFF_W, tq), F32),
                        pltpu.VMEM((n_keys, tq), F32), pltpu.VMEM((n_keys, tq), F32)],
        compiler_params=_cparams(2),
        name="diff_attn",
    )(qd, kd, vd, *lam_params, g_tiled)


def _mla_attn_kernel(q_ref, k_ref, v_ref, o_ref, acc_ref, s0_ref, s1_ref):
    def head_rows(h):
        return slice(h * MLA_V, (h + 1) * MLA_V)

    def emit(h, pv, denom):
        acc_ref[head_rows(h), :] = pv * (1.0 / denom)

    _softmax_maps(MLA_HEADS, lambda h, lo, hi: _dot_nt(k_ref[0, h, lo:hi, :], q_ref[h]),
                  lambda h: v_ref[0, head_rows(h), :], emit, (s0_ref, s1_ref))
    o_ref[...] = acc_ref[...].T.astype(BF16)


def _mla_attn_call(seq_len, n_keys, qc, kc, vm):
    n_tok = qc.shape[1]
    tq = min(Q_BLOCK, seq_len)
    nq = seq_len // tq
    in_specs = [pl.BlockSpec((MLA_HEADS, tq, MLA_HEAD_PAD), lambda b, i: (0, b * nq + i, 0)),
                pl.BlockSpec((1, MLA_HEADS, n_keys, MLA_HEAD_PAD), lambda b, i: (b, 0, 0, 0)),
                pl.BlockSpec((1, MLA_W, n_keys), lambda b, i: (b, 0, 0))]
    return pl.pallas_call(
        _mla_attn_kernel,
        grid=(n_tok // seq_len, nq),
        in_specs=in_specs,
        out_specs=pl.BlockSpec((tq, MLA_W), lambda b, i: (b * nq + i, 0)),
        out_shape=jax.ShapeDtypeStruct((n_tok, MLA_W), BF16),
        scratch_shapes=[pltpu.VMEM((MLA_W, tq), F32),
                        pltpu.VMEM((n_keys, tq), F32), pltpu.VMEM((n_keys, tq), F32)],
        compiler_params=_cparams(2),
        name="mla_attn",
    )(qc, kc, vm)


S5_CHUNK_STEPS = 64
S5_EPILOGUE_ROWS = 256


def _s5_kernel(*refs, steps, segmented):
    if segmented:
        (u_ref, bre_ref, bim_ref, cre_ref, cim_ref, lre_ref, lim_ref, ldt_ref, d_ref, wglu_ref,
         h0_ref, y_ref, up_ref, bur0_ref, bui0_ref, bur1_ref, bui1_ref, yacc_ref,
         inr_ref, ini_ref) = refs
    else:
        (u_ref, bre_ref, bim_ref, cre_ref, cim_ref, lre_ref, lim_ref, ldt_ref, d_ref, wglu_ref,
         y_ref, fin_ref, up_ref, bur0_ref, bui0_ref, bur1_ref, bui1_ref, yacc_ref) = refs
    bu_refs = ((bur0_ref, bui0_ref), (bur1_ref, bui1_ref))
    tc = S5_CHUNK_STEPS
    rows_c = tc * SUBLANES
    n_chunks = steps // tc
    yacc_ref[...] = jnp.zeros_like(yacc_ref)

    def regroup(j, carry):
        dst = pl.ds(pl.multiple_of(j * SUBLANES, SUBLANES), SUBLANES)
        for half in range(S5_W // LANES):
            up_ref[dst, half * LANES:(half + 1) * LANES] = (
                u_ref[half, pl.ds(j, SUBLANES, stride=steps), :])
        return carry

    lax.fori_loop(0, steps, regroup, 0, unroll=8)

    for d in range(2):
        lam_re, lam_im = lre_ref[d:d + 1, :], lim_ref[d:d + 1, :]
        dt = jnp.exp(ldt_ref[d:d + 1, :])
        mag = jnp.exp(lam_re * dt)
        ang = lam_im * dt
        a_re, a_im = mag * jnp.cos(ang), mag * jnp.sin(ang)
        den = lam_re * lam_re + lam_im * lam_im
        n_re, n_im = a_re - 1.0, a_im
        f_re = (n_re * lam_re + n_im * lam_im) / den
        f_im = (n_im * lam_re - n_re * lam_im) / den
        bbar_re = (f_re * bre_ref[d] - f_im * bim_ref[d]).astype(BF16)
        bbar_im = (f_re * bim_ref[d] + f_im * bre_ref[d]).astype(BF16)
        c_re, c_im = cre_ref[d].astype(BF16), cim_ref[d].astype(BF16)
        ar8 = jnp.broadcast_to(a_re, (SUBLANES, S5_N))
        ai8 = jnp.broadcast_to(a_im, (SUBLANES, S5_N))

        def run_pass(init, store, d=d, bbar_re=bbar_re, bbar_im=bbar_im, c_re=c_re, c_im=c_im,
                     ar8=ar8, ai8=ai8):
            def chunk_rows(ci):
                c = jnp.minimum(ci, n_chunks - 1)
                c = c if d == 0 else n_chunks - 1 - c
                return pl.ds(pl.multiple_of(c * rows_c, rows_c), rows_c)

            def project_in(ci, slot):
                ub = up_ref[chunk_rows(ci), :].astype(BF16)
                bu_refs[slot][0][...] = _dot(ub, bbar_re)
                bu_refs[slot][1][...] = _dot(ub, bbar_im)

            def scan(slot, hc):
                bur_ref, bui_ref = bu_refs[slot]
                hr, hi = hc
                for jj in range(tc):
                    j = jj if d == 0 else tc - 1 - jj
                    r = slice(j * SUBLANES, (j + 1) * SUBLANES)
                    hr, hi = (ar8 * hr - ai8 * hi + bur_ref[r, :],
                              ar8 * hi + ai8 * hr + bui_ref[r, :])
                    if store:
                        bur_ref[r, :] = hr
                        bui_ref[r, :] = hi
                return hr, hi

            def project_out(ci, slot):
                bur_ref, bui_ref = bu_refs[slot]
                yc = (_dot(bur_ref[...].astype(BF16), c_re)
                      - _dot(bui_ref[...].astype(BF16), c_im))
                for half in range(S5_W // LANES):
                    yacc_ref[half, chunk_rows(ci), :] += yc[:, half * LANES:(half + 1) * LANES]

            def chunk_pair(cp, carry):
                for slot in range(2):
                    ci = 2 * cp + slot
                    project_in(ci + 1, 1 - slot)
                    carry = scan(slot, carry)
                    if store:
                        project_out(ci, slot)
                return carry

            project_in(0, 0)
            return lax.fori_loop(0, n_chunks // 2, chunk_pair, init)

        zeros = (jnp.zeros((SUBLANES, S5_N), F32), jnp.zeros((SUBLANES, S5_N), F32))
        if segmented:
            f_r, f_i = run_pass(zeros, False)
            p_re, p_im = a_re, a_im
            for _ in range(int(math.log2(steps))):
                p_re, p_im = p_re * p_re - p_im * p_im, 2.0 * p_re * p_im
            c_r, c_i = h0_ref[0, d, 0:1, :], h0_ref[0, d, 1:2, :]
            order = range(SUBLANES) if d == 0 else range(SUBLANES - 1, -1, -1)
            for s in order:
                inr_ref[s:s + 1, :] = c_r
                ini_ref[s:s + 1, :] = c_i
                c_r, c_i = (f_r[s:s + 1, :] + p_re * c_r - p_im * c_i,
                            f_i[s:s + 1, :] + p_re * c_i + p_im * c_r)
            run_pass((inr_ref[...], ini_ref[...]), True)
        else:
            f_r, f_i = run_pass(zeros, True)
            fin_ref[0, d, 0] = f_r
            fin_ref[0, d, 1] = f_i

    ep = S5_EPILOGUE_ROWS
    per_chain = steps // ep

    def epilogue(e, carry):
        chain, jc = e // per_chain, e % per_chain
        rows = pl.ds(pl.multiple_of(e * ep, ep), ep)
        src = pl.ds(jc * ep * SUBLANES + chain, ep, stride=SUBLANES)
        halves = range(S5_W // LANES)
        acc = jnp.concatenate([yacc_ref[half, src, :] for half in halves], axis=1)
        u = jnp.concatenate([u_ref[half, rows, :] for half in halves], axis=1)
        y = acc + u * d_ref[...]
        y = jax.nn.gelu(y, approximate=True)
        y_ref[rows, :] = y * jax.nn.sigmoid(_dot(y.astype(BF16), wglu_ref[...]))
        return carry

    lax.fori_loop(0, SUBLANES * per_chain, epilogue, 0)


def _s5_call(layer, u, steps, s5w, h0=None):
    n_tok = u.shape[1]
    n_rows = steps * SUBLANES
    nb = n_tok // n_rows
    segmented = h0 is not None
    assert steps % (2 * S5_CHUNK_STEPS) == 0 and steps % S5_EPILOGUE_ROWS == 0
    assert steps & (steps - 1) == 0
    consts = [s5w["b_re"], s5w["b_im"], s5w["c_re"], s5w["c_im"], s5w["lam_re"], s5w["lam_im"],
              s5w["log_dt"], s5w["d"], s5w["w_glu"]]
    in_specs = ([pl.BlockSpec((S5_W // LANES, n_rows, LANES), lambda b: (0, b, 0))]
                + [_layer_spec(c, layer, single_buffer=True) for c in consts])
    args = [u] + consts
    out_shape = [jax.ShapeDtypeStruct((n_tok, S5_W), F32)]
    out_specs = [pl.BlockSpec((n_rows, S5_W), lambda b: (b, 0))]
    scratch = ([pltpu.VMEM((n_rows, S5_W), F32)]
               + [pltpu.VMEM((S5_CHUNK_STEPS * SUBLANES, S5_N), F32)] * 4
               + [pltpu.VMEM((S5_W // LANES, n_rows, LANES), F32)])
    if segmented:
        in_specs.append(pl.BlockSpec((1, None, 2, 2, S5_N), lambda b: (b, layer, 0, 0, 0)))
        args.append(h0)
        scratch += [pltpu.VMEM((SUBLANES, S5_N), F32), pltpu.VMEM((SUBLANES, S5_N), F32)]
    else:
        out_shape.append(jax.ShapeDtypeStruct((nb, 2, 2, SUBLANES, S5_N), F32))
        out_specs.append(pl.BlockSpec((1, 2, 2, SUBLANES, S5_N), lambda b: (b, 0, 0, 0, 0)))
    outs = pl.pallas_call(
        functools.partial(_s5_kernel, steps=steps, segmented=segmented),
        grid=(nb,), in_specs=in_specs, out_specs=out_specs, out_shape=out_shape,
        scratch_shapes=scratch, compiler_params=_cparams(1),
        name="s5_lat" if segmented else "s5_ctx",
    )(*args)
    return (outs[0], None) if segmented else (outs[0], outs[1])


FFN_CHUNK = FFN_HIDDEN // 2


def _post_kernel(x_ref, a_ref, s_ref, m_ref, g1_ref, sh2_ref, sc2_ref, g2_ref,
                 wo_ref, wg_ref, wu_ref, wd_ref, l1g_ref, l1b_ref, l2g_ref, l2b_ref, o_ref):
    for sub in range(x_ref.shape[0] // TOKEN_SUB):
        rows = slice(sub * TOKEN_SUB, (sub + 1) * TOKEN_SUB)
        x = x_ref[rows, :]
        mix = (_dot(a_ref[rows, :], wo_ref[0:256, :])
               + _dot(s_ref[rows, :].astype(BF16), wo_ref[256:512, :])
               + _dot(m_ref[rows, :], wo_ref[512:1024, :]))
        x1 = _layer_norm(ALPHA * x + g1_ref[0] * mix, l1g_ref[...], l1b_ref[...])
        hb = (x1 * (1.0 + sc2_ref[0]) + sh2_ref[0]).astype(BF16)
        f = jnp.zeros_like(x)
        for c in range(FFN_HIDDEN // FFN_CHUNK):
            cols = slice(c * FFN_CHUNK, (c + 1) * FFN_CHUNK)
            gate = _dot(hb, wg_ref[:, cols])
            up = _dot(hb, wu_ref[:, cols])
            act = (gate * jax.nn.sigmoid(gate) * up).astype(BF16)
            f = f + _dot(act, wd_ref[cols, :])
        o_ref[rows, :] = _layer_norm(ALPHA * x1 + g2_ref[0] * f, l2g_ref[...], l2b_ref[...])


def _post_call(latent, layer, seq_len, x, attn_d, y_s5, attn_m, mods3, wts):
    n_tok = x.shape[0]
    tm = TOKEN_BLOCK
    row = _mod_row(latent, layer, seq_len, tm)
    tok = lambda w: pl.BlockSpec((tm, w), lambda i: (i, 0))
    mod = lambda k: pl.BlockSpec((1, 1, D_MODEL), lambda i: (row(i) + k, 0, 0))
    weights = [wts["w_out"], wts["w_gate"], wts["w_up"], wts["w_down"],
               wts["ln1_g"], wts["ln1_b"], wts["ln2_g"], wts["ln2_b"]]
    w_specs = [_layer_spec(w, layer, single_buffer=True) for w in weights]
    return pl.pallas_call(
        _post_kernel,
        grid=(n_tok // tm,),
        in_specs=[tok(D_MODEL), tok(DIFF_W), tok(S5_W), tok(MLA_W), mod(2), mod(3), mod(4), mod(5)]
                 + w_specs,
        out_specs=tok(D_MODEL),
        out_shape=jax.ShapeDtypeStruct((n_tok, D_MODEL), F32),
        compiler_params=_cparams(1),
        name="post",
    )(x, attn_d, y_s5, attn_m, mods3, mods3, mods3, mods3, *weights)


def _rope_tables(length):
    rows = length // GRID_W
    row = np.repeat(np.arange(rows, dtype=np.float64), GRID_W)
    col = np.tile(np.arange(GRID_W, dtype=np.float64), rows)
    n_freq = DIFF_HEAD_DIM // 4
    inv = (ROPE_BASE ** (-np.arange(n_freq, dtype=np.float32) / np.float32(n_freq))).astype(np.float32)
    ang = np.concatenate([(row[:, None] * inv).astype(np.float32),
                          (col[:, None] * inv).astype(np.float32)], -1).astype(np.float64)
    cos, sin = np.cos(ang).astype(np.float32), np.sin(ang).astype(np.float32)
    zero = np.zeros_like(sin)
    cos32 = np.concatenate([cos, cos], -1)
    hi32 = np.concatenate([zero, sin], -1)
    lo32 = np.concatenate([-sin, zero], -1)
    diff = tuple(np.tile(t, (1, LANES // 32)) for t in (cos32, hi32, lo32))
    ones64, zeros64 = np.ones((length, 64), np.float32), np.zeros((length, 64), np.float32)
    ones32, zeros32 = np.ones((length, 32), np.float32), np.zeros((length, 32), np.float32)
    mla = (np.concatenate([ones64, cos32, ones32], -1),
           np.concatenate([zeros64, hi32, zeros32], -1),
           np.concatenate([zeros64, lo32, zeros32], -1))
    return tuple(jnp.asarray(t) for t in diff + mla)


def _block_diag(blocks):
    *lead, g, r, c = blocks.shape
    eye = jnp.eye(g, dtype=blocks.dtype)
    out = blocks[..., :, :, None, :] * eye[:, None, :, None]
    return out.reshape(*lead, g * r, g * c)


def _stacked_weights(p):
    w_in = p["w_in"]
    krope_cols = jnp.pad(w_in[:, :, 1408:1440], ((0, 0), (0, 0), (64, 32)))
    w_in_ext = jnp.concatenate([w_in[:, :, :1408], krope_cols], axis=2).astype(BF16)
    w_uq = jnp.pad(p["mla_w_uq"].reshape(DEPTH, MLA_Q_RANK, MLA_HEADS, MLA_NOPE + MLA_ROPE),
                   ((0, 0), (0, 0), (0, 0), (0, MLA_HEAD_PAD - MLA_NOPE - MLA_ROPE)))
    w_uk = jnp.pad(p["mla_w_uk"].reshape(DEPTH, MLA_KV_RANK, MLA_HEADS, MLA_NOPE),
                   ((0, 0), (0, 0), (0, 0), (0, MLA_HEAD_PAD - MLA_NOPE)))
    prep = {
        "w_in": w_in_ext,
        "gq": p["mla_q_norm_g"].reshape(DEPTH, 1, MLA_Q_RANK),
        "gkv": p["mla_kv_norm_g"].reshape(DEPTH, 1, MLA_KV_RANK),
        "w_uq": w_uq.reshape(DEPTH, MLA_Q_RANK, MLA_HEADS * MLA_HEAD_PAD).astype(BF16),
        "w_uk": w_uk.reshape(DEPTH, MLA_KV_RANK, MLA_HEADS * MLA_HEAD_PAD).astype(BF16),
        "w_uv": p["mla_w_uv"].astype(BF16),
    }
    lam_params = [p[n].reshape(DEPTH, 1, DIFF_HEAD_DIM)
                  for n in ("diff_lq1", "diff_lk1", "diff_lq2", "diff_lk2")]
    g_tiled = jnp.tile(p["diff_norm_g"], (1, DIFF_HEADS)).reshape(DEPTH, 1, DIFF_W)
    s5w = {
        "b_re": _block_diag(jnp.swapaxes(p["s5_b_re"], -1, -2)),
        "b_im": _block_diag(jnp.swapaxes(p["s5_b_im"], -1, -2)),
        "c_re": _block_diag(jnp.swapaxes(p["s5_c_re"], -1, -2)),
        "c_im": _block_diag(jnp.swapaxes(p["s5_c_im"], -1, -2)),
        "lam_re": p["s5_lam_re"].reshape(DEPTH, 2, S5_N),
        "lam_im": p["s5_lam_im"].reshape(DEPTH, 2, S5_N),
        "log_dt": jnp.repeat(p["s5_log_dt"], S5_STATE, axis=-1),
        "d": p["s5_d"].reshape(DEPTH, 1, S5_W),
        "w_glu": p["s5_w_glu"].astype(BF16),
    }
    post = {
        "w_out": p["w_out"].astype(BF16),
        "w_gate": p["ffn_w_gate"].astype(BF16),
        "w_up": p["ffn_w_up"].astype(BF16),
        "w_down": p["ffn_w_down"].astype(BF16),
        "ln1_g": p["ln1_g"].reshape(DEPTH, 1, D_MODEL), "ln1_b": p["ln1_b"].reshape(DEPTH, 1, D_MODEL),
        "ln2_g": p["ln2_g"].reshape(DEPTH, 1, D_MODEL), "ln2_b": p["ln2_b"].reshape(DEPTH, 1, D_MODEL),
    }
    return prep, lam_params, g_tiled, s5w, post


def kernel(x_prompt, x_sample, c, cache_diff_k, cache_diff_v, cache_mla_ckv, cache_mla_krope, state_s5, c_ctx, w_ada, b_ada, w_in, w_out, diff_lq1, diff_lk1, diff_lq2, diff_lk2, diff_norm_g, s5_lam_re, s5_lam_im, s5_log_dt, s5_b_re, s5_b_im, s5_c_re, s5_c_im, s5_d, s5_w_glu, mla_q_norm_g, mla_w_uq, mla_kv_norm_g, mla_w_uk, mla_w_uv, ln1_g, ln1_b, ln2_g, ln2_b, ffn_w_gate, ffn_w_up, ffn_w_down):
    p = dict(w_in=w_in, w_out=w_out, diff_lq1=diff_lq1, diff_lk1=diff_lk1, diff_lq2=diff_lq2,
             diff_lk2=diff_lk2, diff_norm_g=diff_norm_g, s5_lam_re=s5_lam_re, s5_lam_im=s5_lam_im,
             s5_log_dt=s5_log_dt, s5_b_re=s5_b_re, s5_b_im=s5_b_im, s5_c_re=s5_c_re, s5_c_im=s5_c_im,
             s5_d=s5_d, s5_w_glu=s5_w_glu, mla_q_norm_g=mla_q_norm_g, mla_w_uq=mla_w_uq,
             mla_kv_norm_g=mla_kv_norm_g, mla_w_uk=mla_w_uk, mla_w_uv=mla_w_uv, ln1_g=ln1_g,
             ln1_b=ln1_b, ln2_g=ln2_g, ln2_b=ln2_b, ffn_w_gate=ffn_w_gate, ffn_w_up=ffn_w_up,
             ffn_w_down=ffn_w_down)
    bsz, seq, _ = x_prompt.shape
    dec_b, dec_seq, _ = x_sample.shape
    past = cache_diff_k.shape[2]
    assert bsz % SUBLANES == 0 and dec_seq % SUBLANES == 0

    cond = jnp.concatenate([c_ctx[None, :], c, jnp.zeros((ADA_ROWS - 1 - dec_b, D_MODEL), F32)], 0)
    mods3 = _ada_call(cond, w_ada, b_ada).reshape(DEPTH * ADA_ROWS * 6, 1, D_MODEL)

    tables = _rope_tables(dec_seq)
    caches = (cache_diff_k.reshape(dec_b, DEPTH, past, DIFF_W),
              cache_diff_v.reshape(dec_b, DEPTH, past, DIFF_W),
              cache_mla_ckv,
              jnp.pad(cache_mla_krope, ((0, 0), (0, 0), (0, 0), (MLA_NOPE, LANES - MLA_NOPE - MLA_ROPE))))
    h0_all = jnp.moveaxis(state_s5, -1, 3).reshape(dec_b, DEPTH, 2, 2, S5_N)

    y_ctx = x_prompt.reshape(bsz * seq, D_MODEL)
    y_lat = x_sample.reshape(dec_b * dec_seq, D_MODEL)
    new_k, new_v, new_ckv, new_kr, new_st = [], [], [], [], []
    prep_w, lam_params, g_tiled, s5w, post_w = _stacked_weights(p)
    for l in range(DEPTH):
        qd, kd, vd, u, qc, kc, vm, k32, v32, ckv32, kr32 = _prep_call(
            False, l, y_ctx, seq, mods3, prep_w)
        attn_d = _diff_attn_call(l, seq, seq, qd, kd, vd, lam_params, g_tiled)
        attn_m = _mla_attn_call(seq, seq, qc, kc, vm)
        y_s5, fin = _s5_call(l, u, seq, s5w)
        y_ctx = _post_call(False, l, seq, y_ctx, attn_d, y_s5, attn_m, mods3, post_w)
        new_k.append(k32.reshape(bsz, seq, DIFF_HEADS, 2 * DIFF_HEAD_DIM))
        new_v.append(v32.reshape(bsz, seq, DIFF_HEADS, DIFF_V_DIM))
        new_ckv.append(ckv32.reshape(bsz, seq, MLA_KV_RANK))
        new_kr.append(kr32.reshape(bsz, seq, LANES)[:, :, MLA_NOPE:MLA_NOPE + MLA_ROPE])
        st = jnp.transpose(fin, (0, 3, 1, 4, 2)).reshape(bsz, 2, S5_GROUPS, S5_STATE, 2)
        new_st.append(st)

        qd, kd, vd, u, qc, kc, vm = _prep_call(True, l, y_lat, dec_seq, mods3, prep_w, tables, caches)
        attn_d = _diff_attn_call(l, dec_seq, dec_seq + past, qd, kd, vd, lam_params, g_tiled)
        attn_m = _mla_attn_call(dec_seq, dec_seq + past, qc, kc, vm)
        y_s5, _ = _s5_call(l, u, dec_seq // SUBLANES, s5w, h0_all)
        y_lat = _post_call(True, l, dec_seq, y_lat, attn_d, y_s5, attn_m, mods3, post_w)

    return (y_ctx.reshape(bsz, seq, D_MODEL), y_lat.reshape(dec_b, dec_seq, D_MODEL),
            jnp.stack(new_k, 1), jnp.stack(new_v, 1), jnp.stack(new_ckv, 1),
            jnp.stack(new_kr, 1), jnp.stack(new_st, 1))
```

```python
import functools
import math

import jax
import jax.numpy as jnp
import numpy as np
from jax import lax
from jax.experimental import pallas as pl
from jax.experimental.pallas import tpu as pltpu

F32 = jnp.float32
BF16 = jnp.bfloat16

D_MODEL = 1024
DEPTH = 2
GRID_W = 64
DIFF_HEADS = 4
DIFF_HEAD_DIM = 32
DIFF_V_DIM = 64
DIFF_W = 256
S5_CH = 16
S5_W = 256
S5_GROUPS = 16
S5_STATE = 64
S5_N = S5_GROUPS * S5_STATE
S5_CH_LOG2 = S5_CH.bit_length() - 1
S5_STATE_LOG2 = S5_STATE.bit_length() - 1
assert 1 << S5_CH_LOG2 == S5_CH and 1 << S5_STATE_LOG2 == S5_STATE
MLA_HEADS = 8
MLA_NOPE = 64
MLA_ROPE = 32
MLA_V = 64
MLA_Q_RANK = 256
MLA_KV_RANK = 128
MLA_W = 512
MLA_HEAD_PAD = 128
FFN_HIDDEN = 2816
ALPHA = (2 * DEPTH) ** 0.25
LN_EPS = 1e-5
RMS_EPS = 1e-6
ROPE_BASE = 10000.0
LOG2E = 1.4426950408889634

LANES = 128
SUBLANES = 8
TOKEN_BLOCK = 512
TOKEN_SUB = 256
Q_BLOCK = 256
KEY_PARTS = 4
VMEM_LIMIT = 56 * 1024 * 1024
ADA_ROWS = 8


def _cparams(n_axes):
    return pltpu.CompilerParams(dimension_semantics=("arbitrary",) * n_axes,
                                vmem_limit_bytes=VMEM_LIMIT)


def _layer_spec(arr, layer, single_buffer=False):
    rest = arr.shape[1:]
    mode = dict(pipeline_mode=pl.Buffered(1)) if single_buffer else {}
    return pl.BlockSpec((None,) + rest, lambda *_: (layer,) + (0,) * len(rest), **mode)


def _dot(a, b):
    return jnp.dot(a, b, preferred_element_type=F32)


def _dot_nt(a, b):
    return lax.dot_general(a, b, (((1,), (1,)), ((), ())), preferred_element_type=F32)


def _layer_norm(x, g, b):
    mu = jnp.mean(x, axis=-1, keepdims=True)
    xc = x - mu
    var = jnp.mean(xc * xc, axis=-1, keepdims=True)
    return xc * lax.rsqrt(var + LN_EPS) * g + b


def _rms_norm(x, g):
    return x * lax.rsqrt(jnp.mean(x * x, axis=-1, keepdims=True) + RMS_EPS) * g


def _rope(x, cos, sin_hi, sin_lo):
    outs = []
    for j in range(x.shape[1] // LANES):
        xb = x[:, j * LANES:(j + 1) * LANES]
        outs.append(xb * cos + pltpu.roll(xb, 16, 1) * sin_hi + pltpu.roll(xb, LANES - 16, 1) * sin_lo)
    return outs[0] if len(outs) == 1 else jnp.concatenate(outs, axis=1)


def _ada_kernel(cond_ref, w_ref, b_ref, o_ref):
    c = cond_ref[...]
    s = c * jax.nn.sigmoid(c)
    o_ref[0] = _dot(s.astype(BF16), w_ref[0].astype(BF16)) + b_ref[0]


def _ada_call(cond, w_ada, b_ada):
    n_blk = 6
    return pl.pallas_call(
        _ada_kernel,
        grid=(DEPTH, n_blk),
        in_specs=[
            pl.BlockSpec((ADA_ROWS, D_MODEL), lambda l, j: (0, 0)),
            pl.BlockSpec((1, D_MODEL, D_MODEL), lambda l, j: (l, 0, j)),
            pl.BlockSpec((1, 1, D_MODEL), lambda l, j: (l, 0, j)),
        ],
        out_specs=pl.BlockSpec((1, ADA_ROWS, D_MODEL), lambda l, j: (l, 0, j)),
        out_shape=jax.ShapeDtypeStruct((DEPTH, ADA_ROWS, 6 * D_MODEL), F32),
        compiler_params=_cparams(2),
        name="ada",
    )(cond, w_ada, b_ada.reshape(DEPTH, 1, 6 * D_MODEL))


def _mod_row(latent, layer, seq_len, tm):
    base = layer * ADA_ROWS
    if latent:
        return lambda i: (base + 1 + i // (seq_len // tm)) * 6
    return lambda i: base * 6


def _mla_keys(ckv, kr_wide, wuk_ref, wuv_ref):
    cb = ckv.astype(BF16)
    kn = _dot(cb, wuk_ref[...])
    keys = [(kn[:, h * MLA_HEAD_PAD:(h + 1) * MLA_HEAD_PAD] + kr_wide).astype(BF16)
            for h in range(MLA_HEADS)]
    vv_t = _dot(cb, wuv_ref[...]).T.astype(BF16)
    return keys, vv_t


def _prep_kernel(*refs, latent, n_new, q_scale_diff, q_scale_mla):
    if latent:
        (x_ref, sh_ref, sc_ref, win_ref, gq_ref, gkv_ref, wuq_ref, wuk_ref, wuv_ref,
         cd_ref, shd_ref, sld_ref, cm_ref, shm_ref, slm_ref,
         ck_ref, cv_ref, cckv_ref, ckr_ref,
         qd_ref, kd_ref, vd_ref, u_ref, qc_ref, kc_ref, vm_ref) = refs
    else:
        (x_ref, sh_ref, sc_ref, win_ref, gq_ref, gkv_ref, wuq_ref, wuk_ref, wuv_ref,
         qd_ref, kd_ref, vd_ref, u_ref, qc_ref, kc_ref, vm_ref,
         k32_ref, v32_ref, ckv32_ref, kr32_ref) = refs

    def put_keys(where, dk_b, dv_t, keys, vv_t):
        lead, rows = where
        kd_ref[lead, rows, :] = dk_b
        vd_ref[lead, :, rows] = dv_t
        for hh in range(MLA_HEADS):
            kc_ref[lead, hh, rows, :] = keys[hh]
        vm_ref[lead, :, rows] = vv_t

    def new_tokens(sub):
        rows = slice(sub * TOKEN_SUB, (sub + 1) * TOKEN_SUB)
        h = x_ref[rows, :] * (1.0 + sc_ref[0]) + sh_ref[0]
        z = _dot(h.astype(BF16), win_ref[...])
        yield
        dq, dk, dv = z[:, 0:256], z[:, 256:512], z[:, 512:768]
        u = z[:, 768:1024]
        q_lat, kv_lat, kr = z[:, 1024:1280], z[:, 1280:1408], z[:, 1408:1536]
        ckv = _rms_norm(kv_lat, gkv_ref[...])
        qn = _rms_norm(q_lat, gq_ref[...])
        qc = _dot(qn.astype(BF16), wuq_ref[...])
        yield
        if latent:
            cd, shd, sld = cd_ref[rows, :], shd_ref[rows, :], sld_ref[rows, :]
            cm, shm, slm = cm_ref[rows, :], shm_ref[rows, :], slm_ref[rows, :]
            dq = _rope(dq, cd, shd, sld)
            dk = _rope(dk, cd, shd, sld)
            qc = _rope(qc, cm, shm, slm)
            kr = _rope(kr, cm, shm, slm)
        else:
            k32_ref[rows, :] = dk
            v32_ref[rows, :] = dv
            ckv32_ref[rows, :] = ckv
            kr32_ref[rows, :] = kr
        keys, vv_t = _mla_keys(ckv, kr, wuk_ref, wuv_ref)
        yield
        qd_ref[rows, :] = (dq * q_scale_diff).astype(BF16)
        u_ref[0, rows, :] = u[:, :LANES]
        u_ref[1, rows, :] = u[:, LANES:]
        qcs = (qc * q_scale_mla).astype(BF16)
        for hh in range(MLA_HEADS):
            qc_ref[hh, rows, :] = qcs[:, hh * MLA_HEAD_PAD:(hh + 1) * MLA_HEAD_PAD]
        where = (0, rows) if latent else (sub, slice(None))
        put_keys(where, dk.astype(BF16), dv.T.astype(BF16), keys, vv_t)

    def all_new_tokens():
        running = [new_tokens(sub) for sub in range(x_ref.shape[0] // TOKEN_SUB)]
        while running:
            running = [g for g in running if next(g, "done") != "done"]

    if latent:
        i = pl.program_id(1)
        pl.when(i < n_new)(all_new_tokens)

        @pl.when(i == n_new)
        def _():
            past = ck_ref.shape[2]
            keys, vv_t = _mla_keys(cckv_ref[0, 0], ckr_ref[0, 0], wuk_ref, wuv_ref)
            put_keys((0, slice(0, past)), ck_ref[0, 0].astype(BF16),
                     cv_ref[0, 0].T.astype(BF16), keys, vv_t)
            n_pad = kd_ref.shape[1] - past
            put_keys((0, slice(past, kd_ref.shape[1])),
                     jnp.zeros((n_pad, DIFF_W), BF16), jnp.zeros((DIFF_W, n_pad), BF16),
                     [jnp.zeros((n_pad, MLA_HEAD_PAD), BF16)] * MLA_HEADS,
                     jnp.zeros((MLA_W, n_pad), BF16))
    else:
        all_new_tokens()


def _prep_call(latent, layer, x, seq_len, mods3, wts, tables=None, caches=None):
    n_tok = x.shape[0]
    tm = TOKEN_BLOCK
    n_req = n_tok // seq_len
    n_new = seq_len // tm if latent else 1
    weights = [wts["w_in"], wts["gq"], wts["gkv"], wts["w_uq"], wts["w_uk"], wts["w_uv"]]
    if latent:
        grid = (n_req, n_new + 1)
        key_rows = (n_new + 1) * tm
        blk = lambda b, i: b * n_new + jnp.minimum(i, n_new - 1)
        kblk = lambda b, i: (b, i)
        key_block = (1, tm)
    else:
        assert tm % seq_len == 0 and TOKEN_SUB == seq_len
        grid = (n_tok // tm,)
        key_rows = seq_len
        blk = lambda i: i
        kblk = lambda i: (i, 0)
        key_block = (tm // seq_len, seq_len)
    row = _mod_row(latent, layer, seq_len, tm)
    tok = lambda w: pl.BlockSpec((tm, w), lambda *g: (blk(*g), 0))
    heads = lambda n, w: pl.BlockSpec((n, tm, w), lambda *g: (0, blk(*g), 0))
    keys2 = lambda w: pl.BlockSpec(key_block + (w,), lambda *g: kblk(*g) + (0,))
    keys3 = lambda n, w: pl.BlockSpec((key_block[0], n, key_block[1], w),
                                      lambda *g: (kblk(*g)[0], 0, kblk(*g)[1], 0))
    vals_t = lambda w: pl.BlockSpec((key_block[0], w, key_block[1]),
                                    lambda *g: (kblk(*g)[0], 0, kblk(*g)[1]))
    in_specs = ([tok(D_MODEL),
                 pl.BlockSpec((1, 1, D_MODEL), lambda *g: (row(blk(*g)), 0, 0)),
                 pl.BlockSpec((1, 1, D_MODEL), lambda *g: (row(blk(*g)) + 1, 0, 0))]
                + [_layer_spec(w, layer) for w in weights])
    args = [x, mods3, mods3] + weights
    out_shape = [jax.ShapeDtypeStruct((n_tok, DIFF_W), BF16),
                 jax.ShapeDtypeStruct((n_req, key_rows, DIFF_W), BF16),
                 jax.ShapeDtypeStruct((n_req, DIFF_W, key_rows), BF16),
                 jax.ShapeDtypeStruct((S5_W // LANES, n_tok, LANES), F32),
                 jax.ShapeDtypeStruct((MLA_HEADS, n_tok, MLA_HEAD_PAD), BF16),
                 jax.ShapeDtypeStruct((n_req, MLA_HEADS, key_rows, MLA_HEAD_PAD), BF16),
                 jax.ShapeDtypeStruct((n_req, MLA_W, key_rows), BF16)]
    out_specs = [tok(DIFF_W), keys2(DIFF_W), vals_t(DIFF_W), heads(S5_W // LANES, LANES),
                 heads(MLA_HEADS, MLA_HEAD_PAD), keys3(MLA_HEADS, MLA_HEAD_PAD), vals_t(MLA_W)]
    if latent:
        past = caches[0].shape[2]
        assert past < tm
        in_specs += [pl.BlockSpec((tm, LANES), lambda b, i: (jnp.minimum(i, n_new - 1), 0))] * 6
        in_specs += [pl.BlockSpec((1, 1, past, w), lambda b, i: (b, layer, 0, 0))
                     for w in (DIFF_W, DIFF_W, LANES, LANES)]
        args += list(tables) + list(caches)
    else:
        out_shape += [jax.ShapeDtypeStruct((n_tok, DIFF_W), F32),
                      jax.ShapeDtypeStruct((n_tok, DIFF_W), F32),
                      jax.ShapeDtypeStruct((n_tok, MLA_KV_RANK), F32),
                      jax.ShapeDtypeStruct((n_tok, LANES), F32)]
        out_specs += [tok(DIFF_W), tok(DIFF_W), tok(MLA_KV_RANK), tok(LANES)]
    body = functools.partial(_prep_kernel, latent=latent, n_new=n_new,
                             q_scale_diff=DIFF_HEAD_DIM ** -0.5 * LOG2E,
                             q_scale_mla=(MLA_NOPE + MLA_ROPE) ** -0.5 * LOG2E)
    return pl.pallas_call(
        body, grid=grid, in_specs=in_specs, out_specs=out_specs, out_shape=out_shape,
        compiler_params=_cparams(len(grid)),
        name="prep_lat" if latent else "prep_ctx",
    )(*args)


def _softmax_maps(n_maps, scores_fn, values_fn, emit_fn, s_refs):
    n_keys = s_refs[0].shape[0]
    units = n_keys // LANES
    n_parts = min(KEY_PARTS, units)
    cuts = [LANES * (units * p // n_parts) for p in range(n_parts + 1)]
    key_parts = list(zip(cuts[:-1], cuts[1:]))

    def stage_a(i):
        m = None
        for lo, hi in key_parts:
            s = scores_fn(i, lo, hi)
            s_refs[i % 2][lo:hi, :] = s
            pm = jnp.max(s, axis=0, keepdims=True)
            m = pm if m is None else jnp.maximum(m, pm)
        return m

    def stage_b(i, m):
        v = values_fn(i)
        pv, denom = None, None
        for lo, hi in key_parts:
            e = jnp.exp2(s_refs[i % 2][lo:hi, :] - m)
            ps = jnp.sum(e, axis=0, keepdims=True)
            pp = _dot(v[:, lo:hi], e.astype(BF16))
            pv, denom = (pp, ps) if pv is None else (pv + pp, denom + ps)
        emit_fn(i, pv, denom)

    m_prev = stage_a(0)
    for i in range(1, n_maps):
        m_cur = stage_a(i)
        stage_b(i - 1, m_prev)
        m_prev = m_cur
    stage_b(n_maps - 1, m_prev)


def _diff_lambda(lq1_ref, lk1_ref, lq2_ref, lk2_ref, lam_init):
    s1 = jnp.sum(lq1_ref[...] * lk1_ref[...], axis=-1, keepdims=True)
    s2 = jnp.sum(lq2_ref[...] * lk2_ref[...], axis=-1, keepdims=True)
    return jnp.exp(s1) - jnp.exp(s2) + lam_init


def _diff_attn_kernel(q_ref, k_ref, v_ref, lq1_ref, lk1_ref, lq2_ref, lk2_ref, g_ref, o_ref,
                      acc_ref, s0_ref, s1_ref, *, lam_init):
    q = q_ref[...]
    lam = _diff_lambda(lq1_ref, lk1_ref, lq2_ref, lk2_ref, lam_init)
    lane = lax.broadcasted_iota(jnp.int32, (1, DIFF_W), 1)

    def scores(hc, lo_key, hi_key):
        lo = hc * DIFF_HEAD_DIM
        qm = q * jnp.where((lane >= lo) & (lane < lo + DIFF_HEAD_DIM), 1.0, 0.0).astype(BF16)
        return _dot_nt(k_ref[0, lo_key:hi_key, :], qm)

    def head_rows(hc):
        return slice((hc // 2) * DIFF_V_DIM, (hc // 2 + 1) * DIFF_V_DIM)

    def emit(hc, pv, denom):
        if hc % 2 == 0:
            acc_ref[head_rows(hc), :] = pv * (1.0 / denom)
        else:
            o = acc_ref[head_rows(hc), :] - pv * (lam / denom)
            ms = jnp.mean(o * o, axis=0, keepdims=True)
            acc_ref[head_rows(hc), :] = o * lax.rsqrt(ms + RMS_EPS)

    _softmax_maps(2 * DIFF_HEADS, scores, lambda hc: v_ref[0, head_rows(hc), :], emit,
                  (s0_ref, s1_ref))
    o_ref[...] = (acc_ref[...].T * g_ref[...] * (1.0 - lam_init)).astype(BF16)


def _diff_attn_call(layer, seq_len, n_keys, qd, kd, vd, lam_params, g_tiled):
    n_tok = qd.shape[0]
    tq = min(Q_BLOCK, seq_len)
    nq = seq_len // tq
    lam_init = 0.8 - 0.6 * math.exp(-0.3 * layer)
    in_specs = ([pl.BlockSpec((tq, DIFF_W), lambda b, i: (b * nq + i, 0)),
                 pl.BlockSpec((1, n_keys, DIFF_W), lambda b, i: (b, 0, 0)),
                 pl.BlockSpec((1, DIFF_W, n_keys), lambda b, i: (b, 0, 0))]
                + [_layer_spec(w, layer) for w in lam_params] + [_layer_spec(g_tiled, layer)])
    return pl.pallas_call(
        functools.partial(_diff_attn_kernel, lam_init=lam_init),
        grid=(n_tok // seq_len, nq),
        in_specs=in_specs,
        out_specs=pl.BlockSpec((tq, DIFF_W), lambda b, i: (b * nq + i, 0)),
        out_shape=jax.ShapeDtypeStruct((n_tok, DIFF_W), BF16),
        scratch_shapes=[pltpu.VMEM((DIFF_W, tq), F32),
                        pltpu.VMEM((n_keys, tq), F32), pltpu.VMEM((n_keys, tq), F32)],
        compiler_params=_cparams(2),
        name="diff_attn",
    )(qd, kd, vd, *lam_params, g_tiled)


def _mla_attn_kernel(q_ref, k_ref, v_ref, o_ref, acc_ref, s0_ref, s1_ref):
    def head_rows(h):
        return slice(h * MLA_V, (h + 1) * MLA_V)

    def emit(h, pv, denom):
        acc_ref[head_rows(h), :] = pv * (1.0 / denom)

    _softmax_maps(MLA_HEADS, lambda h, lo, hi: _dot_nt(k_ref[0, h, lo:hi, :], q_ref[h]),
                  lambda h: v_ref[0, head_rows(h), :], emit, (s0_ref, s1_ref))
    o_ref[...] = acc_ref[...].T.astype(BF16)


def _mla_attn_call(seq_len, n_keys, qc, kc, vm):
    n_tok = qc.shape[1]
    tq = min(Q_BLOCK, seq_len)
    nq = seq_len // tq
    in_specs = [pl.BlockSpec((MLA_HEADS, tq, MLA_HEAD_PAD), lambda b, i: (0, b * nq + i, 0)),
                pl.BlockSpec((1, MLA_HEADS, n_keys, MLA_HEAD_PAD), lambda b, i: (b, 0, 0, 0)),
                pl.BlockSpec((1, MLA_W, n_keys), lambda b, i: (b, 0, 0))]
    return pl.pallas_call(
        _mla_attn_kernel,
        grid=(n_tok // seq_len, nq),
        in_specs=in_specs,
        out_specs=pl.BlockSpec((tq, MLA_W), lambda b, i: (b * nq + i, 0)),
        out_shape=jax.ShapeDtypeStruct((n_tok, MLA_W), BF16),
        scratch_shapes=[pltpu.VMEM((MLA_W, tq), F32),
                        pltpu.VMEM((n_keys, tq), F32), pltpu.VMEM((n_keys, tq), F32)],
        compiler_params=_cparams(2),
        name="mla_attn",
    )(qc, kc, vm)


S5_CHUNK_STEPS = 64
S5_EPILOGUE_ROWS = 256


def _s5_kernel(*refs, steps, segmented):
    if segmented:
        (u_ref, bre_ref, bim_ref, cre_ref, cim_ref, lre_ref, lim_ref, ldt_ref, d_ref, wglu_ref,
         h0_ref, y_ref, up_ref, bur0_ref, bui0_ref, bur1_ref, bui1_ref, yacc_ref,
         inr_ref, ini_ref) = refs
    else:
        (u_ref, bre_ref, bim_ref, cre_ref, cim_ref, lre_ref, lim_ref, ldt_ref, d_ref, wglu_ref,
         y_ref, fin_ref, up_ref, bur0_ref, bui0_ref, bur1_ref, bui1_ref, yacc_ref) = refs
    bu_refs = ((bur0_ref, bui0_ref), (bur1_ref, bui1_ref))
    tc = S5_CHUNK_STEPS
    rows_c = tc * SUBLANES
    n_chunks = steps // tc
    yacc_ref[...] = jnp.zeros_like(yacc_ref)
    group_mask = (jnp.right_shift(lax.broadcasted_iota(jnp.int32, (S5_W, S5_N), 0), S5_CH_LOG2)
                  == jnp.right_shift(lax.broadcasted_iota(jnp.int32, (S5_W, S5_N), 1), S5_STATE_LOG2))

    def regroup(j, carry):
        dst = pl.ds(pl.multiple_of(j * SUBLANES, SUBLANES), SUBLANES)
        for half in range(S5_W // LANES):
            up_ref[dst, half * LANES:(half + 1) * LANES] = (
                u_ref[half, pl.ds(j, SUBLANES, stride=steps), :])
        return carry

    lax.fori_loop(0, steps, regroup, 0, unroll=8)

    for d in range(2):
        lam_re, lam_im = lre_ref[d:d + 1, :], lim_ref[d:d + 1, :]
        dt = jnp.exp(ldt_ref[d:d + 1, :])
        mag = jnp.exp(lam_re * dt)
        ang = lam_im * dt
        a_re, a_im = mag * jnp.cos(ang), mag * jnp.sin(ang)
        den = lam_re * lam_re + lam_im * lam_im
        n_re, n_im = a_re - 1.0, a_im
        f_re = (n_re * lam_re + n_im * lam_im) / den
        f_im = (n_im * lam_re - n_re * lam_im) / den
        def block_diagonal(rows16):
            dense = jnp.concatenate([rows16] * S5_GROUPS, axis=0)
            return jnp.where(group_mask, dense, 0.0).astype(BF16)

        bbar_re = block_diagonal(f_re * bre_ref[d] - f_im * bim_ref[d])
        bbar_im = block_diagonal(f_re * bim_ref[d] + f_im * bre_ref[d])
        ct_re, ct_im = block_diagonal(cre_ref[d]), block_diagonal(cim_ref[d])
        ar8 = jnp.broadcast_to(a_re, (SUBLANES, S5_N))
        ai8 = jnp.broadcast_to(a_im, (SUBLANES, S5_N))

        def run_pass(init, store, d=d, bbar_re=bbar_re, bbar_im=bbar_im, ct_re=ct_re, ct_im=ct_im,
                     ar8=ar8, ai8=ai8):
            def chunk_rows(ci):
                c = jnp.minimum(ci, n_chunks - 1)
                c = c if d == 0 else n_chunks - 1 - c
                return pl.ds(pl.multiple_of(c * rows_c, rows_c), rows_c)

            def project_in(ci, slot):
                ub = up_ref[chunk_rows(ci), :].astype(BF16)
                bu_refs[slot][0][...] = _dot(ub, bbar_re)
                bu_refs[slot][1][...] = _dot(ub, bbar_im)

            def scan(slot, hc):
                bur_ref, bui_ref = bu_refs[slot]
                hr, hi = hc
                for jj in range(tc):
                    j = jj if d == 0 else tc - 1 - jj
                    r = slice(j * SUBLANES, (j + 1) * SUBLANES)
                    hr, hi = (ar8 * hr - ai8 * hi + bur_ref[r, :],
                              ar8 * hi + ai8 * hr + bui_ref[r, :])
                    if store:
                        bur_ref[r, :] = hr
                        bui_ref[r, :] = hi
                return hr, hi

            def project_out(ci, slot):
                bur_ref, bui_ref = bu_refs[slot]
                yc = (_dot_nt(bur_ref[...].astype(BF16), ct_re)
                      - _dot_nt(bui_ref[...].astype(BF16), ct_im))
                for half in range(S5_W // LANES):
                    yacc_ref[half, chunk_rows(ci), :] += yc[:, half * LANES:(half + 1) * LANES]

            def chunk_pair(cp, carry):
                for slot in range(2):
                    ci = 2 * cp + slot
                    project_in(ci + 1, 1 - slot)
                    carry = scan(slot, carry)
                    if store:
                        project_out(ci, slot)
                return carry

            project_in(0, 0)
            return lax.fori_loop(0, n_chunks // 2, chunk_pair, init)

        zeros = (jnp.zeros((SUBLANES, S5_N), F32), jnp.zeros((SUBLANES, S5_N), F32))
        if segmented:
            f_r, f_i = run_pass(zeros, False)
            p_re, p_im = a_re, a_im
            for _ in range(int(math.log2(steps))):
                p_re, p_im = p_re * p_re - p_im * p_im, 2.0 * p_re * p_im
            c_r, c_i = h0_ref[0, d, 0:1, :], h0_ref[0, d, 1:2, :]
            order = range(SUBLANES) if d == 0 else range(SUBLANES - 1, -1, -1)
            for s in order:
                inr_ref[s:s + 1, :] = c_r
                ini_ref[s:s + 1, :] = c_i
                c_r, c_i = (f_r[s:s + 1, :] + p_re * c_r - p_im * c_i,
                            f_i[s:s + 1, :] + p_re * c_i + p_im * c_r)
            run_pass((inr_ref[...], ini_ref[...]), True)
        else:
            f_r, f_i = run_pass(zeros, True)
            fin_ref[0, d, 0] = f_r
            fin_ref[0, d, 1] = f_i

    ep = S5_EPILOGUE_ROWS
    per_chain = steps // ep

    def epilogue(e, carry):
        chain, jc = e // per_chain, e % per_chain
        rows = pl.ds(pl.multiple_of(e * ep, ep), ep)
        src = pl.ds(jc * ep * SUBLANES + chain, ep, stride=SUBLANES)
        halves = range(S5_W // LANES)
        acc = jnp.concatenate([yacc_ref[half, src, :] for half in halves], axis=1)
        u = jnp.concatenate([u_ref[half, rows, :] for half in halves], axis=1)
        y = acc + u * d_ref[...]
        y = jax.nn.gelu(y, approximate=True)
        y_ref[rows, :] = y * jax.nn.sigmoid(_dot(y.astype(BF16), wglu_ref[...]))
        return carry

    lax.fori_loop(0, SUBLANES * per_chain, epilogue, 0)


def _s5_call(layer, u, steps, s5w, h0=None):
    n_tok = u.shape[1]
    n_rows = steps * SUBLANES
    nb = n_tok // n_rows
    segmented = h0 is not None
    assert steps % (2 * S5_CHUNK_STEPS) == 0 and steps % S5_EPILOGUE_ROWS == 0
    assert steps & (steps - 1) == 0
    consts = [s5w["b_re"], s5w["b_im"], s5w["c_re"], s5w["c_im"], s5w["lam_re"], s5w["lam_im"],
              s5w["log_dt"], s5w["d"], s5w["w_glu"]]
    in_specs = ([pl.BlockSpec((S5_W // LANES, n_rows, LANES), lambda b: (0, b, 0))]
                + [_layer_spec(c, layer, single_buffer=True) for c in consts])
    args = [u] + consts
    out_shape = [jax.ShapeDtypeStruct((n_tok, S5_W), F32)]
    out_specs = [pl.BlockSpec((n_rows, S5_W), lambda b: (b, 0))]
    scratch = ([pltpu.VMEM((n_rows, S5_W), F32)]
               + [pltpu.VMEM((S5_CHUNK_STEPS * SUBLANES, S5_N), F32)] * 4
               + [pltpu.VMEM((S5_W // LANES, n_rows, LANES), F32)])
    if segmented:
        in_specs.append(pl.BlockSpec((1, None, 2, 2, S5_N), lambda b: (b, layer, 0, 0, 0)))
        args.append(h0)
        scratch += [pltpu.VMEM((SUBLANES, S5_N), F32), pltpu.VMEM((SUBLANES, S5_N), F32)]
    else:
        out_shape.append(jax.ShapeDtypeStruct((nb, 2, 2, SUBLANES, S5_N), F32))
        out_specs.append(pl.BlockSpec((1, 2, 2, SUBLANES, S5_N), lambda b: (b, 0, 0, 0, 0)))
    outs = pl.pallas_call(
        functools.partial(_s5_kernel, steps=steps, segmented=segmented),
        grid=(nb,), in_specs=in_specs, out_specs=out_specs, out_shape=out_shape,
        scratch_shapes=scratch, compiler_params=_cparams(1),
        name="s5_lat" if segmented else "s5_ctx",
    )(*args)
    return (outs[0], None) if segmented else (outs[0], outs[1])


FFN_CHUNK = FFN_HIDDEN // 2


def _post_kernel(x_ref, a_ref, s_ref, m_ref, g1_ref, sh2_ref, sc2_ref, g2_ref,
                 wo_ref, wg_ref, wu_ref, wd_ref, l1g_ref, l1b_ref, l2g_ref, l2b_ref, o_ref):
    subs = [slice(s * TOKEN_SUB, (s + 1) * TOKEN_SUB) for s in range(x_ref.shape[0] // TOKEN_SUB)]
    mix = [_dot(a_ref[rows, :], wo_ref[0:256, :])
           + _dot(s_ref[rows, :].astype(BF16), wo_ref[256:512, :])
           + _dot(m_ref[rows, :], wo_ref[512:1024, :]) for rows in subs]
    x1 = [_layer_norm(ALPHA * x_ref[rows, :] + g1_ref[0] * mx, l1g_ref[...], l1b_ref[...])
          for rows, mx in zip(subs, mix)]
    hb = [(v * (1.0 + sc2_ref[0]) + sh2_ref[0]).astype(BF16) for v in x1]
    f = [None] * len(subs)
    for c in range(FFN_HIDDEN // FFN_CHUNK):
        cols = slice(c * FFN_CHUNK, (c + 1) * FFN_CHUNK)
        for s in range(len(subs)):
            gate = _dot(hb[s], wg_ref[:, cols])
            up = _dot(hb[s], wu_ref[:, cols])
            act = (gate * jax.nn.sigmoid(gate) * up).astype(BF16)
            part = _dot(act, wd_ref[cols, :])
            f[s] = part if f[s] is None else f[s] + part
    for s, rows in enumerate(subs):
        o_ref[rows, :] = _layer_norm(ALPHA * x1[s] + g2_ref[0] * f[s], l2g_ref[...], l2b_ref[...])


def _post_call(latent, layer, seq_len, x, attn_d, y_s5, attn_m, mods3, wts):
    n_tok = x.shape[0]
    tm = TOKEN_BLOCK
    row = _mod_row(latent, layer, seq_len, tm)
    tok = lambda w: pl.BlockSpec((tm, w), lambda i: (i, 0))
    mod = lambda k: pl.BlockSpec((1, 1, D_MODEL), lambda i: (row(i) + k, 0, 0))
    weights = [wts["w_out"], wts["w_gate"], wts["w_up"], wts["w_down"],
               wts["ln1_g"], wts["ln1_b"], wts["ln2_g"], wts["ln2_b"]]
    w_specs = [_layer_spec(w, layer, single_buffer=True) for w in weights]
    return pl.pallas_call(
        _post_kernel,
        grid=(n_tok // tm,),
        in_specs=[tok(D_MODEL), tok(DIFF_W), tok(S5_W), tok(MLA_W), mod(2), mod(3), mod(4), mod(5)]
                 + w_specs,
        out_specs=tok(D_MODEL),
        out_shape=jax.ShapeDtypeStruct((n_tok, D_MODEL), F32),
        compiler_params=_cparams(1),
        name="post",
    )(x, attn_d, y_s5, attn_m, mods3, mods3, mods3, mods3, *weights)


def _rope_tables(length):
    rows = length // GRID_W
    row = np.repeat(np.arange(rows, dtype=np.float64), GRID_W)
    col = np.tile(np.arange(GRID_W, dtype=np.float64), rows)
    n_freq = DIFF_HEAD_DIM // 4
    inv = (ROPE_BASE ** (-np.arange(n_freq, dtype=np.float32) / np.float32(n_freq))).astype(np.float32)
    ang = np.concatenate([(row[:, None] * inv).astype(np.float32),
                          (col[:, None] * inv).astype(np.float32)], -1).astype(np.float64)
    cos, sin = np.cos(ang).astype(np.float32), np.sin(ang).astype(np.float32)
    zero = np.zeros_like(sin)
    cos32 = np.concatenate([cos, cos], -1)
    hi32 = np.concatenate([zero, sin], -1)
    lo32 = np.concatenate([-sin, zero], -1)
    diff = tuple(np.tile(t, (1, LANES // 32)) for t in (cos32, hi32, lo32))
    ones64, zeros64 = np.ones((length, 64), np.float32), np.zeros((length, 64), np.float32)
    ones32, zeros32 = np.ones((length, 32), np.float32), np.zeros((length, 32), np.float32)
    mla = (np.concatenate([ones64, cos32, ones32], -1),
           np.concatenate([zeros64, hi32, zeros32], -1),
           np.concatenate([zeros64, lo32, zeros32], -1))
    return tuple(jnp.asarray(t) for t in diff + mla)


def _stacked_weights(p):
    w_in = p["w_in"]
    krope_cols = jnp.pad(w_in[:, :, 1408:1440], ((0, 0), (0, 0), (64, 32)))
    w_in_ext = jnp.concatenate([w_in[:, :, :1408], krope_cols], axis=2).astype(BF16)
    w_uq = jnp.pad(p["mla_w_uq"].reshape(DEPTH, MLA_Q_RANK, MLA_HEADS, MLA_NOPE + MLA_ROPE),
                   ((0, 0), (0, 0), (0, 0), (0, MLA_HEAD_PAD - MLA_NOPE - MLA_ROPE)))
    w_uk = jnp.pad(p["mla_w_uk"].reshape(DEPTH, MLA_KV_RANK, MLA_HEADS, MLA_NOPE),
                   ((0, 0), (0, 0), (0, 0), (0, MLA_HEAD_PAD - MLA_NOPE)))
    prep = {
        "w_in": w_in_ext,
        "gq": p["mla_q_norm_g"].reshape(DEPTH, 1, MLA_Q_RANK),
        "gkv": p["mla_kv_norm_g"].reshape(DEPTH, 1, MLA_KV_RANK),
        "w_uq": w_uq.reshape(DEPTH, MLA_Q_RANK, MLA_HEADS * MLA_HEAD_PAD).astype(BF16),
        "w_uk": w_uk.reshape(DEPTH, MLA_KV_RANK, MLA_HEADS * MLA_HEAD_PAD).astype(BF16),
        "w_uv": p["mla_w_uv"].astype(BF16),
    }
    lam_params = [p[n].reshape(DEPTH, 1, DIFF_HEAD_DIM)
                  for n in ("diff_lq1", "diff_lk1", "diff_lq2", "diff_lk2")]
    g_tiled = jnp.tile(p["diff_norm_g"], (1, DIFF_HEADS)).reshape(DEPTH, 1, DIFF_W)
    s5w = {
        "b_re": jnp.transpose(p["s5_b_re"], (0, 1, 4, 2, 3)).reshape(DEPTH, 2, S5_CH, S5_N),
        "b_im": jnp.transpose(p["s5_b_im"], (0, 1, 4, 2, 3)).reshape(DEPTH, 2, S5_CH, S5_N),
        "c_re": jnp.transpose(p["s5_c_re"], (0, 1, 3, 2, 4)).reshape(DEPTH, 2, S5_CH, S5_N),
        "c_im": jnp.transpose(p["s5_c_im"], (0, 1, 3, 2, 4)).reshape(DEPTH, 2, S5_CH, S5_N),
        "lam_re": p["s5_lam_re"].reshape(DEPTH, 2, S5_N),
        "lam_im": p["s5_lam_im"].reshape(DEPTH, 2, S5_N),
        "log_dt": jnp.repeat(p["s5_log_dt"], S5_STATE, axis=-1),
        "d": p["s5_d"].reshape(DEPTH, 1, S5_W),
        "w_glu": p["s5_w_glu"].astype(BF16),
    }
    post = {
        "w_out": p["w_out"].astype(BF16),
        "w_gate": p["ffn_w_gate"].astype(BF16),
        "w_up": p["ffn_w_up"].astype(BF16),
        "w_down": p["ffn_w_down"].astype(BF16),
        "ln1_g": p["ln1_g"].reshape(DEPTH, 1, D_MODEL), "ln1_b": p["ln1_b"].reshape(DEPTH, 1, D_MODEL),
        "ln2_g": p["ln2_g"].reshape(DEPTH, 1, D_MODEL), "ln2_b": p["ln2_b"].reshape(DEPTH, 1, D_MODEL),
    }
    return prep, lam_params, g_tiled, s5w, post


def kernel(x_prompt, x_sample, c, cache_diff_k, cache_diff_v, cache_mla_ckv, cache_mla_krope, state_s5, c_ctx, w_ada, b_ada, w_in, w_out, diff_lq1, diff_lk1, diff_lq2, diff_lk2, diff_norm_g, s5_lam_re, s5_lam_im, s5_log_dt, s5_b_re, s5_b_im, s5_c_re, s5_c_im, s5_d, s5_w_glu, mla_q_norm_g, mla_w_uq, mla_kv_norm_g, mla_w_uk, mla_w_uv, ln1_g, ln1_b, ln2_g, ln2_b, ffn_w_gate, ffn_w_up, ffn_w_down):
    p = dict(w_in=w_in, w_out=w_out, diff_lq1=diff_lq1, diff_lk1=diff_lk1, diff_lq2=diff_lq2,
             diff_lk2=diff_lk2, diff_norm_g=diff_norm_g, s5_lam_re=s5_lam_re, s5_lam_im=s5_lam_im,
             s5_log_dt=s5_log_dt, s5_b_re=s5_b_re, s5_b_im=s5_b_im, s5_c_re=s5_c_re, s5_c_im=s5_c_im,
             s5_d=s5_d, s5_w_glu=s5_w_glu, mla_q_norm_g=mla_q_norm_g, mla_w_uq=mla_w_uq,
             mla_kv_norm_g=mla_kv_norm_g, mla_w_uk=mla_w_uk, mla_w_uv=mla_w_uv, ln1_g=ln1_g,
             ln1_b=ln1_b, ln2_g=ln2_g, ln2_b=ln2_b, ffn_w_gate=ffn_w_gate, ffn_w_up=ffn_w_up,
             ffn_w_down=ffn_w_down)
    bsz, seq, _ = x_prompt.shape
    dec_b, dec_seq, _ = x_sample.shape
    past = cache_diff_k.shape[2]
    assert bsz % SUBLANES == 0 and dec_seq % SUBLANES == 0

    cond = jnp.concatenate([c_ctx[None, :], c, jnp.zeros((ADA_ROWS - 1 - dec_b, D_MODEL), F32)], 0)
    mods3 = _ada_call(cond, w_ada, b_ada).reshape(DEPTH * ADA_ROWS * 6, 1, D_MODEL)

    tables = _rope_tables(dec_seq)
    caches = (cache_diff_k.reshape(dec_b, DEPTH, past, DIFF_W),
              cache_diff_v.reshape(dec_b, DEPTH, past, DIFF_W),
              cache_mla_ckv,
              jnp.pad(cache_mla_krope, ((0, 0), (0, 0), (0, 0), (MLA_NOPE, LANES - MLA_NOPE - MLA_ROPE))))
    h0_all = jnp.moveaxis(state_s5, -1, 3).reshape(dec_b, DEPTH, 2, 2, S5_N)

    y_ctx = x_prompt.reshape(bsz * seq, D_MODEL)
    y_lat = x_sample.reshape(dec_b * dec_seq, D_MODEL)
    new_k, new_v, new_ckv, new_kr, new_st = [], [], [], [], []
    prep_w, lam_params, g_tiled, s5w, post_w = _stacked_weights(p)
    for l in range(DEPTH):
        qd, kd, vd, u, qc, kc, vm, k32, v32, ckv32, kr32 = _prep_call(
            False, l, y_ctx, seq, mods3, prep_w)
        attn_d = _diff_attn_call(l, seq, seq, qd, kd, vd, lam_params, g_tiled)
        attn_m = _mla_attn_call(seq, seq, qc, kc, vm)
        y_s5, fin = _s5_call(l, u, seq, s5w)
        y_ctx = _post_call(False, l, seq, y_ctx, attn_d, y_s5, attn_m, mods3, post_w)
        new_k.append(k32.reshape(bsz, seq, DIFF_HEADS, 2 * DIFF_HEAD_DIM))
        new_v.append(v32.reshape(bsz, seq, DIFF_HEADS, DIFF_V_DIM))
        new_ckv.append(ckv32.reshape(bsz, seq, MLA_KV_RANK))
        new_kr.append(kr32.reshape(bsz, seq, LANES)[:, :, MLA_NOPE:MLA_NOPE + MLA_ROPE])
        st = jnp.transpose(fin, (0, 3, 1, 4, 2)).reshape(bsz, 2, S5_GROUPS, S5_STATE, 2)
        new_st.append(st)

        qd, kd, vd, u, qc, kc, vm = _prep_call(True, l, y_lat, dec_seq, mods3, prep_w, tables, caches)
        attn_d = _diff_attn_call(l, dec_seq, dec_seq + past, qd, kd, vd, lam_params, g_tiled)
        attn_m = _mla_attn_call(dec_seq, dec_seq + past, qc, kc, vm)
        y_s5, _ = _s5_call(l, u, dec_seq // SUBLANES, s5w, h0_all)
        y_lat = _post_call(True, l, dec_seq, y_lat, attn_d, y_s5, attn_m, mods3, post_w)

    return (y_ctx.reshape(bsz, seq, D_MODEL), y_lat.reshape(dec_b, dec_seq, D_MODEL),
            jnp.stack(new_k, 1), jnp.stack(new_v, 1), jnp.stack(new_ckv, 1),
            jnp.stack(new_kr, 1), jnp.stack(new_st, 1))
```

```python
import functools
import math

import jax
import jax.numpy as jnp
import numpy as np
from jax import lax
from jax.experimental import pallas as pl
from jax.experimental.pallas import tpu as pltpu

F32 = jnp.float32
BF16 = jnp.bfloat16

D_MODEL = 1024
DEPTH = 2
GRID_W = 64
DIFF_HEADS = 4
DIFF_HEAD_DIM = 32
DIFF_V_DIM = 64
DIFF_W = 256
S5_CH = 16
S5_W = 256
S5_GROUPS = 16
S5_STATE = 64
S5_N = S5_GROUPS * S5_STATE
S5_CH_LOG2 = S5_CH.bit_length() - 1
S5_STATE_LOG2 = S5_STATE.bit_length() - 1
assert 1 << S5_CH_LOG2 == S5_CH and 1 << S5_STATE_LOG2 == S5_STATE
MLA_HEADS = 8
MLA_NOPE = 64
MLA_ROPE = 32
MLA_V = 64
MLA_Q_RANK = 256
MLA_KV_RANK = 128
MLA_W = 512
MLA_HEAD_PAD = 128
FFN_HIDDEN = 2816
ALPHA = (2 * DEPTH) ** 0.25
LN_EPS = 1e-5
RMS_EPS = 1e-6
ROPE_BASE = 10000.0
LOG2E = 1.4426950408889634

LANES = 128
SUBLANES = 8
TOKEN_BLOCK = 512
TOKEN_SUB = 256
Q_BLOCK = 256
KEY_PARTS = 4
VMEM_LIMIT = 56 * 1024 * 1024
ADA_ROWS = 8


def _cparams(n_axes):
    return pltpu.CompilerParams(dimension_semantics=("arbitrary",) * n_axes,
                                vmem_limit_bytes=VMEM_LIMIT)


def _layer_spec(arr, layer, single_buffer=False):
    rest = arr.shape[1:]
    mode = dict(pipeline_mode=pl.Buffered(1)) if single_buffer else {}
    return pl.BlockSpec((None,) + rest, lambda *_: (layer,) + (0,) * len(rest), **mode)


def _dot(a, b):
    return jnp.dot(a, b, preferred_element_type=F32)


def _dot_nt(a, b):
    return lax.dot_general(a, b, (((1,), (1,)), ((), ())), preferred_element_type=F32)


def _layer_norm(x, g, b):
    mu = jnp.mean(x, axis=-1, keepdims=True)
    xc = x - mu
    var = jnp.mean(xc * xc, axis=-1, keepdims=True)
    return xc * lax.rsqrt(var + LN_EPS) * g + b


def _rms_norm(x, g):
    return x * lax.rsqrt(jnp.mean(x * x, axis=-1, keepdims=True) + RMS_EPS) * g


def _rope(x, cos, sin_hi, sin_lo):
    outs = []
    for j in range(x.shape[1] // LANES):
        xb = x[:, j * LANES:(j + 1) * LANES]
        outs.append(xb * cos + pltpu.roll(xb, 16, 1) * sin_hi + pltpu.roll(xb, LANES - 16, 1) * sin_lo)
    return outs[0] if len(outs) == 1 else jnp.concatenate(outs, axis=1)


def _ada_kernel(cond_ref, w_ref, b_ref, o_ref):
    c = cond_ref[...]
    s = c * jax.nn.sigmoid(c)
    o_ref[0] = _dot(s.astype(BF16), w_ref[0].astype(BF16)) + b_ref[0]


def _ada_call(cond, w_ada, b_ada):
    n_blk = 6
    return pl.pallas_call(
        _ada_kernel,
        grid=(DEPTH, n_blk),
        in_specs=[
            pl.BlockSpec((ADA_ROWS, D_MODEL), lambda l, j: (0, 0)),
            pl.BlockSpec((1, D_MODEL, D_MODEL), lambda l, j: (l, 0, j)),
            pl.BlockSpec((1, 1, D_MODEL), lambda l, j: (l, 0, j)),
        ],
        out_specs=pl.BlockSpec((1, ADA_ROWS, D_MODEL), lambda l, j: (l, 0, j)),
        out_shape=jax.ShapeDtypeStruct((DEPTH, ADA_ROWS, 6 * D_MODEL), F32),
        compiler_params=_cparams(2),
        name="ada",
    )(cond, w_ada, b_ada.reshape(DEPTH, 1, 6 * D_MODEL))


def _mod_row(latent, layer, seq_len, tm):
    base = layer * ADA_ROWS
    if latent:
        return lambda i: (base + 1 + i // (seq_len // tm)) * 6
    return lambda i: base * 6


def _mla_keys(ckv, kr_wide, wukv_ref):
    kv = _dot(ckv.astype(BF16), wukv_ref[...])
    keys = [(kv[:, h * MLA_HEAD_PAD:(h + 1) * MLA_HEAD_PAD] + kr_wide).astype(BF16)
            for h in range(MLA_HEADS)]
    vv_t = kv[:, MLA_HEADS * MLA_HEAD_PAD:].T.astype(BF16)
    return keys, vv_t


def _prep_kernel(*refs, latent, n_new, q_scale_diff, q_scale_mla):
    if latent:
        (x_ref, sh_ref, sc_ref, win_ref, gq_ref, gkv_ref, wuq_ref, wukv_ref,
         cd_ref, shd_ref, sld_ref, cm_ref, shm_ref, slm_ref,
         ck_ref, cv_ref, cckv_ref, ckr_ref,
         qd_ref, kd_ref, vd_ref, u_ref, qc_ref, kc_ref, vm_ref) = refs
    else:
        (x_ref, sh_ref, sc_ref, win_ref, gq_ref, gkv_ref, wuq_ref, wukv_ref,
         qd_ref, kd_ref, vd_ref, u_ref, qc_ref, kc_ref, vm_ref,
         k32_ref, v32_ref, ckv32_ref, kr32_ref) = refs

    def put_keys(where, dk_b, dv_t, keys, vv_t):
        lead, rows = where
        kd_ref[lead, rows, :] = dk_b
        vd_ref[lead, :, rows] = dv_t
        for hh in range(MLA_HEADS):
            kc_ref[lead, hh, rows, :] = keys[hh]
        vm_ref[lead, :, rows] = vv_t

    def new_tokens(sub):
        rows = slice(sub * TOKEN_SUB, (sub + 1) * TOKEN_SUB)
        h = x_ref[rows, :] * (1.0 + sc_ref[0]) + sh_ref[0]
        z = _dot(h.astype(BF16), win_ref[...])
        yield
        dq, dk, dv = z[:, 0:256], z[:, 256:512], z[:, 512:768]
        u = z[:, 768:1024]
        q_lat, kv_lat, kr = z[:, 1024:1280], z[:, 1280:1408], z[:, 1408:1536]
        ckv = _rms_norm(kv_lat, gkv_ref[...])
        qn = _rms_norm(q_lat, gq_ref[...])
        qc = _dot(qn.astype(BF16), wuq_ref[...])
        yield
        if latent:
            cd, shd, sld = cd_ref[rows, :], shd_ref[rows, :], sld_ref[rows, :]
            cm, shm, slm = cm_ref[rows, :], shm_ref[rows, :], slm_ref[rows, :]
            dq = _rope(dq, cd, shd, sld)
            dk = _rope(dk, cd, shd, sld)
            qc = _rope(qc, cm, shm, slm)
            kr = _rope(kr, cm, shm, slm)
        else:
            k32_ref[rows, :] = dk
            v32_ref[rows, :] = dv
            ckv32_ref[rows, :] = ckv
            kr32_ref[rows, :] = kr
        keys, vv_t = _mla_keys(ckv, kr, wukv_ref)
        yield
        qd_ref[rows, :] = (dq * q_scale_diff).astype(BF16)
        u_ref[0, rows, :] = u[:, :LANES]
        u_ref[1, rows, :] = u[:, LANES:]
        qcs = (qc * q_scale_mla).astype(BF16)
        for hh in range(MLA_HEADS):
            qc_ref[hh, rows, :] = qcs[:, hh * MLA_HEAD_PAD:(hh + 1) * MLA_HEAD_PAD]
        where = (0, rows) if latent else (sub, slice(None))
        put_keys(where, dk.astype(BF16), dv.T.astype(BF16), keys, vv_t)

    def all_new_tokens():
        running = [new_tokens(sub) for sub in range(x_ref.shape[0] // TOKEN_SUB)]
        while running:
            running = [g for g in running if next(g, "done") != "done"]

    if latent:
        i = pl.program_id(1)
        pl.when(i < n_new)(all_new_tokens)

        @pl.when(i == n_new)
        def _():
            past = ck_ref.shape[2]
            keys, vv_t = _mla_keys(cckv_ref[0, 0], ckr_ref[0, 0], wukv_ref)
            put_keys((0, slice(0, past)), ck_ref[0, 0].astype(BF16),
                     cv_ref[0, 0].T.astype(BF16), keys, vv_t)
            n_pad = kd_ref.shape[1] - past
            put_keys((0, slice(past, kd_ref.shape[1])),
                     jnp.zeros((n_pad, DIFF_W), BF16), jnp.zeros((DIFF_W, n_pad), BF16),
                     [jnp.zeros((n_pad, MLA_HEAD_PAD), BF16)] * MLA_HEADS,
                     jnp.zeros((MLA_W, n_pad), BF16))
    else:
        all_new_tokens()


def _prep_call(latent, layer, x, seq_len, mods3, wts, tables=None, caches=None):
    n_tok = x.shape[0]
    tm = TOKEN_BLOCK
    n_req = n_tok // seq_len
    n_new = seq_len // tm if latent else 1
    weights = [wts["w_in"], wts["gq"], wts["gkv"], wts["w_uq"], wts["w_ukv"]]
    if latent:
        grid = (n_req, n_new + 1)
        key_rows = (n_new + 1) * tm
        blk = lambda b, i: b * n_new + jnp.minimum(i, n_new - 1)
        kblk = lambda b, i: (b, i)
        key_block = (1, tm)
    else:
        assert tm % seq_len == 0 and TOKEN_SUB == seq_len
        grid = (n_tok // tm,)
        key_rows = seq_len
        blk = lambda i: i
        kblk = lambda i: (i, 0)
        key_block = (tm // seq_len, seq_len)
    row = _mod_row(latent, layer, seq_len, tm)
    tok = lambda w: pl.BlockSpec((tm, w), lambda *g: (blk(*g), 0))
    heads = lambda n, w: pl.BlockSpec((n, tm, w), lambda *g: (0, blk(*g), 0))
    keys2 = lambda w: pl.BlockSpec(key_block + (w,), lambda *g: kblk(*g) + (0,))
    keys3 = lambda n, w: pl.BlockSpec((key_block[0], n, key_block[1], w),
                                      lambda *g: (kblk(*g)[0], 0, kblk(*g)[1], 0))
    vals_t = lambda w: pl.BlockSpec((key_block[0], w, key_block[1]),
                                    lambda *g: (kblk(*g)[0], 0, kblk(*g)[1]))
    in_specs = ([tok(D_MODEL),
                 pl.BlockSpec((1, 1, D_MODEL), lambda *g: (row(blk(*g)), 0, 0)),
                 pl.BlockSpec((1, 1, D_MODEL), lambda *g: (row(blk(*g)) + 1, 0, 0))]
                + [_layer_spec(w, layer) for w in weights])
    args = [x, mods3, mods3] + weights
    out_shape = [jax.ShapeDtypeStruct((n_tok, DIFF_W), BF16),
                 jax.ShapeDtypeStruct((n_req, key_rows, DIFF_W), BF16),
                 jax.ShapeDtypeStruct((n_req, DIFF_W, key_rows), BF16),
                 jax.ShapeDtypeStruct((S5_W // LANES, n_tok, LANES), F32),
                 jax.ShapeDtypeStruct((MLA_HEADS, n_tok, MLA_HEAD_PAD), BF16),
                 jax.ShapeDtypeStruct((n_req, MLA_HEADS, key_rows, MLA_HEAD_PAD), BF16),
                 jax.ShapeDtypeStruct((n_req, MLA_W, key_rows), BF16)]
    out_specs = [tok(DIFF_W), keys2(DIFF_W), vals_t(DIFF_W), heads(S5_W // LANES, LANES),
                 heads(MLA_HEADS, MLA_HEAD_PAD), keys3(MLA_HEADS, MLA_HEAD_PAD), vals_t(MLA_W)]
    if latent:
        past = caches[0].shape[2]
        assert past < tm
        in_specs += [pl.BlockSpec((tm, LANES), lambda b, i: (jnp.minimum(i, n_new - 1), 0))] * 6
        in_specs += [pl.BlockSpec((1, 1, past, w), lambda b, i: (b, layer, 0, 0))
                     for w in (DIFF_W, DIFF_W, LANES, LANES)]
        args += list(tables) + list(caches)
    else:
        out_shape += [jax.ShapeDtypeStruct((n_tok, DIFF_W), F32),
                      jax.ShapeDtypeStruct((n_tok, DIFF_W), F32),
                      jax.ShapeDtypeStruct((n_tok, MLA_KV_RANK), F32),
                      jax.ShapeDtypeStruct((n_tok, LANES), F32)]
        out_specs += [tok(DIFF_W), tok(DIFF_W), tok(MLA_KV_RANK), tok(LANES)]
    body = functools.partial(_prep_kernel, latent=latent, n_new=n_new,
                             q_scale_diff=DIFF_HEAD_DIM ** -0.5 * LOG2E,
                             q_scale_mla=(MLA_NOPE + MLA_ROPE) ** -0.5 * LOG2E)
    return pl.pallas_call(
        body, grid=grid, in_specs=in_specs, out_specs=out_specs, out_shape=out_shape,
        compiler_params=_cparams(len(grid)),
        name="prep_lat" if latent else "prep_ctx",
    )(*args)


def _softmax_maps(n_maps, scores_fn, values_fn, emit_fn, s_refs):
    n_keys = s_refs[0].shape[0]
    units = n_keys // LANES
    n_parts = min(KEY_PARTS, units)
    cuts = [LANES * (units * p // n_parts) for p in range(n_parts + 1)]
    key_parts = list(zip(cuts[:-1], cuts[1:]))

    def stage_a(i):
        m = None
        for lo, hi in key_parts:
            s = scores_fn(i, lo, hi)
            s_refs[i % 2][lo:hi, :] = s
            pm = jnp.max(s, axis=0, keepdims=True)
            m = pm if m is None else jnp.maximum(m, pm)
        return m

    def stage_b(i, m):
        v = values_fn(i)
        pv, denom = None, None
        for lo, hi in key_parts:
            e = jnp.exp2(s_refs[i % 2][lo:hi, :] - m)
            ps = jnp.sum(e, axis=0, keepdims=True)
            pp = _dot(v[:, lo:hi], e.astype(BF16))
            pv, denom = (pp, ps) if pv is None else (pv + pp, denom + ps)
        emit_fn(i, pv, denom)

    m_prev = stage_a(0)
    for i in range(1, n_maps):
        m_cur = stage_a(i)
        stage_b(i - 1, m_prev)
        m_prev = m_cur
    stage_b(n_maps - 1, m_prev)


def _diff_lambda(lq1_ref, lk1_ref, lq2_ref, lk2_ref, lam_init):
    s1 = jnp.sum(lq1_ref[...] * lk1_ref[...], axis=-1, keepdims=True)
    s2 = jnp.sum(lq2_ref[...] * lk2_ref[...], axis=-1, keepdims=True)
    return jnp.exp(s1) - jnp.exp(s2) + lam_init


def _diff_attn_kernel(q_ref, k_ref, v_ref, lq1_ref, lk1_ref, lq2_ref, lk2_ref, g_ref, o_ref,
                      acc_ref, s0_ref, s1_ref, *, lam_init):
    q = q_ref[...]
    lam = _diff_lambda(lq1_ref, lk1_ref, lq2_ref, lk2_ref, lam_init)
    lane = lax.broadcasted_iota(jnp.int32, (1, DIFF_W), 1)

    def scores(hc, lo_key, hi_key):
        lo = hc * DIFF_HEAD_DIM
        qm = q * jnp.where((lane >= lo) & (lane < lo + DIFF_HEAD_DIM), 1.0, 0.0).astype(BF16)
        return _dot_nt(k_ref[0, lo_key:hi_key, :], qm)

    def head_rows(hc):
        return slice((hc // 2) * DIFF_V_DIM, (hc // 2 + 1) * DIFF_V_DIM)

    def emit(hc, pv, denom):
        if hc % 2 == 0:
            acc_ref[head_rows(hc), :] = pv * (1.0 / denom)
        else:
            o = acc_ref[head_rows(hc), :] - pv * (lam / denom)
            ms = jnp.mean(o * o, axis=0, keepdims=True)
            acc_ref[head_rows(hc), :] = o * lax.rsqrt(ms + RMS_EPS)

    _softmax_maps(2 * DIFF_HEADS, scores, lambda hc: v_ref[0, head_rows(hc), :], emit,
                  (s0_ref, s1_ref))
    o_ref[...] = (acc_ref[...].T * g_ref[...] * (1.0 - lam_init)).astype(BF16)


def _diff_attn_call(layer, seq_len, n_keys, qd, kd, vd, lam_params, g_tiled):
    n_tok = qd.shape[0]
    tq = min(Q_BLOCK, seq_len)
    nq = seq_len // tq
    lam_init = 0.8 - 0.6 * math.exp(-0.3 * layer)
    in_specs = ([pl.BlockSpec((tq, DIFF_W), lambda b, i: (b * nq + i, 0)),
                 pl.BlockSpec((1, n_keys, DIFF_W), lambda b, i: (b, 0, 0)),
                 pl.BlockSpec((1, DIFF_W, n_keys), lambda b, i: (b, 0, 0))]
                + [_layer_spec(w, layer) for w in lam_params] + [_layer_spec(g_tiled, layer)])
    return pl.pallas_call(
        functools.partial(_diff_attn_kernel, lam_init=lam_init),
        grid=(n_tok // seq_len, nq),
        in_specs=in_specs,
        out_specs=pl.BlockSpec((tq, DIFF_W), lambda b, i: (b * nq + i, 0)),
        out_shape=jax.ShapeDtypeStruct((n_tok, DIFF_W), BF16),
        scratch_shapes=[pltpu.VMEM((DIFF_W, tq), F32),
                        pltpu.VMEM((n_keys, tq), F32), pltpu.VMEM((n_keys, tq), F32)],
        compiler_params=_cparams(2),
        name="diff_attn",
    )(qd, kd, vd, *lam_params, g_tiled)


def _mla_attn_kernel(q_ref, k_ref, v_ref, o_ref, acc_ref, s0_ref, s1_ref):
    def head_rows(h):
        return slice(h * MLA_V, (h + 1) * MLA_V)

    def emit(h, pv, denom):
        acc_ref[head_rows(h), :] = pv * (1.0 / denom)

    _softmax_maps(MLA_HEADS, lambda h, lo, hi: _dot_nt(k_ref[0, h, lo:hi, :], q_ref[h]),
                  lambda h: v_ref[0, head_rows(h), :], emit, (s0_ref, s1_ref))
    o_ref[...] = acc_ref[...].T.astype(BF16)


def _mla_attn_call(seq_len, n_keys, qc, kc, vm):
    n_tok = qc.shape[1]
    tq = min(Q_BLOCK, seq_len)
    nq = seq_len // tq
    in_specs = [pl.BlockSpec((MLA_HEADS, tq, MLA_HEAD_PAD), lambda b, i: (0, b * nq + i, 0)),
                pl.BlockSpec((1, MLA_HEADS, n_keys, MLA_HEAD_PAD), lambda b, i: (b, 0, 0, 0)),
                pl.BlockSpec((1, MLA_W, n_keys), lambda b, i: (b, 0, 0))]
    return pl.pallas_call(
        _mla_attn_kernel,
        grid=(n_tok // seq_len, nq),
        in_specs=in_specs,
        out_specs=pl.BlockSpec((tq, MLA_W), lambda b, i: (b * nq + i, 0)),
        out_shape=jax.ShapeDtypeStruct((n_tok, MLA_W), BF16),
        scratch_shapes=[pltpu.VMEM((MLA_W, tq), F32),
                        pltpu.VMEM((n_keys, tq), F32), pltpu.VMEM((n_keys, tq), F32)],
        compiler_params=_cparams(2),
        name="mla_attn",
    )(qc, kc, vm)


S5_CHUNK_STEPS = 64
S5_EPILOGUE_ROWS = 256


def _s5_kernel(*refs, steps, segmented):
    if segmented:
        (u_ref, bre_ref, bim_ref, cre_ref, cim_ref, lre_ref, lim_ref, ldt_ref, d_ref, wglu_ref,
         h0_ref, y_ref, up_ref, bur0_ref, bui0_ref, bur1_ref, bui1_ref, yacc_ref,
         inr_ref, ini_ref) = refs
    else:
        (u_ref, bre_ref, bim_ref, cre_ref, cim_ref, lre_ref, lim_ref, ldt_ref, d_ref, wglu_ref,
         y_ref, fin_ref, up_ref, bur0_ref, bui0_ref, bur1_ref, bui1_ref, yacc_ref) = refs
    bu_refs = ((bur0_ref, bui0_ref), (bur1_ref, bui1_ref))
    tc = S5_CHUNK_STEPS
    rows_c = tc * SUBLANES
    n_chunks = steps // tc
    yacc_ref[...] = jnp.zeros_like(yacc_ref)
    group_mask = (jnp.right_shift(lax.broadcasted_iota(jnp.int32, (S5_W, S5_N), 0), S5_CH_LOG2)
                  == jnp.right_shift(lax.broadcasted_iota(jnp.int32, (S5_W, S5_N), 1), S5_STATE_LOG2))

    def regroup(j, carry):
        dst = pl.ds(pl.multiple_of(j * SUBLANES, SUBLANES), SUBLANES)
        for half in range(S5_W // LANES):
            up_ref[dst, half * LANES:(half + 1) * LANES] = (
                u_ref[half, pl.ds(j, SUBLANES, stride=steps), :])
        return carry

    lax.fori_loop(0, steps, regroup, 0, unroll=8)

    for d in range(2):
        lam_re, lam_im = lre_ref[d:d + 1, :], lim_ref[d:d + 1, :]
        dt = jnp.exp(ldt_ref[d:d + 1, :])
        mag = jnp.exp(lam_re * dt)
        ang = lam_im * dt
        a_re, a_im = mag * jnp.cos(ang), mag * jnp.sin(ang)
        den = lam_re * lam_re + lam_im * lam_im
        n_re, n_im = a_re - 1.0, a_im
        f_re = (n_re * lam_re + n_im * lam_im) / den
        f_im = (n_im * lam_re - n_re * lam_im) / den
        def block_diagonal(rows16):
            dense = jnp.concatenate([rows16] * S5_GROUPS, axis=0)
            return jnp.where(group_mask, dense, 0.0).astype(BF16)

        bbar = jnp.concatenate([block_diagonal(f_re * bre_ref[d] - f_im * bim_ref[d]),
                                block_diagonal(f_re * bim_ref[d] + f_im * bre_ref[d])], axis=1)
        c_t = jnp.concatenate([block_diagonal(cre_ref[d]), block_diagonal(-cim_ref[d])], axis=1)
        ar8 = jnp.broadcast_to(a_re, (SUBLANES, S5_N))
        ai8 = jnp.broadcast_to(a_im, (SUBLANES, S5_N))

        def run_pass(init, store, d=d, bbar=bbar, c_t=c_t, ar8=ar8, ai8=ai8):
            def chunk_rows(ci):
                c = jnp.minimum(ci, n_chunks - 1)
                c = c if d == 0 else n_chunks - 1 - c
                return pl.ds(pl.multiple_of(c * rows_c, rows_c), rows_c)

            def project_in(ci, slot):
                bu = _dot(up_ref[chunk_rows(ci), :].astype(BF16), bbar)
                bu_refs[slot][0][...] = bu[:, :S5_N]
                bu_refs[slot][1][...] = bu[:, S5_N:]

            def scan(slot, hc):
                bur_ref, bui_ref = bu_refs[slot]
                hr, hi = hc
                for jj in range(tc):
                    j = jj if d == 0 else tc - 1 - jj
                    r = slice(j * SUBLANES, (j + 1) * SUBLANES)
                    hr, hi = (ar8 * hr - ai8 * hi + bur_ref[r, :],
                              ar8 * hi + ai8 * hr + bui_ref[r, :])
                    if store:
                        bur_ref[r, :] = hr
                        bui_ref[r, :] = hi
                return hr, hi

            def project_out(ci, slot):
                bur_ref, bui_ref = bu_refs[slot]
                h = jnp.concatenate([bur_ref[...].astype(BF16), bui_ref[...].astype(BF16)], axis=1)
                yc = _dot_nt(h, c_t)
                for half in range(S5_W // LANES):
                    yacc_ref[half, chunk_rows(ci), :] += yc[:, half * LANES:(half + 1) * LANES]

            def chunk_pair(cp, carry):
                for slot in range(2):
                    ci = 2 * cp + slot
                    project_in(ci + 1, 1 - slot)
                    carry = scan(slot, carry)
                    if store:
                        project_out(ci, slot)
                return carry

            project_in(0, 0)
            return lax.fori_loop(0, n_chunks // 2, chunk_pair, init)

        zeros = (jnp.zeros((SUBLANES, S5_N), F32), jnp.zeros((SUBLANES, S5_N), F32))
        if segmented:
            f_r, f_i = run_pass(zeros, False)
            p_re, p_im = a_re, a_im
            for _ in range(int(math.log2(steps))):
                p_re, p_im = p_re * p_re - p_im * p_im, 2.0 * p_re * p_im
            c_r, c_i = h0_ref[0, d, 0:1, :], h0_ref[0, d, 1:2, :]
            order = range(SUBLANES) if d == 0 else range(SUBLANES - 1, -1, -1)
            for s in order:
                inr_ref[s:s + 1, :] = c_r
                ini_ref[s:s + 1, :] = c_i
                c_r, c_i = (f_r[s:s + 1, :] + p_re * c_r - p_im * c_i,
                            f_i[s:s + 1, :] + p_re * c_i + p_im * c_r)
            run_pass((inr_ref[...], ini_ref[...]), True)
        else:
            f_r, f_i = run_pass(zeros, True)
            fin_ref[0, d, 0] = f_r
            fin_ref[0, d, 1] = f_i

    ep = S5_EPILOGUE_ROWS
    per_chain = steps // ep

    def epilogue(e, carry):
        chain, jc = e // per_chain, e % per_chain
        rows = pl.ds(pl.multiple_of(e * ep, ep), ep)
        src = pl.ds(jc * ep * SUBLANES + chain, ep, stride=SUBLANES)
        halves = range(S5_W // LANES)
        acc = jnp.concatenate([yacc_ref[half, src, :] for half in halves], axis=1)
        u = jnp.concatenate([u_ref[half, rows, :] for half in halves], axis=1)
        y = acc + u * d_ref[...]
        y = jax.nn.gelu(y, approximate=True)
        y_ref[rows, :] = y * jax.nn.sigmoid(_dot(y.astype(BF16), wglu_ref[...]))
        return carry

    lax.fori_loop(0, SUBLANES * per_chain, epilogue, 0)


def _s5_call(layer, u, steps, s5w, h0=None):
    n_tok = u.shape[1]
    n_rows = steps * SUBLANES
    nb = n_tok // n_rows
    segmented = h0 is not None
    assert steps % (2 * S5_CHUNK_STEPS) == 0 and steps % S5_EPILOGUE_ROWS == 0
    assert steps & (steps - 1) == 0
    consts = [s5w["b_re"], s5w["b_im"], s5w["c_re"], s5w["c_im"], s5w["lam_re"], s5w["lam_im"],
              s5w["log_dt"], s5w["d"], s5w["w_glu"]]
    in_specs = ([pl.BlockSpec((S5_W // LANES, n_rows, LANES), lambda b: (0, b, 0))]
                + [_layer_spec(c, layer, single_buffer=True) for c in consts])
    args = [u] + consts
    out_shape = [jax.ShapeDtypeStruct((n_tok, S5_W), F32)]
    out_specs = [pl.BlockSpec((n_rows, S5_W), lambda b: (b, 0))]
    scratch = ([pltpu.VMEM((n_rows, S5_W), F32)]
               + [pltpu.VMEM((S5_CHUNK_STEPS * SUBLANES, S5_N), F32)] * 4
               + [pltpu.VMEM((S5_W // LANES, n_rows, LANES), F32)])
    if segmented:
        in_specs.append(pl.BlockSpec((1, None, 2, 2, S5_N), lambda b: (b, layer, 0, 0, 0)))
        args.append(h0)
        scratch += [pltpu.VMEM((SUBLANES, S5_N), F32), pltpu.VMEM((SUBLANES, S5_N), F32)]
    else:
        out_shape.append(jax.ShapeDtypeStruct((nb, 2, 2, SUBLANES, S5_N), F32))
        out_specs.append(pl.BlockSpec((1, 2, 2, SUBLANES, S5_N), lambda b: (b, 0, 0, 0, 0)))
    outs = pl.pallas_call(
        functools.partial(_s5_kernel, steps=steps, segmented=segmented),
        grid=(nb,), in_specs=in_specs, out_specs=out_specs, out_shape=out_shape,
        scratch_shapes=scratch, compiler_params=_cparams(1),
        name="s5_lat" if segmented else "s5_ctx",
    )(*args)
    return (outs[0], None) if segmented else (outs[0], outs[1])


def _post_kernel(x_ref, a_ref, s_ref, m_ref, g1_ref, sh2_ref, sc2_ref, g2_ref,
                 wo_ref, wgu_ref, wd_ref, l1g_ref, l1b_ref, l2g_ref, l2b_ref, o_ref):
    subs = [slice(s * TOKEN_SUB, (s + 1) * TOKEN_SUB) for s in range(x_ref.shape[0] // TOKEN_SUB)]
    mixed = [jnp.concatenate([a_ref[rows, :], s_ref[rows, :].astype(BF16), m_ref[rows, :]], axis=1)
             for rows in subs]
    mix = [_dot(mi, wo_ref[...]) for mi in mixed]
    x1 = [_layer_norm(ALPHA * x_ref[rows, :] + g1_ref[0] * mx, l1g_ref[...], l1b_ref[...])
          for rows, mx in zip(subs, mix)]
    hb = [(v * (1.0 + sc2_ref[0]) + sh2_ref[0]).astype(BF16) for v in x1]
    f = []
    for h in hb:
        gu = _dot(h, wgu_ref[...])
        gate, up = gu[:, :FFN_HIDDEN], gu[:, FFN_HIDDEN:]
        f.append(_dot((gate * jax.nn.sigmoid(gate) * up).astype(BF16), wd_ref[...]))
    for s, rows in enumerate(subs):
        o_ref[rows, :] = _layer_norm(ALPHA * x1[s] + g2_ref[0] * f[s], l2g_ref[...], l2b_ref[...])


def _post_call(latent, layer, seq_len, x, attn_d, y_s5, attn_m, mods3, wts):
    n_tok = x.shape[0]
    tm = TOKEN_BLOCK
    row = _mod_row(latent, layer, seq_len, tm)
    tok = lambda w: pl.BlockSpec((tm, w), lambda i: (i, 0))
    mod = lambda k: pl.BlockSpec((1, 1, D_MODEL), lambda i: (row(i) + k, 0, 0))
    weights = [wts["w_out"], wts["w_gate_up"], wts["w_down"],
               wts["ln1_g"], wts["ln1_b"], wts["ln2_g"], wts["ln2_b"]]
    w_specs = [_layer_spec(w, layer, single_buffer=True) for w in weights]
    return pl.pallas_call(
        _post_kernel,
        grid=(n_tok // tm,),
        in_specs=[tok(D_MODEL), tok(DIFF_W), tok(S5_W), tok(MLA_W), mod(2), mod(3), mod(4), mod(5)]
                 + w_specs,
        out_specs=tok(D_MODEL),
        out_shape=jax.ShapeDtypeStruct((n_tok, D_MODEL), F32),
        compiler_params=_cparams(1),
        name="post",
    )(x, attn_d, y_s5, attn_m, mods3, mods3, mods3, mods3, *weights)


def _rope_tables(length):
    rows = length // GRID_W
    row = np.repeat(np.arange(rows, dtype=np.float64), GRID_W)
    col = np.tile(np.arange(GRID_W, dtype=np.float64), rows)
    n_freq = DIFF_HEAD_DIM // 4
    inv = (ROPE_BASE ** (-np.arange(n_freq, dtype=np.float32) / np.float32(n_freq))).astype(np.float32)
    ang = np.concatenate([(row[:, None] * inv).astype(np.float32),
                          (col[:, None] * inv).astype(np.float32)], -1).astype(np.float64)
    cos, sin = np.cos(ang).astype(np.float32), np.sin(ang).astype(np.float32)
    zero = np.zeros_like(sin)
    cos32 = np.concatenate([cos, cos], -1)
    hi32 = np.concatenate([zero, sin], -1)
    lo32 = np.concatenate([-sin, zero], -1)
    diff = tuple(np.tile(t, (1, LANES // 32)) for t in (cos32, hi32, lo32))
    ones64, zeros64 = np.ones((length, 64), np.float32), np.zeros((length, 64), np.float32)
    ones32, zeros32 = np.ones((length, 32), np.float32), np.zeros((length, 32), np.float32)
    mla = (np.concatenate([ones64, cos32, ones32], -1),
           np.concatenate([zeros64, hi32, zeros32], -1),
           np.concatenate([zeros64, lo32, zeros32], -1))
    return tuple(jnp.asarray(t) for t in diff + mla)


def _stacked_weights(p):
    w_in = p["w_in"]
    krope_cols = jnp.pad(w_in[:, :, 1408:1440], ((0, 0), (0, 0), (64, 32)))
    w_in_ext = jnp.concatenate([w_in[:, :, :1408], krope_cols], axis=2).astype(BF16)
    w_uq = jnp.pad(p["mla_w_uq"].reshape(DEPTH, MLA_Q_RANK, MLA_HEADS, MLA_NOPE + MLA_ROPE),
                   ((0, 0), (0, 0), (0, 0), (0, MLA_HEAD_PAD - MLA_NOPE - MLA_ROPE)))
    w_uk = jnp.pad(p["mla_w_uk"].reshape(DEPTH, MLA_KV_RANK, MLA_HEADS, MLA_NOPE),
                   ((0, 0), (0, 0), (0, 0), (0, MLA_HEAD_PAD - MLA_NOPE)))
    prep = {
        "w_in": w_in_ext,
        "gq": p["mla_q_norm_g"].reshape(DEPTH, 1, MLA_Q_RANK),
        "gkv": p["mla_kv_norm_g"].reshape(DEPTH, 1, MLA_KV_RANK),
        "w_uq": w_uq.reshape(DEPTH, MLA_Q_RANK, MLA_HEADS * MLA_HEAD_PAD).astype(BF16),
        "w_ukv": jnp.concatenate(
            [w_uk.reshape(DEPTH, MLA_KV_RANK, MLA_HEADS * MLA_HEAD_PAD), p["mla_w_uv"]],
            axis=2).astype(BF16),
    }
    lam_params = [p[n].reshape(DEPTH, 1, DIFF_HEAD_DIM)
                  for n in ("diff_lq1", "diff_lk1", "diff_lq2", "diff_lk2")]
    g_tiled = jnp.tile(p["diff_norm_g"], (1, DIFF_HEADS)).reshape(DEPTH, 1, DIFF_W)
    s5w = {
        "b_re": jnp.transpose(p["s5_b_re"], (0, 1, 4, 2, 3)).reshape(DEPTH, 2, S5_CH, S5_N),
        "b_im": jnp.transpose(p["s5_b_im"], (0, 1, 4, 2, 3)).reshape(DEPTH, 2, S5_CH, S5_N),
        "c_re": jnp.transpose(p["s5_c_re"], (0, 1, 3, 2, 4)).reshape(DEPTH, 2, S5_CH, S5_N),
        "c_im": jnp.transpose(p["s5_c_im"], (0, 1, 3, 2, 4)).reshape(DEPTH, 2, S5_CH, S5_N),
        "lam_re": p["s5_lam_re"].reshape(DEPTH, 2, S5_N),
        "lam_im": p["s5_lam_im"].reshape(DEPTH, 2, S5_N),
        "log_dt": jnp.repeat(p["s5_log_dt"], S5_STATE, axis=-1),
        "d": p["s5_d"].reshape(DEPTH, 1, S5_W),
        "w_glu": p["s5_w_glu"].astype(BF16),
    }
    post = {
        "w_out": p["w_out"].astype(BF16),
        "w_gate_up": jnp.concatenate([p["ffn_w_gate"], p["ffn_w_up"]], axis=2).astype(BF16),
        "w_down": p["ffn_w_down"].astype(BF16),
        "ln1_g": p["ln1_g"].reshape(DEPTH, 1, D_MODEL), "ln1_b": p["ln1_b"].reshape(DEPTH, 1, D_MODEL),
        "ln2_g": p["ln2_g"].reshape(DEPTH, 1, D_MODEL), "ln2_b": p["ln2_b"].reshape(DEPTH, 1, D_MODEL),
    }
    return prep, lam_params, g_tiled, s5w, post


def kernel(x_prompt, x_sample, c, cache_diff_k, cache_diff_v, cache_mla_ckv, cache_mla_krope, state_s5, c_ctx, w_ada, b_ada, w_in, w_out, diff_lq1, diff_lk1, diff_lq2, diff_lk2, diff_norm_g, s5_lam_re, s5_lam_im, s5_log_dt, s5_b_re, s5_b_im, s5_c_re, s5_c_im, s5_d, s5_w_glu, mla_q_norm_g, mla_w_uq, mla_kv_norm_g, mla_w_uk, mla_w_uv, ln1_g, ln1_b, ln2_g, ln2_b, ffn_w_gate, ffn_w_up, ffn_w_down):
    p = dict(w_in=w_in, w_out=w_out, diff_lq1=diff_lq1, diff_lk1=diff_lk1, diff_lq2=diff_lq2,
             diff_lk2=diff_lk2, diff_norm_g=diff_norm_g, s5_lam_re=s5_lam_re, s5_lam_im=s5_lam_im,
             s5_log_dt=s5_log_dt, s5_b_re=s5_b_re, s5_b_im=s5_b_im, s5_c_re=s5_c_re, s5_c_im=s5_c_im,
             s5_d=s5_d, s5_w_glu=s5_w_glu, mla_q_norm_g=mla_q_norm_g, mla_w_uq=mla_w_uq,
             mla_kv_norm_g=mla_kv_norm_g, mla_w_uk=mla_w_uk, mla_w_uv=mla_w_uv, ln1_g=ln1_g,
             ln1_b=ln1_b, ln2_g=ln2_g, ln2_b=ln2_b, ffn_w_gate=ffn_w_gate, ffn_w_up=ffn_w_up,
             ffn_w_down=ffn_w_down)
    bsz, seq, _ = x_prompt.shape
    dec_b, dec_seq, _ = x_sample.shape
    past = cache_diff_k.shape[2]
    assert bsz % SUBLANES == 0 and dec_seq % SUBLANES == 0

    cond = jnp.concatenate([c_ctx[None, :], c, jnp.zeros((ADA_ROWS - 1 - dec_b, D_MODEL), F32)], 0)
    mods3 = _ada_call(cond, w_ada, b_ada).reshape(DEPTH * ADA_ROWS * 6, 1, D_MODEL)

    tables = _rope_tables(dec_seq)
    caches = (cache_diff_k.reshape(dec_b, DEPTH, past, DIFF_W),
              cache_diff_v.reshape(dec_b, DEPTH, past, DIFF_W),
              cache_mla_ckv,
              jnp.pad(cache_mla_krope, ((0, 0), (0, 0), (0, 0), (MLA_NOPE, LANES - MLA_NOPE - MLA_ROPE))))
    h0_all = jnp.moveaxis(state_s5, -1, 3).reshape(dec_b, DEPTH, 2, 2, S5_N)

    y_ctx = x_prompt.reshape(bsz * seq, D_MODEL)
    y_lat = x_sample.reshape(dec_b * dec_seq, D_MODEL)
    new_k, new_v, new_ckv, new_kr, new_st = [], [], [], [], []
    prep_w, lam_params, g_tiled, s5w, post_w = _stacked_weights(p)
    for l in range(DEPTH):
        qd, kd, vd, u, qc, kc, vm, k32, v32, ckv32, kr32 = _prep_call(
            False, l, y_ctx, seq, mods3, prep_w)
        attn_d = _diff_attn_call(l, seq, seq, qd, kd, vd, lam_params, g_tiled)
        attn_m = _mla_attn_call(seq, seq, qc, kc, vm)
        y_s5, fin = _s5_call(l, u, seq, s5w)
        y_ctx = _post_call(False, l, seq, y_ctx, attn_d, y_s5, attn_m, mods3, post_w)
        new_k.append(k32.reshape(bsz, seq, DIFF_HEADS, 2 * DIFF_HEAD_DIM))
        new_v.append(v32.reshape(bsz, seq, DIFF_HEADS, DIFF_V_DIM))
        new_ckv.append(ckv32.reshape(bsz, seq, MLA_KV_RANK))
        new_kr.append(kr32.reshape(bsz, seq, LANES)[:, :, MLA_NOPE:MLA_NOPE + MLA_ROPE])
        st = jnp.transpose(fin, (0, 3, 1, 4, 2)).reshape(bsz, 2, S5_GROUPS, S5_STATE, 2)
        new_st.append(st)

        qd, kd, vd, u, qc, kc, vm = _prep_call(True, l, y_lat, dec_seq, mods3, prep_w, tables, caches)
        attn_d = _diff_attn_call(l, dec_seq, dec_seq + past, qd, kd, vd, lam_params, g_tiled)
        attn_m = _mla_attn_call(dec_seq, dec_seq + past, qc, kc, vm)
        y_s5, _ = _s5_call(l, u, dec_seq // SUBLANES, s5w, h0_all)
        y_lat = _post_call(True, l, dec_seq, y_lat, attn_d, y_s5, attn_m, mods3, post_w)

    return (y_ctx.reshape(bsz, seq, D_MODEL), y_lat.reshape(dec_b, dec_seq, D_MODEL),
            jnp.stack(new_k, 1), jnp.stack(new_v, 1), jnp.stack(new_ckv, 1),
            jnp.stack(new_kr, 1), jnp.stack(new_st, 1))
```

```python
import functools
import math

import jax
import jax.numpy as jnp
import numpy as np
from jax import lax
from jax.experimental import pallas as pl
from jax.experimental.pallas import tpu as pltpu

F32 = jnp.float32
BF16 = jnp.bfloat16

D_MODEL = 1024
DEPTH = 2
GRID_W = 64
DIFF_HEADS = 4
DIFF_HEAD_DIM = 32
DIFF_V_DIM = 64
DIFF_W = 256
S5_CH = 16
S5_W = 256
S5_GROUPS = 16
S5_STATE = 64
S5_N = S5_GROUPS * S5_STATE
S5_CH_LOG2 = S5_CH.bit_length() - 1
S5_STATE_LOG2 = S5_STATE.bit_length() - 1
assert 1 << S5_CH_LOG2 == S5_CH and 1 << S5_STATE_LOG2 == S5_STATE
MLA_HEADS = 8
MLA_NOPE = 64
MLA_ROPE = 32
MLA_V = 64
MLA_Q_RANK = 256
MLA_KV_RANK = 128
MLA_W = 512
MLA_HEAD_PAD = 128
FFN_HIDDEN = 2816
ALPHA = (2 * DEPTH) ** 0.25
LN_EPS = 1e-5
RMS_EPS = 1e-6
ROPE_BASE = 10000.0
LOG2E = 1.4426950408889634

LANES = 128
SUBLANES = 8
TOKEN_BLOCK = 1024
TOKEN_SUB = 256
POST_BLOCK = 1024
Q_BLOCK = 256
KEY_PARTS = 4
VMEM_LIMIT = 56 * 1024 * 1024
ADA_ROWS = 8


def _cparams(n_axes):
    return pltpu.CompilerParams(dimension_semantics=("arbitrary",) * n_axes,
                                vmem_limit_bytes=VMEM_LIMIT)


def _layer_spec(arr, layer, single_buffer=False):
    rest = arr.shape[1:]
    mode = dict(pipeline_mode=pl.Buffered(1)) if single_buffer else {}
    return pl.BlockSpec((None,) + rest, lambda *_: (layer,) + (0,) * len(rest), **mode)


def _dot(a, b):
    return jnp.dot(a, b, preferred_element_type=F32)


def _dot_nt(a, b):
    return lax.dot_general(a, b, (((1,), (1,)), ((), ())), preferred_element_type=F32)


def _layer_norm(x, g, b):
    mu = jnp.mean(x, axis=-1, keepdims=True)
    xc = x - mu
    var = jnp.mean(xc * xc, axis=-1, keepdims=True)
    return xc * lax.rsqrt(var + LN_EPS) * g + b


def _rms_norm(x, g):
    return x * lax.rsqrt(jnp.mean(x * x, axis=-1, keepdims=True) + RMS_EPS) * g


def _rope(x, cos, sin_hi, sin_lo):
    outs = []
    for j in range(x.shape[1] // LANES):
        xb = x[:, j * LANES:(j + 1) * LANES]
        outs.append(xb * cos + pltpu.roll(xb, 16, 1) * sin_hi + pltpu.roll(xb, LANES - 16, 1) * sin_lo)
    return outs[0] if len(outs) == 1 else jnp.concatenate(outs, axis=1)


def _ada_kernel(cond_ref, w_ref, b_ref, o_ref):
    c = cond_ref[...]
    s = c * jax.nn.sigmoid(c)
    o_ref[0] = _dot(s.astype(BF16), w_ref[0].astype(BF16)) + b_ref[0]


def _ada_call(cond, w_ada, b_ada):
    n_blk = 6
    return pl.pallas_call(
        _ada_kernel,
        grid=(DEPTH, n_blk),
        in_specs=[
            pl.BlockSpec((ADA_ROWS, D_MODEL), lambda l, j: (0, 0)),
            pl.BlockSpec((1, D_MODEL, D_MODEL), lambda l, j: (l, 0, j)),
            pl.BlockSpec((1, 1, D_MODEL), lambda l, j: (l, 0, j)),
        ],
        out_specs=pl.BlockSpec((1, ADA_ROWS, D_MODEL), lambda l, j: (l, 0, j)),
        out_shape=jax.ShapeDtypeStruct((DEPTH, ADA_ROWS, 6 * D_MODEL), F32),
        compiler_params=_cparams(2),
        name="ada",
    )(cond, w_ada, b_ada.reshape(DEPTH, 1, 6 * D_MODEL))


def _mod_row(latent, layer, seq_len, tm):
    base = layer * ADA_ROWS
    if latent:
        return lambda i: (base + 1 + i // (seq_len // tm)) * 6
    return lambda i: base * 6


def _mla_keys(ckv, kr_wide, wukv_ref):
    kv = _dot(ckv.astype(BF16), wukv_ref[...])
    keys = [(kv[:, h * MLA_HEAD_PAD:(h + 1) * MLA_HEAD_PAD] + kr_wide).astype(BF16)
            for h in range(MLA_HEADS)]
    vv_t = kv[:, MLA_HEADS * MLA_HEAD_PAD:].T.astype(BF16)
    return keys, vv_t


def _prep_kernel(*refs, latent, n_new, q_scale_diff, q_scale_mla):
    if latent:
        (x_ref, sh_ref, sc_ref, win_ref, gq_ref, gkv_ref, wuq_ref, wukv_ref,
         cd_ref, shd_ref, sld_ref, cm_ref, shm_ref, slm_ref,
         ck_ref, cv_ref, cckv_ref, ckr_ref,
         qd_ref, kd_ref, vd_ref, u_ref, qc_ref, kc_ref, vm_ref) = refs
    else:
        (x_ref, sh_ref, sc_ref, win_ref, gq_ref, gkv_ref, wuq_ref, wukv_ref,
         qd_ref, kd_ref, vd_ref, u_ref, qc_ref, kc_ref, vm_ref,
         k32_ref, v32_ref, ckv32_ref, kr32_ref) = refs

    def put_keys(where, dk_b, dv_t, keys, vv_t):
        lead, rows = where
        kd_ref[lead, rows, :] = dk_b
        vd_ref[lead, :, rows] = dv_t
        for hh in range(MLA_HEADS):
            kc_ref[lead, hh, rows, :] = keys[hh]
        vm_ref[lead, :, rows] = vv_t

    def new_tokens(sub):
        rows = slice(sub * TOKEN_SUB, (sub + 1) * TOKEN_SUB)
        h = x_ref[rows, :] * (1.0 + sc_ref[0]) + sh_ref[0]
        z = _dot(h.astype(BF16), win_ref[...])
        yield
        dq, dk, dv = z[:, 0:256], z[:, 256:512], z[:, 512:768]
        u = z[:, 768:1024]
        q_lat, kv_lat, kr = z[:, 1024:1280], z[:, 1280:1408], z[:, 1408:1536]
        ckv = _rms_norm(kv_lat, gkv_ref[...])
        qn = _rms_norm(q_lat, gq_ref[...])
        qc = _dot(qn.astype(BF16), wuq_ref[...])
        yield
        if latent:
            cd, shd, sld = cd_ref[rows, :], shd_ref[rows, :], sld_ref[rows, :]
            cm, shm, slm = cm_ref[rows, :], shm_ref[rows, :], slm_ref[rows, :]
            dq = _rope(dq, cd, shd, sld)
            dk = _rope(dk, cd, shd, sld)
            qc = _rope(qc, cm, shm, slm)
            kr = _rope(kr, cm, shm, slm)
        else:
            k32_ref[rows, :] = dk
            v32_ref[rows, :] = dv
            ckv32_ref[rows, :] = ckv
            kr32_ref[rows, :] = kr
        keys, vv_t = _mla_keys(ckv, kr, wukv_ref)
        yield
        qd_ref[rows, :] = (dq * q_scale_diff).astype(BF16)
        u_ref[0, rows, :] = u[:, :LANES]
        u_ref[1, rows, :] = u[:, LANES:]
        qcs = (qc * q_scale_mla).astype(BF16)
        for hh in range(MLA_HEADS):
            qc_ref[hh, rows, :] = qcs[:, hh * MLA_HEAD_PAD:(hh + 1) * MLA_HEAD_PAD]
        where = (0, rows) if latent else (sub, slice(None))
        put_keys(where, dk.astype(BF16), dv.T.astype(BF16), keys, vv_t)

    def all_new_tokens():
        running = [new_tokens(sub) for sub in range(x_ref.shape[0] // TOKEN_SUB)]
        while running:
            running = [g for g in running if next(g, "done") != "done"]

    if latent:
        i = pl.program_id(1)
        pl.when(i < n_new)(all_new_tokens)

        @pl.when(i == n_new)
        def _():
            past = ck_ref.shape[2]
            keys, vv_t = _mla_keys(cckv_ref[0, 0], ckr_ref[0, 0], wukv_ref)
            put_keys((0, slice(0, past)), ck_ref[0, 0].astype(BF16),
                     cv_ref[0, 0].T.astype(BF16), keys, vv_t)
            n_pad = kd_ref.shape[1] - past
            put_keys((0, slice(past, kd_ref.shape[1])),
                     jnp.zeros((n_pad, DIFF_W), BF16), jnp.zeros((DIFF_W, n_pad), BF16),
                     [jnp.zeros((n_pad, MLA_HEAD_PAD), BF16)] * MLA_HEADS,
                     jnp.zeros((MLA_W, n_pad), BF16))
    else:
        all_new_tokens()


def _prep_call(latent, layer, x, seq_len, mods3, wts, tables=None, caches=None):
    n_tok = x.shape[0]
    tm = TOKEN_BLOCK
    n_req = n_tok // seq_len
    n_new = seq_len // tm if latent else 1
    weights = [wts["w_in"], wts["gq"], wts["gkv"], wts["w_uq"], wts["w_ukv"]]
    if latent:
        grid = (n_req, n_new + 1)
        key_rows = (n_new + 1) * tm
        blk = lambda b, i: b * n_new + jnp.minimum(i, n_new - 1)
        kblk = lambda b, i: (b, i)
        key_block = (1, tm)
    else:
        assert tm % seq_len == 0 and TOKEN_SUB == seq_len
        grid = (n_tok // tm,)
        key_rows = seq_len
        blk = lambda i: i
        kblk = lambda i: (i, 0)
        key_block = (tm // seq_len, seq_len)
    row = _mod_row(latent, layer, seq_len, tm)
    tok = lambda w: pl.BlockSpec((tm, w), lambda *g: (blk(*g), 0))
    heads = lambda n, w: pl.BlockSpec((n, tm, w), lambda *g: (0, blk(*g), 0))
    keys2 = lambda w: pl.BlockSpec(key_block + (w,), lambda *g: kblk(*g) + (0,))
    keys3 = lambda n, w: pl.BlockSpec((key_block[0], n, key_block[1], w),
                                      lambda *g: (kblk(*g)[0], 0, kblk(*g)[1], 0))
    vals_t = lambda w: pl.BlockSpec((key_block[0], w, key_block[1]),
                                    lambda *g: (kblk(*g)[0], 0, kblk(*g)[1]))
    in_specs = ([tok(D_MODEL),
                 pl.BlockSpec((1, 1, D_MODEL), lambda *g: (row(blk(*g)), 0, 0)),
                 pl.BlockSpec((1, 1, D_MODEL), lambda *g: (row(blk(*g)) + 1, 0, 0))]
                + [_layer_spec(w, layer) for w in weights])
    args = [x, mods3, mods3] + weights
    out_shape = [jax.ShapeDtypeStruct((n_tok, DIFF_W), BF16),
                 jax.ShapeDtypeStruct((n_req, key_rows, DIFF_W), BF16),
                 jax.ShapeDtypeStruct((n_req, DIFF_W, key_rows), BF16),
                 jax.ShapeDtypeStruct((S5_W // LANES, n_tok, LANES), F32),
                 jax.ShapeDtypeStruct((MLA_HEADS, n_tok, MLA_HEAD_PAD), BF16),
                 jax.ShapeDtypeStruct((n_req, MLA_HEADS, key_rows, MLA_HEAD_PAD), BF16),
                 jax.ShapeDtypeStruct((n_req, MLA_W, key_rows), BF16)]
    out_specs = [tok(DIFF_W), keys2(DIFF_W), vals_t(DIFF_W), heads(S5_W // LANES, LANES),
                 heads(MLA_HEADS, MLA_HEAD_PAD), keys3(MLA_HEADS, MLA_HEAD_PAD), vals_t(MLA_W)]
    if latent:
        past = caches[0].shape[2]
        assert past < tm
        in_specs += [pl.BlockSpec((tm, LANES), lambda b, i: (jnp.minimum(i, n_new - 1), 0))] * 6
        in_specs += [pl.BlockSpec((1, 1, past, w), lambda b, i: (b, layer, 0, 0))
                     for w in (DIFF_W, DIFF_W, LANES, LANES)]
        args += list(tables) + list(caches)
    else:
        out_shape += [jax.ShapeDtypeStruct((n_tok, DIFF_W), F32),
                      jax.ShapeDtypeStruct((n_tok, DIFF_W), F32),
                      jax.ShapeDtypeStruct((n_tok, MLA_KV_RANK), F32),
                      jax.ShapeDtypeStruct((n_tok, LANES), F32)]
        out_specs += [tok(DIFF_W), tok(DIFF_W), tok(MLA_KV_RANK), tok(LANES)]
    body = functools.partial(_prep_kernel, latent=latent, n_new=n_new,
                             q_scale_diff=DIFF_HEAD_DIM ** -0.5 * LOG2E,
                             q_scale_mla=(MLA_NOPE + MLA_ROPE) ** -0.5 * LOG2E)
    return pl.pallas_call(
        body, grid=grid, in_specs=in_specs, out_specs=out_specs, out_shape=out_shape,
        compiler_params=_cparams(len(grid)),
        name="prep_lat" if latent else "prep_ctx",
    )(*args)


def _softmax_maps(n_maps, scores_fn, values_fn, emit_fn, s_refs):
    n_keys = s_refs[0].shape[0]
    units = n_keys // LANES
    n_parts = min(KEY_PARTS, units)
    cuts = [LANES * (units * p // n_parts) for p in range(n_parts + 1)]
    key_parts = list(zip(cuts[:-1], cuts[1:]))

    def stage_a(i):
        m = None
        for lo, hi in key_parts:
            s = scores_fn(i, lo, hi)
            s_refs[i % 2][lo:hi, :] = s
            pm = jnp.max(s, axis=0, keepdims=True)
            m = pm if m is None else jnp.maximum(m, pm)
        return m

    def stage_b(i, m):
        v = values_fn(i)
        pv, denom = None, None
        for lo, hi in key_parts:
            e = jnp.exp2(s_refs[i % 2][lo:hi, :] - m)
            ps = jnp.sum(e, axis=0, keepdims=True)
            pp = _dot(v[:, lo:hi], e.astype(BF16))
            pv, denom = (pp, ps) if pv is None else (pv + pp, denom + ps)
        emit_fn(i, pv, denom)

    m_prev = stage_a(0)
    for i in range(1, n_maps):
        m_cur = stage_a(i)
        stage_b(i - 1, m_prev)
        m_prev = m_cur
    stage_b(n_maps - 1, m_prev)


def _diff_lambda(lq1_ref, lk1_ref, lq2_ref, lk2_ref, lam_init):
    s1 = jnp.sum(lq1_ref[...] * lk1_ref[...], axis=-1, keepdims=True)
    s2 = jnp.sum(lq2_ref[...] * lk2_ref[...], axis=-1, keepdims=True)
    return jnp.exp(s1) - jnp.exp(s2) + lam_init


def _diff_attn_kernel(q_ref, k_ref, v_ref, lq1_ref, lk1_ref, lq2_ref, lk2_ref, g_ref, o_ref,
                      acc_ref, s0_ref, s1_ref, *, lam_init):
    q = q_ref[...]
    lam = _diff_lambda(lq1_ref, lk1_ref, lq2_ref, lk2_ref, lam_init)
    lane = lax.broadcasted_iota(jnp.int32, (1, DIFF_W), 1)

    def scores(hc, lo_key, hi_key):
        lo = hc * DIFF_HEAD_DIM
        qm = q * jnp.where((lane >= lo) & (lane < lo + DIFF_HEAD_DIM), 1.0, 0.0).astype(BF16)
        return _dot_nt(k_ref[0, lo_key:hi_key, :], qm)

    def head_rows(hc):
        return slice((hc // 2) * DIFF_V_DIM, (hc // 2 + 1) * DIFF_V_DIM)

    def emit(hc, pv, denom):
        if hc % 2 == 0:
            acc_ref[head_rows(hc), :] = pv * (1.0 / denom)
        else:
            o = acc_ref[head_rows(hc), :] - pv * (lam / denom)
            ms = jnp.mean(o * o, axis=0, keepdims=True)
            acc_ref[head_rows(hc), :] = o * lax.rsqrt(ms + RMS_EPS)

    _softmax_maps(2 * DIFF_HEADS, scores, lambda hc: v_ref[0, head_rows(hc), :], emit,
                  (s0_ref, s1_ref))
    o_ref[...] = (acc_ref[...].T * g_ref[...] * (1.0 - lam_init)).astype(BF16)


def _diff_attn_call(layer, seq_len, n_keys, qd, kd, vd, lam_params, g_tiled):
    n_tok = qd.shape[0]
    tq = min(Q_BLOCK, seq_len)
    nq = seq_len // tq
    lam_init = 0.8 - 0.6 * math.exp(-0.3 * layer)
    in_specs = ([pl.BlockSpec((tq, DIFF_W), lambda b, i: (b * nq + i, 0)),
                 pl.BlockSpec((1, n_keys, DIFF_W), lambda b, i: (b, 0, 0)),
                 pl.BlockSpec((1, DIFF_W, n_keys), lambda b, i: (b, 0, 0))]
                + [_layer_spec(w, layer) for w in lam_params] + [_layer_spec(g_tiled, layer)])
    return pl.pallas_call(
        functools.partial(_diff_attn_kernel, lam_init=lam_init),
        grid=(n_tok // seq_len, nq),
        in_specs=in_specs,
        out_specs=pl.BlockSpec((tq, DIFF_W), lambda b, i: (b * nq + i, 0)),
        out_shape=jax.ShapeDtypeStruct((n_tok, DIFF_W), BF16),
        scratch_shapes=[pltpu.VMEM((DIFF_W, tq), F32),
                        pltpu.VMEM((n_keys, tq), F32), pltpu.VMEM((n_keys, tq), F32)],
        compiler_params=_cparams(2),
        name="diff_attn",
    )(qd, kd, vd, *lam_params, g_tiled)


def _mla_attn_kernel(q_ref, k_ref, v_ref, o_ref, acc_ref, s0_ref, s1_ref):
    def head_rows(h):
        return slice(h * MLA_V, (h + 1) * MLA_V)

    def emit(h, pv, denom):
        acc_ref[head_rows(h), :] = pv * (1.0 / denom)

    _softmax_maps(MLA_HEADS, lambda h, lo, hi: _dot_nt(k_ref[0, h, lo:hi, :], q_ref[h]),
                  lambda h: v_ref[0, head_rows(h), :], emit, (s0_ref, s1_ref))
    o_ref[...] = acc_ref[...].T.astype(BF16)


def _mla_attn_call(seq_len, n_keys, qc, kc, vm):
    n_tok = qc.shape[1]
    tq = min(Q_BLOCK, seq_len)
    nq = seq_len // tq
    in_specs = [pl.BlockSpec((MLA_HEADS, tq, MLA_HEAD_PAD), lambda b, i: (0, b * nq + i, 0)),
                pl.BlockSpec((1, MLA_HEADS, n_keys, MLA_HEAD_PAD), lambda b, i: (b, 0, 0, 0)),
                pl.BlockSpec((1, MLA_W, n_keys), lambda b, i: (b, 0, 0))]
    return pl.pallas_call(
        _mla_attn_kernel,
        grid=(n_tok // seq_len, nq),
        in_specs=in_specs,
        out_specs=pl.BlockSpec((tq, MLA_W), lambda b, i: (b * nq + i, 0)),
        out_shape=jax.ShapeDtypeStruct((n_tok, MLA_W), BF16),
        scratch_shapes=[pltpu.VMEM((MLA_W, tq), F32),
                        pltpu.VMEM((n_keys, tq), F32), pltpu.VMEM((n_keys, tq), F32)],
        compiler_params=_cparams(2),
        name="mla_attn",
    )(qc, kc, vm)


S5_CHUNK_STEPS = 64
S5_EPILOGUE_ROWS = 256


def _s5_kernel(*refs, steps, segmented):
    if segmented:
        (u_ref, bre_ref, bim_ref, cre_ref, cim_ref, lre_ref, lim_ref, ldt_ref, d_ref, wglu_ref,
         h0_ref, y_ref, up_ref, bur0_ref, bui0_ref, bur1_ref, bui1_ref, yacc_ref,
         inr_ref, ini_ref) = refs
    else:
        (u_ref, bre_ref, bim_ref, cre_ref, cim_ref, lre_ref, lim_ref, ldt_ref, d_ref, wglu_ref,
         y_ref, fin_ref, up_ref, bur0_ref, bui0_ref, bur1_ref, bui1_ref, yacc_ref) = refs
    bu_refs = ((bur0_ref, bui0_ref), (bur1_ref, bui1_ref))
    tc = S5_CHUNK_STEPS
    rows_c = tc * SUBLANES
    n_chunks = steps // tc
    yacc_ref[...] = jnp.zeros_like(yacc_ref)
    group_mask = (jnp.right_shift(lax.broadcasted_iota(jnp.int32, (S5_W, S5_N), 0), S5_CH_LOG2)
                  == jnp.right_shift(lax.broadcasted_iota(jnp.int32, (S5_W, S5_N), 1), S5_STATE_LOG2))

    def regroup(j, carry):
        dst = pl.ds(pl.multiple_of(j * SUBLANES, SUBLANES), SUBLANES)
        for half in range(S5_W // LANES):
            up_ref[dst, half * LANES:(half + 1) * LANES] = (
                u_ref[half, pl.ds(j, SUBLANES, stride=steps), :])
        return carry

    lax.fori_loop(0, steps, regroup, 0, unroll=8)

    for d in range(2):
        lam_re, lam_im = lre_ref[d:d + 1, :], lim_ref[d:d + 1, :]
        dt = jnp.exp(ldt_ref[d:d + 1, :])
        mag = jnp.exp(lam_re * dt)
        ang = lam_im * dt
        a_re, a_im = mag * jnp.cos(ang), mag * jnp.sin(ang)
        den = lam_re * lam_re + lam_im * lam_im
        n_re, n_im = a_re - 1.0, a_im
        f_re = (n_re * lam_re + n_im * lam_im) / den
        f_im = (n_im * lam_re - n_re * lam_im) / den
        def block_diagonal(rows16):
            dense = jnp.concatenate([rows16] * S5_GROUPS, axis=0)
            return jnp.where(group_mask, dense, 0.0).astype(BF16)

        bbar = jnp.concatenate([block_diagonal(f_re * bre_ref[d] - f_im * bim_ref[d]),
                                block_diagonal(f_re * bim_ref[d] + f_im * bre_ref[d])], axis=1)
        c_t = jnp.concatenate([block_diagonal(cre_ref[d]), block_diagonal(-cim_ref[d])], axis=1)
        ar8 = jnp.broadcast_to(a_re, (SUBLANES, S5_N))
        ai8 = jnp.broadcast_to(a_im, (SUBLANES, S5_N))

        def run_pass(init, store, d=d, bbar=bbar, c_t=c_t, ar8=ar8, ai8=ai8):
            def chunk_rows(ci):
                c = jnp.minimum(ci, n_chunks - 1)
                c = c if d == 0 else n_chunks - 1 - c
                return pl.ds(pl.multiple_of(c * rows_c, rows_c), rows_c)

            def project_in(ci, slot):
                bu = _dot(up_ref[chunk_rows(ci), :].astype(BF16), bbar)
                bu_refs[slot][0][...] = bu[:, :S5_N]
                bu_refs[slot][1][...] = bu[:, S5_N:]

            def scan(slot, hc):
                bur_ref, bui_ref = bu_refs[slot]
                hr, hi = hc
                for jj in range(tc):
                    j = jj if d == 0 else tc - 1 - jj
                    r = slice(j * SUBLANES, (j + 1) * SUBLANES)
                    hr, hi = (ar8 * hr - ai8 * hi + bur_ref[r, :],
                              ar8 * hi + ai8 * hr + bui_ref[r, :])
                    if store:
                        bur_ref[r, :] = hr
                        bui_ref[r, :] = hi
                return hr, hi

            def project_out(ci, slot):
                bur_ref, bui_ref = bu_refs[slot]
                h = jnp.concatenate([bur_ref[...].astype(BF16), bui_ref[...].astype(BF16)], axis=1)
                yc = _dot_nt(h, c_t)
                for half in range(S5_W // LANES):
                    yacc_ref[half, chunk_rows(ci), :] += yc[:, half * LANES:(half + 1) * LANES]

            def chunk_pair(cp, carry):
                for slot in range(2):
                    ci = 2 * cp + slot
                    project_in(ci + 1, 1 - slot)
                    carry = scan(slot, carry)
                    if store:
                        project_out(ci, slot)
                return carry

            project_in(0, 0)
            return lax.fori_loop(0, n_chunks // 2, chunk_pair, init)

        zeros = (jnp.zeros((SUBLANES, S5_N), F32), jnp.zeros((SUBLANES, S5_N), F32))
        if segmented:
            f_r, f_i = run_pass(zeros, False)
            p_re, p_im = a_re, a_im
            for _ in range(int(math.log2(steps))):
                p_re, p_im = p_re * p_re - p_im * p_im, 2.0 * p_re * p_im
            c_r, c_i = h0_ref[0, d, 0:1, :], h0_ref[0, d, 1:2, :]
            order = range(SUBLANES) if d == 0 else range(SUBLANES - 1, -1, -1)
            for s in order:
                inr_ref[s:s + 1, :] = c_r
                ini_ref[s:s + 1, :] = c_i
                c_r, c_i = (f_r[s:s + 1, :] + p_re * c_r - p_im * c_i,
                            f_i[s:s + 1, :] + p_re * c_i + p_im * c_r)
            run_pass((inr_ref[...], ini_ref[...]), True)
        else:
            f_r, f_i = run_pass(zeros, True)
            fin_ref[0, d, 0] = f_r
            fin_ref[0, d, 1] = f_i

    ep = S5_EPILOGUE_ROWS
    per_chain = steps // ep

    def epilogue(e, carry):
        chain, jc = e // per_chain, e % per_chain
        rows = pl.ds(pl.multiple_of(e * ep, ep), ep)
        src = pl.ds(jc * ep * SUBLANES + chain, ep, stride=SUBLANES)
        halves = range(S5_W // LANES)
        acc = jnp.concatenate([yacc_ref[half, src, :] for half in halves], axis=1)
        u = jnp.concatenate([u_ref[half, rows, :] for half in halves], axis=1)
        y = acc + u * d_ref[...]
        y = jax.nn.gelu(y, approximate=True)
        y_ref[rows, :] = y * jax.nn.sigmoid(_dot(y.astype(BF16), wglu_ref[...]))
        return carry

    lax.fori_loop(0, SUBLANES * per_chain, epilogue, 0)


def _s5_call(layer, u, steps, s5w, h0=None):
    n_tok = u.shape[1]
    n_rows = steps * SUBLANES
    nb = n_tok // n_rows
    segmented = h0 is not None
    assert steps % (2 * S5_CHUNK_STEPS) == 0 and steps % S5_EPILOGUE_ROWS == 0
    assert steps & (steps - 1) == 0
    consts = [s5w["b_re"], s5w["b_im"], s5w["c_re"], s5w["c_im"], s5w["lam_re"], s5w["lam_im"],
              s5w["log_dt"], s5w["d"], s5w["w_glu"]]
    in_specs = ([pl.BlockSpec((S5_W // LANES, n_rows, LANES), lambda b: (0, b, 0))]
                + [_layer_spec(c, layer, single_buffer=True) for c in consts])
    args = [u] + consts
    out_shape = [jax.ShapeDtypeStruct((n_tok, S5_W), F32)]
    out_specs = [pl.BlockSpec((n_rows, S5_W), lambda b: (b, 0))]
    scratch = ([pltpu.VMEM((n_rows, S5_W), F32)]
               + [pltpu.VMEM((S5_CHUNK_STEPS * SUBLANES, S5_N), F32)] * 4
               + [pltpu.VMEM((S5_W // LANES, n_rows, LANES), F32)])
    if segmented:
        in_specs.append(pl.BlockSpec((1, None, 2, 2, S5_N), lambda b: (b, layer, 0, 0, 0)))
        args.append(h0)
        scratch += [pltpu.VMEM((SUBLANES, S5_N), F32), pltpu.VMEM((SUBLANES, S5_N), F32)]
    else:
        out_shape.append(jax.ShapeDtypeStruct((nb, 2, 2, SUBLANES, S5_N), F32))
        out_specs.append(pl.BlockSpec((1, 2, 2, SUBLANES, S5_N), lambda b: (b, 0, 0, 0, 0)))
    outs = pl.pallas_call(
        functools.partial(_s5_kernel, steps=steps, segmented=segmented),
        grid=(nb,), in_specs=in_specs, out_specs=out_specs, out_shape=out_shape,
        scratch_shapes=scratch, compiler_params=_cparams(1),
        name="s5_lat" if segmented else "s5_ctx",
    )(*args)
    return (outs[0], None) if segmented else (outs[0], outs[1])


def _post_kernel(x_ref, a_ref, s_ref, m_ref, g1_ref, sh2_ref, sc2_ref, g2_ref,
                 wo_ref, wgu_ref, wd_ref, l1g_ref, l1b_ref, l2g_ref, l2b_ref, o_ref):
    subs = [slice(s * TOKEN_SUB, (s + 1) * TOKEN_SUB) for s in range(x_ref.shape[0] // TOKEN_SUB)]
    mixed = [jnp.concatenate([a_ref[rows, :], s_ref[rows, :].astype(BF16), m_ref[rows, :]], axis=1)
             for rows in subs]
    mix = [_dot(mi, wo_ref[...]) for mi in mixed]
    x1 = [_layer_norm(ALPHA * x_ref[rows, :] + g1_ref[0] * mx, l1g_ref[...], l1b_ref[...])
          for rows, mx in zip(subs, mix)]
    hb = [(v * (1.0 + sc2_ref[0]) + sh2_ref[0]).astype(BF16) for v in x1]
    f = []
    for h in hb:
        gu = _dot(h, wgu_ref[...])
        gate, up = gu[:, :FFN_HIDDEN], gu[:, FFN_HIDDEN:]
        f.append(_dot((gate * jax.nn.sigmoid(gate) * up).astype(BF16), wd_ref[...]))
    for s, rows in enumerate(subs):
        o_ref[rows, :] = _layer_norm(ALPHA * x1[s] + g2_ref[0] * f[s], l2g_ref[...], l2b_ref[...])


def _post_call(latent, layer, seq_len, x, attn_d, y_s5, attn_m, mods3, wts):
    n_tok = x.shape[0]
    tm = POST_BLOCK
    row = _mod_row(latent, layer, seq_len, tm)
    tok = lambda w: pl.BlockSpec((tm, w), lambda i: (i, 0))
    mod = lambda k: pl.BlockSpec((1, 1, D_MODEL), lambda i: (row(i) + k, 0, 0))
    weights = [wts["w_out"], wts["w_gate_up"], wts["w_down"],
               wts["ln1_g"], wts["ln1_b"], wts["ln2_g"], wts["ln2_b"]]
    w_specs = [_layer_spec(w, layer, single_buffer=True) for w in weights]
    return pl.pallas_call(
        _post_kernel,
        grid=(n_tok // tm,),
        in_specs=[tok(D_MODEL), tok(DIFF_W), tok(S5_W), tok(MLA_W), mod(2), mod(3), mod(4), mod(5)]
                 + w_specs,
        out_specs=tok(D_MODEL),
        out_shape=jax.ShapeDtypeStruct((n_tok, D_MODEL), F32),
        compiler_params=_cparams(1),
        name="post",
    )(x, attn_d, y_s5, attn_m, mods3, mods3, mods3, mods3, *weights)


def _rope_tables(length):
    rows = length // GRID_W
    row = np.repeat(np.arange(rows, dtype=np.float64), GRID_W)
    col = np.tile(np.arange(GRID_W, dtype=np.float64), rows)
    n_freq = DIFF_HEAD_DIM // 4
    inv = (ROPE_BASE ** (-np.arange(n_freq, dtype=np.float32) / np.float32(n_freq))).astype(np.float32)
    ang = np.concatenate([(row[:, None] * inv).astype(np.float32),
                          (col[:, None] * inv).astype(np.float32)], -1).astype(np.float64)
    cos, sin = np.cos(ang).astype(np.float32), np.sin(ang).astype(np.float32)
    zero = np.zeros_like(sin)
    cos32 = np.concatenate([cos, cos], -1)
    hi32 = np.concatenate([zero, sin], -1)
    lo32 = np.concatenate([-sin, zero], -1)
    diff = tuple(np.tile(t, (1, LANES // 32)) for t in (cos32, hi32, lo32))
    ones64, zeros64 = np.ones((length, 64), np.float32), np.zeros((length, 64), np.float32)
    ones32, zeros32 = np.ones((length, 32), np.float32), np.zeros((length, 32), np.float32)
    mla = (np.concatenate([ones64, cos32, ones32], -1),
           np.concatenate([zeros64, hi32, zeros32], -1),
           np.concatenate([zeros64, lo32, zeros32], -1))
    return tuple(jnp.asarray(t) for t in diff + mla)


def _stacked_weights(p):
    w_in = p["w_in"]
    krope_cols = jnp.pad(w_in[:, :, 1408:1440], ((0, 0), (0, 0), (64, 32)))
    w_in_ext = jnp.concatenate([w_in[:, :, :1408], krope_cols], axis=2).astype(BF16)
    w_uq = jnp.pad(p["mla_w_uq"].reshape(DEPTH, MLA_Q_RANK, MLA_HEADS, MLA_NOPE + MLA_ROPE),
                   ((0, 0), (0, 0), (0, 0), (0, MLA_HEAD_PAD - MLA_NOPE - MLA_ROPE)))
    w_uk = jnp.pad(p["mla_w_uk"].reshape(DEPTH, MLA_KV_RANK, MLA_HEADS, MLA_NOPE),
                   ((0, 0), (0, 0), (0, 0), (0, MLA_HEAD_PAD - MLA_NOPE)))
    prep = {
        "w_in": w_in_ext,
        "gq": p["mla_q_norm_g"].reshape(DEPTH, 1, MLA_Q_RANK),
        "gkv": p["mla_kv_norm_g"].reshape(DEPTH, 1, MLA_KV_RANK),
        "w_uq": w_uq.reshape(DEPTH, MLA_Q_RANK, MLA_HEADS * MLA_HEAD_PAD).astype(BF16),
        "w_ukv": jnp.concatenate(
            [w_uk.reshape(DEPTH, MLA_KV_RANK, MLA_HEADS * MLA_HEAD_PAD), p["mla_w_uv"]],
            axis=2).astype(BF16),
    }
    lam_params = [p[n].reshape(DEPTH, 1, DIFF_HEAD_DIM)
                  for n in ("diff_lq1", "diff_lk1", "diff_lq2", "diff_lk2")]
    g_tiled = jnp.tile(p["diff_norm_g"], (1, DIFF_HEADS)).reshape(DEPTH, 1, DIFF_W)
    s5w = {
        "b_re": jnp.transpose(p["s5_b_re"], (0, 1, 4, 2, 3)).reshape(DEPTH, 2, S5_CH, S5_N),
        "b_im": jnp.transpose(p["s5_b_im"], (0, 1, 4, 2, 3)).reshape(DEPTH, 2, S5_CH, S5_N),
        "c_re": jnp.transpose(p["s5_c_re"], (0, 1, 3, 2, 4)).reshape(DEPTH, 2, S5_CH, S5_N),
        "c_im": jnp.transpose(p["s5_c_im"], (0, 1, 3, 2, 4)).reshape(DEPTH, 2, S5_CH, S5_N),
        "lam_re": p["s5_lam_re"].reshape(DEPTH, 2, S5_N),
        "lam_im": p["s5_lam_im"].reshape(DEPTH, 2, S5_N),
        "log_dt": jnp.repeat(p["s5_log_dt"], S5_STATE, axis=-1),
        "d": p["s5_d"].reshape(DEPTH, 1, S5_W),
        "w_glu": p["s5_w_glu"].astype(BF16),
    }
    post = {
        "w_out": p["w_out"].astype(BF16),
        "w_gate_up": jnp.concatenate([p["ffn_w_gate"], p["ffn_w_up"]], axis=2).astype(BF16),
        "w_down": p["ffn_w_down"].astype(BF16),
        "ln1_g": p["ln1_g"].reshape(DEPTH, 1, D_MODEL), "ln1_b": p["ln1_b"].reshape(DEPTH, 1, D_MODEL),
        "ln2_g": p["ln2_g"].reshape(DEPTH, 1, D_MODEL), "ln2_b": p["ln2_b"].reshape(DEPTH, 1, D_MODEL),
    }
    return prep, lam_params, g_tiled, s5w, post


def kernel(x_prompt, x_sample, c, cache_diff_k, cache_diff_v, cache_mla_ckv, cache_mla_krope, state_s5, c_ctx, w_ada, b_ada, w_in, w_out, diff_lq1, diff_lk1, diff_lq2, diff_lk2, diff_norm_g, s5_lam_re, s5_lam_im, s5_log_dt, s5_b_re, s5_b_im, s5_c_re, s5_c_im, s5_d, s5_w_glu, mla_q_norm_g, mla_w_uq, mla_kv_norm_g, mla_w_uk, mla_w_uv, ln1_g, ln1_b, ln2_g, ln2_b, ffn_w_gate, ffn_w_up, ffn_w_down):
    p = dict(w_in=w_in, w_out=w_out, diff_lq1=diff_lq1, diff_lk1=diff_lk1, diff_lq2=diff_lq2,
             diff_lk2=diff_lk2, diff_norm_g=diff_norm_g, s5_lam_re=s5_lam_re, s5_lam_im=s5_lam_im,
             s5_log_dt=s5_log_dt, s5_b_re=s5_b_re, s5_b_im=s5_b_im, s5_c_re=s5_c_re, s5_c_im=s5_c_im,
             s5_d=s5_d, s5_w_glu=s5_w_glu, mla_q_norm_g=mla_q_norm_g, mla_w_uq=mla_w_uq,
             mla_kv_norm_g=mla_kv_norm_g, mla_w_uk=mla_w_uk, mla_w_uv=mla_w_uv, ln1_g=ln1_g,
             ln1_b=ln1_b, ln2_g=ln2_g, ln2_b=ln2_b, ffn_w_gate=ffn_w_gate, ffn_w_up=ffn_w_up,
             ffn_w_down=ffn_w_down)
    bsz, seq, _ = x_prompt.shape
    dec_b, dec_seq, _ = x_sample.shape
    past = cache_diff_k.shape[2]
    assert bsz % SUBLANES == 0 and dec_seq % SUBLANES == 0

    cond = jnp.concatenate([c_ctx[None, :], c, jnp.zeros((ADA_ROWS - 1 - dec_b, D_MODEL), F32)], 0)
    mods3 = _ada_call(cond, w_ada, b_ada).reshape(DEPTH * ADA_ROWS * 6, 1, D_MODEL)

    tables = _rope_tables(dec_seq)
    caches = (cache_diff_k.reshape(dec_b, DEPTH, past, DIFF_W),
              cache_diff_v.reshape(dec_b, DEPTH, past, DIFF_W),
              cache_mla_ckv,
              jnp.pad(cache_mla_krope, ((0, 0), (0, 0), (0, 0), (MLA_NOPE, LANES - MLA_NOPE - MLA_ROPE))))
    h0_all = jnp.moveaxis(state_s5, -1, 3).reshape(dec_b, DEPTH, 2, 2, S5_N)

    y_ctx = x_prompt.reshape(bsz * seq, D_MODEL)
    y_lat = x_sample.reshape(dec_b * dec_seq, D_MODEL)
    new_k, new_v, new_ckv, new_kr, new_st = [], [], [], [], []
    prep_w, lam_params, g_tiled, s5w, post_w = _stacked_weights(p)
    for l in range(DEPTH):
        qd, kd, vd, u, qc, kc, vm, k32, v32, ckv32, kr32 = _prep_call(
            False, l, y_ctx, seq, mods3, prep_w)
        attn_d = _diff_attn_call(l, seq, seq, qd, kd, vd, lam_params, g_tiled)
        attn_m = _mla_attn_call(seq, seq, qc, kc, vm)
        y_s5, fin = _s5_call(l, u, seq, s5w)
        y_ctx = _post_call(False, l, seq, y_ctx, attn_d, y_s5, attn_m, mods3, post_w)
        new_k.append(k32.reshape(bsz, seq, DIFF_HEADS, 2 * DIFF_HEAD_DIM))
        new_v.append(v32.reshape(bsz, seq, DIFF_HEADS, DIFF_V_DIM))
        new_ckv.append(ckv32.reshape(bsz, seq, MLA_KV_RANK))
        new_kr.append(kr32.reshape(bsz, seq, LANES)[:, :, MLA_NOPE:MLA_NOPE + MLA_ROPE])
        st = jnp.transpose(fin, (0, 3, 1, 4, 2)).reshape(bsz, 2, S5_GROUPS, S5_STATE, 2)
        new_st.append(st)

        qd, kd, vd, u, qc, kc, vm = _prep_call(True, l, y_lat, dec_seq, mods3, prep_w, tables, caches)
        attn_d = _diff_attn_call(l, dec_seq, dec_seq + past, qd, kd, vd, lam_params, g_tiled)
        attn_m = _mla_attn_call(dec_seq, dec_seq + past, qc, kc, vm)
        y_s5, _ = _s5_call(l, u, dec_seq // SUBLANES, s5w, h0_all)
        y_lat = _post_call(True, l, dec_seq, y_lat, attn_d, y_s5, attn_m, mods3, post_w)

    return (y_ctx.reshape(bsz, seq, D_MODEL), y_lat.reshape(dec_b, dec_seq, D_MODEL),
            jnp.stack(new_k, 1), jnp.stack(new_v, 1), jnp.stack(new_ckv, 1),
            jnp.stack(new_kr, 1), jnp.stack(new_st, 1))
```

```python
import functools
import math

import jax
import jax.numpy as jnp
import numpy as np
from jax import lax
from jax.experimental import pallas as pl
from jax.experimental.pallas import tpu as pltpu

F32 = jnp.float32
BF16 = jnp.bfloat16

D_MODEL = 1024
DEPTH = 2
GRID_W = 64
DIFF_HEADS = 4
DIFF_HEAD_DIM = 32
DIFF_V_DIM = 64
DIFF_W = 256
S5_CH = 16
S5_W = 256
S5_GROUPS = 16
S5_STATE = 64
S5_N = S5_GROUPS * S5_STATE
S5_CH_LOG2 = S5_CH.bit_length() - 1
S5_STATE_LOG2 = S5_STATE.bit_length() - 1
assert 1 << S5_CH_LOG2 == S5_CH and 1 << S5_STATE_LOG2 == S5_STATE
MLA_HEADS = 8
MLA_NOPE = 64
MLA_ROPE = 32
MLA_V = 64
MLA_Q_RANK = 256
MLA_KV_RANK = 128
MLA_W = 512
MLA_HEAD_PAD = 128
FFN_HIDDEN = 2816
ALPHA = (2 * DEPTH) ** 0.25
LN_EPS = 1e-5
RMS_EPS = 1e-6
ROPE_BASE = 10000.0
LOG2E = 1.4426950408889634

LANES = 128
SUBLANES = 8
TOKEN_BLOCK = 512
TOKEN_SUB = 256
Q_BLOCK = 256
KEY_PARTS = 4
SHORT_REQUESTS_PER_STEP = 4
VMEM_LIMIT = 56 * 1024 * 1024
ADA_ROWS = 8


def _cparams(n_axes):
    return pltpu.CompilerParams(dimension_semantics=("arbitrary",) * n_axes,
                                vmem_limit_bytes=VMEM_LIMIT)


def _layer_spec(arr, layer, single_buffer=False):
    rest = arr.shape[1:]
    mode = dict(pipeline_mode=pl.Buffered(1)) if single_buffer else {}
    return pl.BlockSpec((None,) + rest, lambda *_: (layer,) + (0,) * len(rest), **mode)


def _dot(a, b):
    return jnp.dot(a, b, preferred_element_type=F32)


def _dot_nt(a, b):
    return lax.dot_general(a, b, (((1,), (1,)), ((), ())), preferred_element_type=F32)


def _layer_norm(x, g, b):
    mu = jnp.mean(x, axis=-1, keepdims=True)
    xc = x - mu
    var = jnp.mean(xc * xc, axis=-1, keepdims=True)
    return xc * lax.rsqrt(var + LN_EPS) * g + b


def _rms_norm(x, g):
    return x * lax.rsqrt(jnp.mean(x * x, axis=-1, keepdims=True) + RMS_EPS) * g


def _rope(x, cos, sin_hi, sin_lo):
    outs = []
    for j in range(x.shape[1] // LANES):
        xb = x[:, j * LANES:(j + 1) * LANES]
        outs.append(xb * cos + pltpu.roll(xb, 16, 1) * sin_hi + pltpu.roll(xb, LANES - 16, 1) * sin_lo)
    return outs[0] if len(outs) == 1 else jnp.concatenate(outs, axis=1)


def _ada_kernel(cond_ref, w_ref, b_ref, o_ref):
    c = cond_ref[...]
    s = c * jax.nn.sigmoid(c)
    o_ref[0] = _dot(s.astype(BF16), w_ref[0].astype(BF16)) + b_ref[0]


def _ada_call(cond, w_ada, b_ada):
    n_blk = 6
    return pl.pallas_call(
        _ada_kernel,
        grid=(DEPTH, n_blk),
        in_specs=[
            pl.BlockSpec((ADA_ROWS, D_MODEL), lambda l, j: (0, 0)),
            pl.BlockSpec((1, D_MODEL, D_MODEL), lambda l, j: (l, 0, j)),
            pl.BlockSpec((1, 1, D_MODEL), lambda l, j: (l, 0, j)),
        ],
        out_specs=pl.BlockSpec((1, ADA_ROWS, D_MODEL), lambda l, j: (l, 0, j)),
        out_shape=jax.ShapeDtypeStruct((DEPTH, ADA_ROWS, 6 * D_MODEL), F32),
        compiler_params=_cparams(2),
        name="ada",
    )(cond, w_ada, b_ada.reshape(DEPTH, 1, 6 * D_MODEL))


def _mod_row(latent, layer, seq_len, tm):
    base = layer * ADA_ROWS
    if latent:
        return lambda i: (base + 1 + i // (seq_len // tm)) * 6
    return lambda i: base * 6


def _mla_keys(ckv, kr_wide, wukv_ref):
    kv = _dot(ckv.astype(BF16), wukv_ref[...])
    keys = [(kv[:, h * MLA_HEAD_PAD:(h + 1) * MLA_HEAD_PAD] + kr_wide).astype(BF16)
            for h in range(MLA_HEADS)]
    vv_t = kv[:, MLA_HEADS * MLA_HEAD_PAD:].T.astype(BF16)
    return keys, vv_t


def _prep_kernel(*refs, latent, n_new, q_scale_diff, q_scale_mla):
    if latent:
        (x_ref, sh_ref, sc_ref, win_ref, gq_ref, gkv_ref, wuq_ref, wukv_ref,
         cd_ref, shd_ref, sld_ref, cm_ref, shm_ref, slm_ref,
         ck_ref, cv_ref, cckv_ref, ckr_ref,
         qd_ref, kd_ref, vd_ref, u_ref, qc_ref, kc_ref, vm_ref) = refs
    else:
        (x_ref, sh_ref, sc_ref, win_ref, gq_ref, gkv_ref, wuq_ref, wukv_ref,
         qd_ref, kd_ref, vd_ref, u_ref, qc_ref, kc_ref, vm_ref,
         k32_ref, v32_ref, ckv32_ref, kr32_ref) = refs

    def put_keys(where, dk_b, dv_t, keys, vv_t):
        lead, rows = where
        kd_ref[lead, rows, :] = dk_b
        vd_ref[lead, :, rows] = dv_t
        for hh in range(MLA_HEADS):
            kc_ref[lead, hh, rows, :] = keys[hh]
        vm_ref[lead, :, rows] = vv_t

    def new_tokens(sub):
        rows = slice(sub * TOKEN_SUB, (sub + 1) * TOKEN_SUB)
        h = x_ref[rows, :] * (1.0 + sc_ref[0]) + sh_ref[0]
        z = _dot(h.astype(BF16), win_ref[...])
        yield
        dq, dk, dv = z[:, 0:256], z[:, 256:512], z[:, 512:768]
        u = z[:, 768:1024]
        q_lat, kv_lat, kr = z[:, 1024:1280], z[:, 1280:1408], z[:, 1408:1536]
        ckv = _rms_norm(kv_lat, gkv_ref[...])
        qn = _rms_norm(q_lat, gq_ref[...])
        qc = _dot(qn.astype(BF16), wuq_ref[...])
        yield
        if latent:
            cd, shd, sld = cd_ref[rows, :], shd_ref[rows, :], sld_ref[rows, :]
            cm, shm, slm = cm_ref[rows, :], shm_ref[rows, :], slm_ref[rows, :]
            dq = _rope(dq, cd, shd, sld)
            dk = _rope(dk, cd, shd, sld)
            qc = _rope(qc, cm, shm, slm)
            kr = _rope(kr, cm, shm, slm)
        else:
            k32_ref[rows, :] = dk
            v32_ref[rows, :] = dv
            ckv32_ref[rows, :] = ckv
            kr32_ref[rows, :] = kr
        keys, vv_t = _mla_keys(ckv, kr, wukv_ref)
        yield
        qd_ref[rows, :] = (dq * q_scale_diff).astype(BF16)
        u_ref[0, rows, :] = u[:, :LANES]
        u_ref[1, rows, :] = u[:, LANES:]
        qcs = (qc * q_scale_mla).astype(BF16)
        for hh in range(MLA_HEADS):
            qc_ref[hh, rows, :] = qcs[:, hh * MLA_HEAD_PAD:(hh + 1) * MLA_HEAD_PAD]
        where = (0, rows) if latent else (sub, slice(None))
        put_keys(where, dk.astype(BF16), dv.T.astype(BF16), keys, vv_t)

    def all_new_tokens():
        running = [new_tokens(sub) for sub in range(x_ref.shape[0] // TOKEN_SUB)]
        while running:
            running = [g for g in running if next(g, "done") != "done"]

    if latent:
        i = pl.program_id(1)
        pl.when(i < n_new)(all_new_tokens)

        @pl.when(i == n_new)
        def _():
            past = ck_ref.shape[2]
            keys, vv_t = _mla_keys(cckv_ref[0, 0], ckr_ref[0, 0], wukv_ref)
            put_keys((0, slice(0, past)), ck_ref[0, 0].astype(BF16),
                     cv_ref[0, 0].T.astype(BF16), keys, vv_t)
            n_pad = kd_ref.shape[1] - past
            put_keys((0, slice(past, kd_ref.shape[1])),
                     jnp.zeros((n_pad, DIFF_W), BF16), jnp.zeros((DIFF_W, n_pad), BF16),
                     [jnp.zeros((n_pad, MLA_HEAD_PAD), BF16)] * MLA_HEADS,
                     jnp.zeros((MLA_W, n_pad), BF16))
    else:
        all_new_tokens()


def _prep_call(latent, layer, x, seq_len, mods3, wts, tables=None, caches=None):
    n_tok = x.shape[0]
    tm = TOKEN_BLOCK
    n_req = n_tok // seq_len
    n_new = seq_len // tm if latent else 1
    weights = [wts["w_in"], wts["gq"], wts["gkv"], wts["w_uq"], wts["w_ukv"]]
    if latent:
        grid = (n_req, n_new + 1)
        key_rows = (n_new + 1) * tm
        blk = lambda b, i: b * n_new + jnp.minimum(i, n_new - 1)
        kblk = lambda b, i: (b, i)
        key_block = (1, tm)
    else:
        assert tm % seq_len == 0 and TOKEN_SUB == seq_len
        grid = (n_tok // tm,)
        key_rows = seq_len
        blk = lambda i: i
        kblk = lambda i: (i, 0)
        key_block = (tm // seq_len, seq_len)
    row = _mod_row(latent, layer, seq_len, tm)
    tok = lambda w: pl.BlockSpec((tm, w), lambda *g: (blk(*g), 0))
    heads = lambda n, w: pl.BlockSpec((n, tm, w), lambda *g: (0, blk(*g), 0))
    keys2 = lambda w: pl.BlockSpec(key_block + (w,), lambda *g: kblk(*g) + (0,))
    keys3 = lambda n, w: pl.BlockSpec((key_block[0], n, key_block[1], w),
                                      lambda *g: (kblk(*g)[0], 0, kblk(*g)[1], 0))
    vals_t = lambda w: pl.BlockSpec((key_block[0], w, key_block[1]),
                                    lambda *g: (kblk(*g)[0], 0, kblk(*g)[1]))
    in_specs = ([tok(D_MODEL),
                 pl.BlockSpec((1, 1, D_MODEL), lambda *g: (row(blk(*g)), 0, 0)),
                 pl.BlockSpec((1, 1, D_MODEL), lambda *g: (row(blk(*g)) + 1, 0, 0))]
                + [_layer_spec(w, layer) for w in weights])
    args = [x, mods3, mods3] + weights
    out_shape = [jax.ShapeDtypeStruct((n_tok, DIFF_W), BF16),
                 jax.ShapeDtypeStruct((n_req, key_rows, DIFF_W), BF16),
                 jax.ShapeDtypeStruct((n_req, DIFF_W, key_rows), BF16),
                 jax.ShapeDtypeStruct((S5_W // LANES, n_tok, LANES), F32),
                 jax.ShapeDtypeStruct((MLA_HEADS, n_tok, MLA_HEAD_PAD), BF16),
                 jax.ShapeDtypeStruct((n_req, MLA_HEADS, key_rows, MLA_HEAD_PAD), BF16),
                 jax.ShapeDtypeStruct((n_req, MLA_W, key_rows), BF16)]
    out_specs = [tok(DIFF_W), keys2(DIFF_W), vals_t(DIFF_W), heads(S5_W // LANES, LANES),
                 heads(MLA_HEADS, MLA_HEAD_PAD), keys3(MLA_HEADS, MLA_HEAD_PAD), vals_t(MLA_W)]
    if latent:
        past = caches[0].shape[2]
        assert past < tm
        in_specs += [pl.BlockSpec((tm, LANES), lambda b, i: (jnp.minimum(i, n_new - 1), 0))] * 6
        in_specs += [pl.BlockSpec((1, 1, past, w), lambda b, i: (b, layer, 0, 0))
                     for w in (DIFF_W, DIFF_W, LANES, LANES)]
        args += list(tables) + list(caches)
    else:
        out_shape += [jax.ShapeDtypeStruct((n_tok, DIFF_W), F32),
                      jax.ShapeDtypeStruct((n_tok, DIFF_W), F32),
                      jax.ShapeDtypeStruct((n_tok, MLA_KV_RANK), F32),
                      jax.ShapeDtypeStruct((n_tok, LANES), F32)]
        out_specs += [tok(DIFF_W), tok(DIFF_W), tok(MLA_KV_RANK), tok(LANES)]
    body = functools.partial(_prep_kernel, latent=latent, n_new=n_new,
                             q_scale_diff=DIFF_HEAD_DIM ** -0.5 * LOG2E,
                             q_scale_mla=(MLA_NOPE + MLA_ROPE) ** -0.5 * LOG2E)
    return pl.pallas_call(
        body, grid=grid, in_specs=in_specs, out_specs=out_specs, out_shape=out_shape,
        compiler_params=_cparams(len(grid)),
        name="prep_lat" if latent else "prep_ctx",
    )(*args)


def _softmax_maps(n_maps, scores_fn, values_fn, emit_fn, s_refs):
    requests = range(len(s_refs) // 2)
    n_keys = s_refs[0].shape[0]
    units = n_keys // LANES
    n_parts = max(1, min(KEY_PARTS, units // KEY_PARTS))
    cuts = [LANES * (units * p // n_parts) for p in range(n_parts + 1)]
    key_parts = list(zip(cuts[:-1], cuts[1:]))

    def stage_a(r, i):
        m = None
        for lo, hi in key_parts:
            s = scores_fn(r, i, lo, hi)
            s_refs[2 * r + i % 2][lo:hi, :] = s
            pm = jnp.max(s, axis=0, keepdims=True)
            m = pm if m is None else jnp.maximum(m, pm)
        return m

    def stage_b(r, i, m):
        v = values_fn(r, i)
        pv, denom = None, None
        for lo, hi in key_parts:
            e = jnp.exp2(s_refs[2 * r + i % 2][lo:hi, :] - m)
            ps = jnp.sum(e, axis=0, keepdims=True)
            pp = _dot(v[:, lo:hi], e.astype(BF16))
            pv, denom = (pp, ps) if pv is None else (pv + pp, denom + ps)
        emit_fn(r, i, pv, denom)

    m_prev = [stage_a(r, 0) for r in requests]
    for i in range(1, n_maps):
        m_cur = [stage_a(r, i) for r in requests]
        for r in requests:
            stage_b(r, i - 1, m_prev[r])
        m_prev = m_cur
    for r in requests:
        stage_b(r, n_maps - 1, m_prev[r])


def _diff_lambda(lq1_ref, lk1_ref, lq2_ref, lk2_ref, lam_init):
    s1 = jnp.sum(lq1_ref[...] * lk1_ref[...], axis=-1, keepdims=True)
    s2 = jnp.sum(lq2_ref[...] * lk2_ref[...], axis=-1, keepdims=True)
    return jnp.exp(s1) - jnp.exp(s2) + lam_init


def _diff_attn_kernel(q_ref, k_ref, v_ref, lq1_ref, lk1_ref, lq2_ref, lk2_ref, g_ref, o_ref,
                      acc_ref, *s_refs, lam_init):
    tq = acc_ref.shape[2]
    lam = _diff_lambda(lq1_ref, lk1_ref, lq2_ref, lk2_ref, lam_init)
    lane = lax.broadcasted_iota(jnp.int32, (1, DIFF_W), 1)

    def scores(r, hc, lo_key, hi_key):
        lo = hc * DIFF_HEAD_DIM
        qmask = jnp.where((lane >= lo) & (lane < lo + DIFF_HEAD_DIM), 1.0, 0.0).astype(BF16)
        qm = q_ref[r * tq:(r + 1) * tq, :] * qmask
        return _dot_nt(k_ref[r, lo_key:hi_key, :], qm)

    def head_rows(hc):
        return slice((hc // 2) * DIFF_V_DIM, (hc // 2 + 1) * DIFF_V_DIM)

    def emit(r, hc, pv, denom):
        if hc % 2 == 0:
            acc_ref[r, head_rows(hc), :] = pv * (1.0 / denom)
        else:
            o = acc_ref[r, head_rows(hc), :] - pv * (lam / denom)
            ms = jnp.mean(o * o, axis=0, keepdims=True)
            acc_ref[r, head_rows(hc), :] = o * lax.rsqrt(ms + RMS_EPS)

    _softmax_maps(2 * DIFF_HEADS, scores, lambda r, hc: v_ref[r, head_rows(hc), :], emit, s_refs)
    for r in range(acc_ref.shape[0]):
        o_ref[r * tq:(r + 1) * tq, :] = (acc_ref[r].T * g_ref[...] * (1.0 - lam_init)).astype(BF16)


def _requests_per_step(n_req, nq):
    return math.gcd(n_req, SHORT_REQUESTS_PER_STEP) if nq == 1 else 1


def _diff_attn_call(layer, seq_len, n_keys, qd, kd, vd, lam_params, g_tiled):
    n_tok = qd.shape[0]
    tq = min(Q_BLOCK, seq_len)
    nq = seq_len // tq
    n_req = n_tok // seq_len
    rs = _requests_per_step(n_req, nq)
    lam_init = 0.8 - 0.6 * math.exp(-0.3 * layer)
    in_specs = ([pl.BlockSpec((rs * tq, DIFF_W), lambda b, i: (b * nq + i, 0)),
                 pl.BlockSpec((rs, n_keys, DIFF_W), lambda b, i: (b, 0, 0)),
                 pl.BlockSpec((rs, DIFF_W, n_keys), lambda b, i: (b, 0, 0))]
                + [_layer_spec(w, layer) for w in lam_params] + [_layer_spec(g_tiled, layer)])
    return pl.pallas_call(
        functools.partial(_diff_attn_kernel, lam_init=lam_init),
        grid=(n_req // rs, nq),
        in_specs=in_specs,
        out_specs=pl.BlockSpec((rs * tq, DIFF_W), lambda b, i: (b * nq + i, 0)),
        out_shape=jax.ShapeDtypeStruct((n_tok, DIFF_W), BF16),
        scratch_shapes=([pltpu.VMEM((rs, DIFF_W, tq), F32)]
                        + [pltpu.VMEM((n_keys, tq), F32)] * (2 * rs)),
        compiler_params=_cparams(2),
        name="diff_attn",
    )(qd, kd, vd, *lam_params, g_tiled)


def _mla_attn_kernel(q_ref, k_ref, v_ref, o_ref, acc_ref, *s_refs):
    tq = acc_ref.shape[2]

    def head_rows(h):
        return slice(h * MLA_V, (h + 1) * MLA_V)

    def scores(r, h, lo, hi):
        return _dot_nt(k_ref[r, h, lo:hi, :], q_ref[h, r * tq:(r + 1) * tq, :])

    def emit(r, h, pv, denom):
        acc_ref[r, head_rows(h), :] = pv * (1.0 / denom)

    _softmax_maps(MLA_HEADS, scores, lambda r, h: v_ref[r, head_rows(h), :], emit, s_refs)
    for r in range(acc_ref.shape[0]):
        o_ref[r * tq:(r + 1) * tq, :] = acc_ref[r].T.astype(BF16)


def _mla_attn_call(seq_len, n_keys, qc, kc, vm):
    n_tok = qc.shape[1]
    tq = min(Q_BLOCK, seq_len)
    nq = seq_len // tq
    n_req = n_tok // seq_len
    rs = _requests_per_step(n_req, nq)
    in_specs = [pl.BlockSpec((MLA_HEADS, rs * tq, MLA_HEAD_PAD), lambda b, i: (0, b * nq + i, 0)),
                pl.BlockSpec((rs, MLA_HEADS, n_keys, MLA_HEAD_PAD), lambda b, i: (b, 0, 0, 0)),
                pl.BlockSpec((rs, MLA_W, n_keys), lambda b, i: (b, 0, 0))]
    return pl.pallas_call(
        _mla_attn_kernel,
        grid=(n_req // rs, nq),
        in_specs=in_specs,
        out_specs=pl.BlockSpec((rs * tq, MLA_W), lambda b, i: (b * nq + i, 0)),
        out_shape=jax.ShapeDtypeStruct((n_tok, MLA_W), BF16),
        scratch_shapes=([pltpu.VMEM((rs, MLA_W, tq), F32)]
                        + [pltpu.VMEM((n_keys, tq), F32)] * (2 * rs)),
        compiler_params=_cparams(2),
        name="mla_attn",
    )(qc, kc, vm)


S5_CHUNK_STEPS = 64
S5_EPILOGUE_ROWS = 256


def _s5_kernel(*refs, steps, segmented):
    if segmented:
        (u_ref, bre_ref, bim_ref, cre_ref, cim_ref, lre_ref, lim_ref, ldt_ref, d_ref, wglu_ref,
         h0_ref, y_ref, up_ref, bur0_ref, bui0_ref, bur1_ref, bui1_ref, yacc_ref,
         inr_ref, ini_ref) = refs
    else:
        (u_ref, bre_ref, bim_ref, cre_ref, cim_ref, lre_ref, lim_ref, ldt_ref, d_ref, wglu_ref,
         y_ref, fin_ref, up_ref, bur0_ref, bui0_ref, bur1_ref, bui1_ref, yacc_ref) = refs
    bu_refs = ((bur0_ref, bui0_ref), (bur1_ref, bui1_ref))
    tc = S5_CHUNK_STEPS
    rows_c = tc * SUBLANES
    n_chunks = steps // tc
    yacc_ref[...] = jnp.zeros_like(yacc_ref)
    group_mask = (jnp.right_shift(lax.broadcasted_iota(jnp.int32, (S5_W, S5_N), 0), S5_CH_LOG2)
                  == jnp.right_shift(lax.broadcasted_iota(jnp.int32, (S5_W, S5_N), 1), S5_STATE_LOG2))

    def regroup(j, carry):
        dst = pl.ds(pl.multiple_of(j * SUBLANES, SUBLANES), SUBLANES)
        for half in range(S5_W // LANES):
            up_ref[dst, half * LANES:(half + 1) * LANES] = (
                u_ref[half, pl.ds(j, SUBLANES, stride=steps), :])
        return carry

    lax.fori_loop(0, steps, regroup, 0, unroll=8)

    for d in range(2):
        lam_re, lam_im = lre_ref[d:d + 1, :], lim_ref[d:d + 1, :]
        dt = jnp.exp(ldt_ref[d:d + 1, :])
        mag = jnp.exp(lam_re * dt)
        ang = lam_im * dt
        a_re, a_im = mag * jnp.cos(ang), mag * jnp.sin(ang)
        den = lam_re * lam_re + lam_im * lam_im
        n_re, n_im = a_re - 1.0, a_im
        f_re = (n_re * lam_re + n_im * lam_im) / den
        f_im = (n_im * lam_re - n_re * lam_im) / den
        def block_diagonal(rows16):
            dense = jnp.concatenate([rows16] * S5_GROUPS, axis=0)
            return jnp.where(group_mask, dense, 0.0).astype(BF16)

        bbar = jnp.concatenate([block_diagonal(f_re * bre_ref[d] - f_im * bim_ref[d]),
                                block_diagonal(f_re * bim_ref[d] + f_im * bre_ref[d])], axis=1)
        c_t = jnp.concatenate([block_diagonal(cre_ref[d]), block_diagonal(-cim_ref[d])], axis=1)
        ar8 = jnp.broadcast_to(a_re, (SUBLANES, S5_N))
        ai8 = jnp.broadcast_to(a_im, (SUBLANES, S5_N))

        def run_pass(init, store, d=d, bbar=bbar, c_t=c_t, ar8=ar8, ai8=ai8):
            def chunk_rows(ci):
                c = jnp.minimum(ci, n_chunks - 1)
                c = c if d == 0 else n_chunks - 1 - c
                return pl.ds(pl.multiple_of(c * rows_c, rows_c), rows_c)

            def project_in(ci, slot):
                bu = _dot(up_ref[chunk_rows(ci), :].astype(BF16), bbar)
                bu_refs[slot][0][...] = bu[:, :S5_N]
                bu_refs[slot][1][...] = bu[:, S5_N:]

            def scan(slot, hc):
                bur_ref, bui_ref = bu_refs[slot]
                hr, hi = hc
                for jj in range(tc):
                    j = jj if d == 0 else tc - 1 - jj
                    r = slice(j * SUBLANES, (j + 1) * SUBLANES)
                    hr, hi = (ar8 * hr - ai8 * hi + bur_ref[r, :],
                              ar8 * hi + ai8 * hr + bui_ref[r, :])
                    if store:
                        bur_ref[r, :] = hr
                        bui_ref[r, :] = hi
                return hr, hi

            def project_out(ci, slot):
                bur_ref, bui_ref = bu_refs[slot]
                h = jnp.concatenate([bur_ref[...].astype(BF16), bui_ref[...].astype(BF16)], axis=1)
                yc = _dot_nt(h, c_t)
                for half in range(S5_W // LANES):
                    yacc_ref[half, chunk_rows(ci), :] += yc[:, half * LANES:(half + 1) * LANES]

            def chunk_pair(cp, carry):
                for slot in range(2):
                    ci = 2 * cp + slot
                    project_in(ci + 1, 1 - slot)
                    carry = scan(slot, carry)
                    if store:
                        project_out(ci, slot)
                return carry

            project_in(0, 0)
            return lax.fori_loop(0, n_chunks // 2, chunk_pair, init)

        zeros = (jnp.zeros((SUBLANES, S5_N), F32), jnp.zeros((SUBLANES, S5_N), F32))
        if segmented:
            f_r, f_i = run_pass(zeros, False)
            p_re, p_im = a_re, a_im
            for _ in range(int(math.log2(steps))):
                p_re, p_im = p_re * p_re - p_im * p_im, 2.0 * p_re * p_im
            c_r, c_i = h0_ref[0, d, 0:1, :], h0_ref[0, d, 1:2, :]
            order = range(SUBLANES) if d == 0 else range(SUBLANES - 1, -1, -1)
            for s in order:
                inr_ref[s:s + 1, :] = c_r
                ini_ref[s:s + 1, :] = c_i
                c_r, c_i = (f_r[s:s + 1, :] + p_re * c_r - p_im * c_i,
                            f_i[s:s + 1, :] + p_re * c_i + p_im * c_r)
            run_pass((inr_ref[...], ini_ref[...]), True)
        else:
            f_r, f_i = run_pass(zeros, True)
            fin_ref[0, d, 0] = f_r
            fin_ref[0, d, 1] = f_i

    ep = S5_EPILOGUE_ROWS
    per_chain = steps // ep

    def epilogue(e, carry):
        chain, jc = e // per_chain, e % per_chain
        rows = pl.ds(pl.multiple_of(e * ep, ep), ep)
        src = pl.ds(jc * ep * SUBLANES + chain, ep, stride=SUBLANES)
        halves = range(S5_W // LANES)
        acc = jnp.concatenate([yacc_ref[half, src, :] for half in halves], axis=1)
        u = jnp.concatenate([u_ref[half, rows, :] for half in halves], axis=1)
        y = acc + u * d_ref[...]
        y = jax.nn.gelu(y, approximate=True)
        y_ref[rows, :] = y * jax.nn.sigmoid(_dot(y.astype(BF16), wglu_ref[...]))
        return carry

    lax.fori_loop(0, SUBLANES * per_chain, epilogue, 0)


def _s5_call(layer, u, steps, s5w, h0=None):
    n_tok = u.shape[1]
    n_rows = steps * SUBLANES
    nb = n_tok // n_rows
    segmented = h0 is not None
    assert steps % (2 * S5_CHUNK_STEPS) == 0 and steps % S5_EPILOGUE_ROWS == 0
    assert steps & (steps - 1) == 0
    consts = [s5w["b_re"], s5w["b_im"], s5w["c_re"], s5w["c_im"], s5w["lam_re"], s5w["lam_im"],
              s5w["log_dt"], s5w["d"], s5w["w_glu"]]
    in_specs = ([pl.BlockSpec((S5_W // LANES, n_rows, LANES), lambda b: (0, b, 0))]
                + [_layer_spec(c, layer, single_buffer=True) for c in consts])
    args = [u] + consts
    out_shape = [jax.ShapeDtypeStruct((n_tok, S5_W), F32)]
    out_specs = [pl.BlockSpec((n_rows, S5_W), lambda b: (b, 0))]
    scratch = ([pltpu.VMEM((n_rows, S5_W), F32)]
               + [pltpu.VMEM((S5_CHUNK_STEPS * SUBLANES, S5_N), F32)] * 4
               + [pltpu.VMEM((S5_W // LANES, n_rows, LANES), F32)])
    if segmented:
        in_specs.append(pl.BlockSpec((1, None, 2, 2, S5_N), lambda b: (b, layer, 0, 0, 0)))
        args.append(h0)
        scratch += [pltpu.VMEM((SUBLANES, S5_N), F32), pltpu.VMEM((SUBLANES, S5_N), F32)]
    else:
        out_shape.append(jax.ShapeDtypeStruct((nb, 2, 2, SUBLANES, S5_N), F32))
        out_specs.append(pl.BlockSpec((1, 2, 2, SUBLANES, S5_N), lambda b: (b, 0, 0, 0, 0)))
    outs = pl.pallas_call(
        functools.partial(_s5_kernel, steps=steps, segmented=segmented),
        grid=(nb,), in_specs=in_specs, out_specs=out_specs, out_shape=out_shape,
        scratch_shapes=scratch, compiler_params=_cparams(1),
        name="s5_lat" if segmented else "s5_ctx",
    )(*args)
    return (outs[0], None) if segmented else (outs[0], outs[1])


def _post_kernel(x_ref, a_ref, s_ref, m_ref, g1_ref, sh2_ref, sc2_ref, g2_ref,
                 wo_ref, wgu_ref, wd_ref, l1g_ref, l1b_ref, l2g_ref, l2b_ref, o_ref):
    subs = [slice(s * TOKEN_SUB, (s + 1) * TOKEN_SUB) for s in range(x_ref.shape[0] // TOKEN_SUB)]
    mixed = [jnp.concatenate([a_ref[rows, :], s_ref[rows, :].astype(BF16), m_ref[rows, :]], axis=1)
             for rows in subs]
    mix = [_dot(mi, wo_ref[...]) for mi in mixed]
    x1 = [_layer_norm(ALPHA * x_ref[rows, :] + g1_ref[0] * mx, l1g_ref[...], l1b_ref[...])
          for rows, mx in zip(subs, mix)]
    hb = [(v * (1.0 + sc2_ref[0]) + sh2_ref[0]).astype(BF16) for v in x1]
    f = []
    for h in hb:
        gu = _dot(h, wgu_ref[...])
        gate, up = gu[:, :FFN_HIDDEN], gu[:, FFN_HIDDEN:]
        f.append(_dot((gate * jax.nn.sigmoid(gate) * up).astype(BF16), wd_ref[...]))
    for s, rows in enumerate(subs):
        o_ref[rows, :] = _layer_norm(ALPHA * x1[s] + g2_ref[0] * f[s], l2g_ref[...], l2b_ref[...])


def _post_call(latent, layer, seq_len, x, attn_d, y_s5, attn_m, mods3, wts):
    n_tok = x.shape[0]
    tm = TOKEN_BLOCK
    row = _mod_row(latent, layer, seq_len, tm)
    tok = lambda w: pl.BlockSpec((tm, w), lambda i: (i, 0))
    mod = lambda k: pl.BlockSpec((1, 1, D_MODEL), lambda i: (row(i) + k, 0, 0))
    weights = [wts["w_out"], wts["w_gate_up"], wts["w_down"],
               wts["ln1_g"], wts["ln1_b"], wts["ln2_g"], wts["ln2_b"]]
    w_specs = [_layer_spec(w, layer, single_buffer=True) for w in weights]
    return pl.pallas_call(
        _post_kernel,
        grid=(n_tok // tm,),
        in_specs=[tok(D_MODEL), tok(DIFF_W), tok(S5_W), tok(MLA_W), mod(2), mod(3), mod(4), mod(5)]
                 + w_specs,
        out_specs=tok(D_MODEL),
        out_shape=jax.ShapeDtypeStruct((n_tok, D_MODEL), F32),
        compiler_params=_cparams(1),
        name="post",
    )(x, attn_d, y_s5, attn_m, mods3, mods3, mods3, mods3, *weights)


def _rope_tables(length):
    rows = length // GRID_W
    row = np.repeat(np.arange(rows, dtype=np.float64), GRID_W)
    col = np.tile(np.arange(GRID_W, dtype=np.float64), rows)
    n_freq = DIFF_HEAD_DIM // 4
    inv = (ROPE_BASE ** (-np.arange(n_freq, dtype=np.float32) / np.float32(n_freq))).astype(np.float32)
    ang = np.concatenate([(row[:, None] * inv).astype(np.float32),
                          (col[:, None] * inv).astype(np.float32)], -1).astype(np.float64)
    cos, sin = np.cos(ang).astype(np.float32), np.sin(ang).astype(np.float32)
    zero = np.zeros_like(sin)
    cos32 = np.concatenate([cos, cos], -1)
    hi32 = np.concatenate([zero, sin], -1)
    lo32 = np.concatenate([-sin, zero], -1)
    diff = tuple(np.tile(t, (1, LANES // 32)) for t in (cos32, hi32, lo32))
    ones64, zeros64 = np.ones((length, 64), np.float32), np.zeros((length, 64), np.float32)
    ones32, zeros32 = np.ones((length, 32), np.float32), np.zeros((length, 32), np.float32)
    mla = (np.concatenate([ones64, cos32, ones32], -1),
           np.concatenate([zeros64, hi32, zeros32], -1),
           np.concatenate([zeros64, lo32, zeros32], -1))
    return tuple(jnp.asarray(t) for t in diff + mla)


def _stacked_weights(p):
    w_in = p["w_in"]
    krope_cols = jnp.pad(w_in[:, :, 1408:1440], ((0, 0), (0, 0), (64, 32)))
    w_in_ext = jnp.concatenate([w_in[:, :, :1408], krope_cols], axis=2).astype(BF16)
    w_uq = jnp.pad(p["mla_w_uq"].reshape(DEPTH, MLA_Q_RANK, MLA_HEADS, MLA_NOPE + MLA_ROPE),
                   ((0, 0), (0, 0), (0, 0), (0, MLA_HEAD_PAD - MLA_NOPE - MLA_ROPE)))
    w_uk = jnp.pad(p["mla_w_uk"].reshape(DEPTH, MLA_KV_RANK, MLA_HEADS, MLA_NOPE),
                   ((0, 0), (0, 0), (0, 0), (0, MLA_HEAD_PAD - MLA_NOPE)))
    prep = {
        "w_in": w_in_ext,
        "gq": p["mla_q_norm_g"].reshape(DEPTH, 1, MLA_Q_RANK),
        "gkv": p["mla_kv_norm_g"].reshape(DEPTH, 1, MLA_KV_RANK),
        "w_uq": w_uq.reshape(DEPTH, MLA_Q_RANK, MLA_HEADS * MLA_HEAD_PAD).astype(BF16),
        "w_ukv": jnp.concatenate(
            [w_uk.reshape(DEPTH, MLA_KV_RANK, MLA_HEADS * MLA_HEAD_PAD), p["mla_w_uv"]],
            axis=2).astype(BF16),
    }
    lam_params = [p[n].reshape(DEPTH, 1, DIFF_HEAD_DIM)
                  for n in ("diff_lq1", "diff_lk1", "diff_lq2", "diff_lk2")]
    g_tiled = jnp.tile(p["diff_norm_g"], (1, DIFF_HEADS)).reshape(DEPTH, 1, DIFF_W)
    s5w = {
        "b_re": jnp.transpose(p["s5_b_re"], (0, 1, 4, 2, 3)).reshape(DEPTH, 2, S5_CH, S5_N),
        "b_im": jnp.transpose(p["s5_b_im"], (0, 1, 4, 2, 3)).reshape(DEPTH, 2, S5_CH, S5_N),
        "c_re": jnp.transpose(p["s5_c_re"], (0, 1, 3, 2, 4)).reshape(DEPTH, 2, S5_CH, S5_N),
        "c_im": jnp.transpose(p["s5_c_im"], (0, 1, 3, 2, 4)).reshape(DEPTH, 2, S5_CH, S5_N),
        "lam_re": p["s5_lam_re"].reshape(DEPTH, 2, S5_N),
        "lam_im": p["s5_lam_im"].reshape(DEPTH, 2, S5_N),
        "log_dt": jnp.repeat(p["s5_log_dt"], S5_STATE, axis=-1),
        "d": p["s5_d"].reshape(DEPTH, 1, S5_W),
        "w_glu": p["s5_w_glu"].astype(BF16),
    }
    post = {
        "w_out": p["w_out"].astype(BF16),
        "w_gate_up": jnp.concatenate([p["ffn_w_gate"], p["ffn_w_up"]], axis=2).astype(BF16),
        "w_down": p["ffn_w_down"].astype(BF16),
        "ln1_g": p["ln1_g"].reshape(DEPTH, 1, D_MODEL), "ln1_b": p["ln1_b"].reshape(DEPTH, 1, D_MODEL),
        "ln2_g": p["ln2_g"].reshape(DEPTH, 1, D_MODEL), "ln2_b": p["ln2_b"].reshape(DEPTH, 1, D_MODEL),
    }
    return prep, lam_params, g_tiled, s5w, post


def kernel(x_prompt, x_sample, c, cache_diff_k, cache_diff_v, cache_mla_ckv, cache_mla_krope, state_s5, c_ctx, w_ada, b_ada, w_in, w_out, diff_lq1, diff_lk1, diff_lq2, diff_lk2, diff_norm_g, s5_lam_re, s5_lam_im, s5_log_dt, s5_b_re, s5_b_im, s5_c_re, s5_c_im, s5_d, s5_w_glu, mla_q_norm_g, mla_w_uq, mla_kv_norm_g, mla_w_uk, mla_w_uv, ln1_g, ln1_b, ln2_g, ln2_b, ffn_w_gate, ffn_w_up, ffn_w_down):
    p = dict(w_in=w_in, w_out=w_out, diff_lq1=diff_lq1, diff_lk1=diff_lk1, diff_lq2=diff_lq2,
             diff_lk2=diff_lk2, diff_norm_g=diff_norm_g, s5_lam_re=s5_lam_re, s5_lam_im=s5_lam_im,
             s5_log_dt=s5_log_dt, s5_b_re=s5_b_re, s5_b_im=s5_b_im, s5_c_re=s5_c_re, s5_c_im=s5_c_im,
             s5_d=s5_d, s5_w_glu=s5_w_glu, mla_q_norm_g=mla_q_norm_g, mla_w_uq=mla_w_uq,
             mla_kv_norm_g=mla_kv_norm_g, mla_w_uk=mla_w_uk, mla_w_uv=mla_w_uv, ln1_g=ln1_g,
             ln1_b=ln1_b, ln2_g=ln2_g, ln2_b=ln2_b, ffn_w_gate=ffn_w_gate, ffn_w_up=ffn_w_up,
             ffn_w_down=ffn_w_down)
    bsz, seq, _ = x_prompt.shape
    dec_b, dec_seq, _ = x_sample.shape
    past = cache_diff_k.shape[2]
    assert bsz % SUBLANES == 0 and dec_seq % SUBLANES == 0

    cond = jnp.concatenate([c_ctx[None, :], c, jnp.zeros((ADA_ROWS - 1 - dec_b, D_MODEL), F32)], 0)
    mods3 = _ada_call(cond, w_ada, b_ada).reshape(DEPTH * ADA_ROWS * 6, 1, D_MODEL)

    tables = _rope_tables(dec_seq)
    caches = (cache_diff_k.reshape(dec_b, DEPTH, past, DIFF_W),
              cache_diff_v.reshape(dec_b, DEPTH, past, DIFF_W),
              cache_mla_ckv,
              jnp.pad(cache_mla_krope, ((0, 0), (0, 0), (0, 0), (MLA_NOPE, LANES - MLA_NOPE - MLA_ROPE))))
    h0_all = jnp.moveaxis(state_s5, -1, 3).reshape(dec_b, DEPTH, 2, 2, S5_N)

    y_ctx = x_prompt.reshape(bsz * seq, D_MODEL)
    y_lat = x_sample.reshape(dec_b * dec_seq, D_MODEL)
    new_k, new_v, new_ckv, new_kr, new_st = [], [], [], [], []
    prep_w, lam_params, g_tiled, s5w, post_w = _stacked_weights(p)
    for l in range(DEPTH):
        qd, kd, vd, u, qc, kc, vm, k32, v32, ckv32, kr32 = _prep_call(
            False, l, y_ctx, seq, mods3, prep_w)
        attn_d = _diff_attn_call(l, seq, seq, qd, kd, vd, lam_params, g_tiled)
        attn_m = _mla_attn_call(seq, seq, qc, kc, vm)
        y_s5, fin = _s5_call(l, u, seq, s5w)
        y_ctx = _post_call(False, l, seq, y_ctx, attn_d, y_s5, attn_m, mods3, post_w)
        new_k.append(k32.reshape(bsz, seq, DIFF_HEADS, 2 * DIFF_HEAD_DIM))
        new_v.append(v32.reshape(bsz, seq, DIFF_HEADS, DIFF_V_DIM))
        new_ckv.append(ckv32.reshape(bsz, seq, MLA_KV_RANK))
        new_kr.append(kr32.reshape(bsz, seq, LANES)[:, :, MLA_NOPE:MLA_NOPE + MLA_ROPE])
        st = jnp.transpose(fin, (0, 3, 1, 4, 2)).reshape(bsz, 2, S5_GROUPS, S5_STATE, 2)
        new_st.append(st)

        qd, kd, vd, u, qc, kc, vm = _prep_call(True, l, y_lat, dec_seq, mods3, prep_w, tables, caches)
        attn_d = _diff_attn_call(l, dec_seq, dec_seq + past, qd, kd, vd, lam_params, g_tiled)
        attn_m = _mla_attn_call(dec_seq, dec_seq + past, qc, kc, vm)
        y_s5, _ = _s5_call(l, u, dec_seq // SUBLANES, s5w, h0_all)
        y_lat = _post_call(True, l, dec_seq, y_lat, attn_d, y_s5, attn_m, mods3, post_w)

    return (y_ctx.reshape(bsz, seq, D_MODEL), y_lat.reshape(dec_b, dec_seq, D_MODEL),
            jnp.stack(new_k, 1), jnp.stack(new_v, 1), jnp.stack(new_ckv, 1),
            jnp.stack(new_kr, 1), jnp.stack(new_st, 1))
```

```python
import functools
import math

import jax
import jax.numpy as jnp
import numpy as np
from jax import lax
from jax.experimental import pallas as pl
from jax.experimental.pallas import tpu as pltpu

F32 = jnp.float32
BF16 = jnp.bfloat16

D_MODEL = 1024
DEPTH = 2
GRID_W = 64
DIFF_HEADS = 4
DIFF_HEAD_DIM = 32
DIFF_V_DIM = 64
DIFF_W = 256
S5_CH = 16
S5_W = 256
S5_GROUPS = 16
S5_STATE = 64
S5_N = S5_GROUPS * S5_STATE
S5_CH_LOG2 = S5_CH.bit_length() - 1
S5_STATE_LOG2 = S5_STATE.bit_length() - 1
assert 1 << S5_CH_LOG2 == S5_CH and 1 << S5_STATE_LOG2 == S5_STATE
MLA_HEADS = 8
MLA_NOPE = 64
MLA_ROPE = 32
MLA_V = 64
MLA_Q_RANK = 256
MLA_KV_RANK = 128
MLA_W = 512
MLA_HEAD_PAD = 128
FFN_HIDDEN = 2816
ALPHA = (2 * DEPTH) ** 0.25
LN_EPS = 1e-5
RMS_EPS = 1e-6
ROPE_BASE = 10000.0
LOG2E = 1.4426950408889634

LANES = 128
SUBLANES = 8
TOKEN_BLOCK = 512
TOKEN_SUB = 256
Q_BLOCK = 256
KEY_PARTS = 4
SHORT_REQUESTS_PER_STEP = 4
VMEM_LIMIT = 56 * 1024 * 1024
ADA_ROWS = 8


def _cparams(n_axes):
    return pltpu.CompilerParams(dimension_semantics=("arbitrary",) * n_axes,
                                vmem_limit_bytes=VMEM_LIMIT)


def _layer_spec(arr, layer, single_buffer=False):
    rest = arr.shape[1:]
    mode = dict(pipeline_mode=pl.Buffered(1)) if single_buffer else {}
    return pl.BlockSpec((None,) + rest, lambda *_: (layer,) + (0,) * len(rest), **mode)


def _dot(a, b):
    return jnp.dot(a, b, preferred_element_type=F32)


def _dot_nt(a, b):
    return lax.dot_general(a, b, (((1,), (1,)), ((), ())), preferred_element_type=F32)


def _layer_norm(x, g, b):
    mu = jnp.mean(x, axis=-1, keepdims=True)
    xc = x - mu
    var = jnp.mean(xc * xc, axis=-1, keepdims=True)
    return xc * lax.rsqrt(var + LN_EPS) * g + b


def _rms_norm(x, g):
    return x * lax.rsqrt(jnp.mean(x * x, axis=-1, keepdims=True) + RMS_EPS) * g


def _rope(x, cos, sin_hi, sin_lo):
    outs = []
    for j in range(x.shape[1] // LANES):
        xb = x[:, j * LANES:(j + 1) * LANES]
        outs.append(xb * cos + pltpu.roll(xb, 16, 1) * sin_hi + pltpu.roll(xb, LANES - 16, 1) * sin_lo)
    return outs[0] if len(outs) == 1 else jnp.concatenate(outs, axis=1)


def _ada_kernel(cond_ref, w_ref, b_ref, o_ref):
    c = cond_ref[...]
    s = c * jax.nn.sigmoid(c)
    o_ref[0] = _dot(s.astype(BF16), w_ref[0].astype(BF16)) + b_ref[0]


def _ada_call(cond, w_ada, b_ada):
    n_blk = 6
    return pl.pallas_call(
        _ada_kernel,
        grid=(DEPTH, n_blk),
        in_specs=[
            pl.BlockSpec((ADA_ROWS, D_MODEL), lambda l, j: (0, 0)),
            pl.BlockSpec((1, D_MODEL, D_MODEL), lambda l, j: (l, 0, j)),
            pl.BlockSpec((1, 1, D_MODEL), lambda l, j: (l, 0, j)),
        ],
        out_specs=pl.BlockSpec((1, ADA_ROWS, D_MODEL), lambda l, j: (l, 0, j)),
        out_shape=jax.ShapeDtypeStruct((DEPTH, ADA_ROWS, 6 * D_MODEL), F32),
        compiler_params=_cparams(2),
        name="ada",
    )(cond, w_ada, b_ada.reshape(DEPTH, 1, 6 * D_MODEL))


def _mod_row(latent, layer, seq_len, tm):
    base = layer * ADA_ROWS
    if latent:
        return lambda i: (base + 1 + i // (seq_len // tm)) * 6
    return lambda i: base * 6


def _mla_keys(ckv, kr_wide, wukv_ref):
    kv = _dot(ckv.astype(BF16), wukv_ref[...])
    keys = [(kv[:, h * MLA_HEAD_PAD:(h + 1) * MLA_HEAD_PAD] + kr_wide).astype(BF16)
            for h in range(MLA_HEADS)]
    vv_t = kv[:, MLA_HEADS * MLA_HEAD_PAD:].T.astype(BF16)
    return keys, vv_t


def _prep_kernel(*refs, latent, n_new, q_scale_diff, q_scale_mla):
    if latent:
        (x_ref, sh_ref, sc_ref, win_ref, gq_ref, gkv_ref, wuq_ref, wukv_ref,
         cd_ref, shd_ref, sld_ref, cm_ref, shm_ref, slm_ref,
         ck_ref, cv_ref, cckv_ref, ckr_ref,
         qd_ref, kd_ref, vd_ref, u_ref, qc_ref, kc_ref, vm_ref) = refs
    else:
        (x_ref, sh_ref, sc_ref, win_ref, gq_ref, gkv_ref, wuq_ref, wukv_ref,
         qd_ref, kd_ref, vd_ref, u_ref, qc_ref, kc_ref, vm_ref,
         k32_ref, v32_ref, ckv32_ref, kr32_ref) = refs

    def put_keys(where, dk_b, dv_t, keys, vv_t):
        lead, rows = where
        kd_ref[lead, rows, :] = dk_b
        vd_ref[lead, :, rows] = dv_t
        for hh in range(MLA_HEADS):
            kc_ref[lead, hh, rows, :] = keys[hh]
        vm_ref[lead, :, rows] = vv_t

    def new_tokens(sub):
        rows = slice(sub * TOKEN_SUB, (sub + 1) * TOKEN_SUB)
        h = x_ref[rows, :] * (1.0 + sc_ref[0]) + sh_ref[0]
        z = _dot(h.astype(BF16), win_ref[...])
        yield
        dq, dk, dv = z[:, 0:256], z[:, 256:512], z[:, 512:768]
        u = z[:, 768:1024]
        q_lat, kv_lat, kr = z[:, 1024:1280], z[:, 1280:1408], z[:, 1408:1536]
        ckv = _rms_norm(kv_lat, gkv_ref[...])
        qn = _rms_norm(q_lat, gq_ref[...])
        qc = _dot(qn.astype(BF16), wuq_ref[...])
        yield
        if latent:
            cd, shd, sld = cd_ref[rows, :], shd_ref[rows, :], sld_ref[rows, :]
            cm, shm, slm = cm_ref[rows, :], shm_ref[rows, :], slm_ref[rows, :]
            dq = _rope(dq, cd, shd, sld)
            dk = _rope(dk, cd, shd, sld)
            qc = _rope(qc, cm, shm, slm)
            kr = _rope(kr, cm, shm, slm)
        else:
            k32_ref[rows, :] = dk
            v32_ref[rows, :] = dv
            ckv32_ref[rows, :] = ckv
            kr32_ref[rows, :] = kr
        keys, vv_t = _mla_keys(ckv, kr, wukv_ref)
        yield
        qd_ref[rows, :] = (dq * q_scale_diff).astype(BF16)
        u_ref[0, rows, :] = u[:, :LANES]
        u_ref[1, rows, :] = u[:, LANES:]
        qcs = (qc * q_scale_mla).astype(BF16)
        for hh in range(MLA_HEADS):
            qc_ref[hh, rows, :] = qcs[:, hh * MLA_HEAD_PAD:(hh + 1) * MLA_HEAD_PAD]
        where = (0, rows) if latent else (sub, slice(None))
        put_keys(where, dk.astype(BF16), dv.T.astype(BF16), keys, vv_t)

    def all_new_tokens():
        running = [new_tokens(sub) for sub in range(x_ref.shape[0] // TOKEN_SUB)]
        while running:
            running = [g for g in running if next(g, "done") != "done"]

    if latent:
        i = pl.program_id(1)
        pl.when(i < n_new)(all_new_tokens)

        @pl.when(i == n_new)
        def _():
            past = ck_ref.shape[2]
            keys, vv_t = _mla_keys(cckv_ref[0, 0], ckr_ref[0, 0], wukv_ref)
            put_keys((0, slice(0, past)), ck_ref[0, 0].astype(BF16),
                     cv_ref[0, 0].T.astype(BF16), keys, vv_t)
            n_pad = kd_ref.shape[1] - past
            put_keys((0, slice(past, kd_ref.shape[1])),
                     jnp.zeros((n_pad, DIFF_W), BF16), jnp.zeros((DIFF_W, n_pad), BF16),
                     [jnp.zeros((n_pad, MLA_HEAD_PAD), BF16)] * MLA_HEADS,
                     jnp.zeros((MLA_W, n_pad), BF16))
    else:
        all_new_tokens()


def _prep_call(latent, layer, x, seq_len, mods3, wts, tables=None, caches=None):
    n_tok = x.shape[0]
    tm = TOKEN_BLOCK
    n_req = n_tok // seq_len
    n_new = seq_len // tm if latent else 1
    weights = [wts["w_in"], wts["gq"], wts["gkv"], wts["w_uq"], wts["w_ukv"]]
    if latent:
        grid = (n_req, n_new + 1)
        key_rows = (n_new + 1) * tm
        blk = lambda b, i: b * n_new + jnp.minimum(i, n_new - 1)
        kblk = lambda b, i: (b, i)
        key_block = (1, tm)
    else:
        assert tm % seq_len == 0 and TOKEN_SUB == seq_len
        grid = (n_tok // tm,)
        key_rows = seq_len
        blk = lambda i: i
        kblk = lambda i: (i, 0)
        key_block = (tm // seq_len, seq_len)
    row = _mod_row(latent, layer, seq_len, tm)
    tok = lambda w: pl.BlockSpec((tm, w), lambda *g: (blk(*g), 0))
    heads = lambda n, w: pl.BlockSpec((n, tm, w), lambda *g: (0, blk(*g), 0))
    keys2 = lambda w: pl.BlockSpec(key_block + (w,), lambda *g: kblk(*g) + (0,))
    keys3 = lambda n, w: pl.BlockSpec((key_block[0], n, key_block[1], w),
                                      lambda *g: (kblk(*g)[0], 0, kblk(*g)[1], 0))
    vals_t = lambda w: pl.BlockSpec((key_block[0], w, key_block[1]),
                                    lambda *g: (kblk(*g)[0], 0, kblk(*g)[1]))
    in_specs = ([tok(D_MODEL),
                 pl.BlockSpec((1, 1, D_MODEL), lambda *g: (row(blk(*g)), 0, 0)),
                 pl.BlockSpec((1, 1, D_MODEL), lambda *g: (row(blk(*g)) + 1, 0, 0))]
                + [_layer_spec(w, layer) for w in weights])
    args = [x, mods3, mods3] + weights
    out_shape = [jax.ShapeDtypeStruct((n_tok, DIFF_W), BF16),
                 jax.ShapeDtypeStruct((n_req, key_rows, DIFF_W), BF16),
                 jax.ShapeDtypeStruct((n_req, DIFF_W, key_rows), BF16),
                 jax.ShapeDtypeStruct((S5_W // LANES, n_tok, LANES), F32),
                 jax.ShapeDtypeStruct((MLA_HEADS, n_tok, MLA_HEAD_PAD), BF16),
                 jax.ShapeDtypeStruct((n_req, MLA_HEADS, key_rows, MLA_HEAD_PAD), BF16),
                 jax.ShapeDtypeStruct((n_req, MLA_W, key_rows), BF16)]
    out_specs = [tok(DIFF_W), keys2(DIFF_W), vals_t(DIFF_W), heads(S5_W // LANES, LANES),
                 heads(MLA_HEADS, MLA_HEAD_PAD), keys3(MLA_HEADS, MLA_HEAD_PAD), vals_t(MLA_W)]
    if latent:
        past = caches[0].shape[2]
        assert past < tm
        in_specs += [pl.BlockSpec((tm, LANES), lambda b, i: (jnp.minimum(i, n_new - 1), 0))] * 6
        in_specs += [pl.BlockSpec((1, 1, past, w), lambda b, i: (b, layer, 0, 0))
                     for w in (DIFF_W, DIFF_W, LANES, LANES)]
        args += list(tables) + list(caches)
    else:
        out_shape += [jax.ShapeDtypeStruct((n_tok, DIFF_W), F32),
                      jax.ShapeDtypeStruct((n_tok, DIFF_W), F32),
                      jax.ShapeDtypeStruct((n_tok, MLA_KV_RANK), F32),
                      jax.ShapeDtypeStruct((n_tok, LANES), F32)]
        out_specs += [tok(DIFF_W), tok(DIFF_W), tok(MLA_KV_RANK), tok(LANES)]
    body = functools.partial(_prep_kernel, latent=latent, n_new=n_new,
                             q_scale_diff=DIFF_HEAD_DIM ** -0.5 * LOG2E,
                             q_scale_mla=(MLA_NOPE + MLA_ROPE) ** -0.5 * LOG2E)
    return pl.pallas_call(
        body, grid=grid, in_specs=in_specs, out_specs=out_specs, out_shape=out_shape,
        compiler_params=_cparams(len(grid)),
        name="prep_lat" if latent else "prep_ctx",
    )(*args)


def _softmax_maps(n_maps, scores_fn, values_fn, emit_fn, s_refs):
    requests = range(len(s_refs) // 2)
    n_keys = s_refs[0].shape[0]
    units = n_keys // LANES
    n_parts = max(1, min(KEY_PARTS, units // KEY_PARTS))
    cuts = [LANES * (units * p // n_parts) for p in range(n_parts + 1)]
    key_parts = list(zip(cuts[:-1], cuts[1:]))

    def stage_a(r, i):
        m = None
        for lo, hi in key_parts:
            s = scores_fn(r, i, lo, hi)
            s_refs[2 * r + i % 2][lo:hi, :] = s
            pm = jnp.max(s, axis=0, keepdims=True)
            m = pm if m is None else jnp.maximum(m, pm)
        return m

    def stage_b(r, i, m):
        v = values_fn(r, i)
        pv, denom = None, None
        for lo, hi in key_parts:
            e = jnp.exp2(s_refs[2 * r + i % 2][lo:hi, :] - m)
            ps = jnp.sum(e, axis=0, keepdims=True)
            pp = _dot(v[:, lo:hi], e.astype(BF16))
            pv, denom = (pp, ps) if pv is None else (pv + pp, denom + ps)
        emit_fn(r, i, pv, denom)

    m_prev = [stage_a(r, 0) for r in requests]
    for i in range(1, n_maps):
        m_cur = [stage_a(r, i) for r in requests]
        for r in requests:
            stage_b(r, i - 1, m_prev[r])
        m_prev = m_cur
    for r in requests:
        stage_b(r, n_maps - 1, m_prev[r])


def _diff_lambda(lq1_ref, lk1_ref, lq2_ref, lk2_ref, lam_init):
    s1 = jnp.sum(lq1_ref[...] * lk1_ref[...], axis=-1, keepdims=True)
    s2 = jnp.sum(lq2_ref[...] * lk2_ref[...], axis=-1, keepdims=True)
    return jnp.exp(s1) - jnp.exp(s2) + lam_init


def _diff_attn_kernel(q_ref, k_ref, v_ref, lq1_ref, lk1_ref, lq2_ref, lk2_ref, g_ref, o_ref,
                      acc_ref, *s_refs, lam_init):
    tq = acc_ref.shape[2]
    lam = _diff_lambda(lq1_ref, lk1_ref, lq2_ref, lk2_ref, lam_init)
    lane = lax.broadcasted_iota(jnp.int32, (1, DIFF_W), 1)

    def scores(r, hc, lo_key, hi_key):
        lo = hc * DIFF_HEAD_DIM
        qmask = jnp.where((lane >= lo) & (lane < lo + DIFF_HEAD_DIM), 1.0, 0.0).astype(BF16)
        qm = q_ref[r * tq:(r + 1) * tq, :] * qmask
        return _dot_nt(k_ref[r, lo_key:hi_key, :], qm)

    def head_rows(hc):
        return slice((hc // 2) * DIFF_V_DIM, (hc // 2 + 1) * DIFF_V_DIM)

    def emit(r, hc, pv, denom):
        if hc % 2 == 0:
            acc_ref[r, head_rows(hc), :] = pv * (1.0 / denom)
        else:
            o = acc_ref[r, head_rows(hc), :] - pv * (lam / denom)
            ms = jnp.mean(o * o, axis=0, keepdims=True)
            acc_ref[r, head_rows(hc), :] = o * lax.rsqrt(ms + RMS_EPS)

    _softmax_maps(2 * DIFF_HEADS, scores, lambda r, hc: v_ref[r, head_rows(hc), :], emit, s_refs)
    for r in range(acc_ref.shape[0]):
        o_ref[r * tq:(r + 1) * tq, :] = (acc_ref[r].T * g_ref[...] * (1.0 - lam_init)).astype(BF16)


def _requests_per_step(n_req, nq):
    return math.gcd(n_req, SHORT_REQUESTS_PER_STEP) if nq == 1 else 1


def _diff_attn_call(layer, seq_len, n_keys, qd, kd, vd, lam_params, g_tiled):
    n_tok = qd.shape[0]
    tq = min(Q_BLOCK, seq_len)
    nq = seq_len // tq
    n_req = n_tok // seq_len
    rs = _requests_per_step(n_req, nq)
    lam_init = 0.8 - 0.6 * math.exp(-0.3 * layer)
    in_specs = ([pl.BlockSpec((rs * tq, DIFF_W), lambda b, i: (b * nq + i, 0)),
                 pl.BlockSpec((rs, n_keys, DIFF_W), lambda b, i: (b, 0, 0)),
                 pl.BlockSpec((rs, DIFF_W, n_keys), lambda b, i: (b, 0, 0))]
                + [_layer_spec(w, layer) for w in lam_params] + [_layer_spec(g_tiled, layer)])
    return pl.pallas_call(
        functools.partial(_diff_attn_kernel, lam_init=lam_init),
        grid=(n_req // rs, nq),
        in_specs=in_specs,
        out_specs=pl.BlockSpec((rs * tq, DIFF_W), lambda b, i: (b * nq + i, 0)),
        out_shape=jax.ShapeDtypeStruct((n_tok, DIFF_W), BF16),
        scratch_shapes=([pltpu.VMEM((rs, DIFF_W, tq), F32)]
                        + [pltpu.VMEM((n_keys, tq), F32)] * (2 * rs)),
        compiler_params=_cparams(2),
        name="diff_attn",
    )(qd, kd, vd, *lam_params, g_tiled)


def _mla_attn_kernel(q_ref, k_ref, v_ref, o_ref, acc_ref, *s_refs):
    tq = acc_ref.shape[2]

    def head_rows(h):
        return slice(h * MLA_V, (h + 1) * MLA_V)

    def scores(r, h, lo, hi):
        return _dot_nt(k_ref[r, h, lo:hi, :], q_ref[h, r * tq:(r + 1) * tq, :])

    def emit(r, h, pv, denom):
        acc_ref[r, head_rows(h), :] = pv * (1.0 / denom)

    _softmax_maps(MLA_HEADS, scores, lambda r, h: v_ref[r, head_rows(h), :], emit, s_refs)
    for r in range(acc_ref.shape[0]):
        o_ref[r * tq:(r + 1) * tq, :] = acc_ref[r].T.astype(BF16)


def _mla_attn_call(seq_len, n_keys, qc, kc, vm):
    n_tok = qc.shape[1]
    tq = min(Q_BLOCK, seq_len)
    nq = seq_len // tq
    n_req = n_tok // seq_len
    rs = _requests_per_step(n_req, nq)
    in_specs = [pl.BlockSpec((MLA_HEADS, rs * tq, MLA_HEAD_PAD), lambda b, i: (0, b * nq + i, 0)),
                pl.BlockSpec((rs, MLA_HEADS, n_keys, MLA_HEAD_PAD), lambda b, i: (b, 0, 0, 0)),
                pl.BlockSpec((rs, MLA_W, n_keys), lambda b, i: (b, 0, 0))]
    return pl.pallas_call(
        _mla_attn_kernel,
        grid=(n_req // rs, nq),
        in_specs=in_specs,
        out_specs=pl.BlockSpec((rs * tq, MLA_W), lambda b, i: (b * nq + i, 0)),
        out_shape=jax.ShapeDtypeStruct((n_tok, MLA_W), BF16),
        scratch_shapes=([pltpu.VMEM((rs, MLA_W, tq), F32)]
                        + [pltpu.VMEM((n_keys, tq), F32)] * (2 * rs)),
        compiler_params=_cparams(2),
        name="mla_attn",
    )(qc, kc, vm)


S5_CHUNK_STEPS = 64
S5_EPILOGUE_ROWS = 256


def _s5_kernel(*refs, steps, segmented):
    if segmented:
        (u_ref, bre_ref, bim_ref, cre_ref, cim_ref, lre_ref, lim_ref, ldt_ref, d_ref, wglu_ref,
         h0_ref, y_ref, up_ref, bur0_ref, bui0_ref, bur1_ref, bui1_ref, yacc_ref,
         abar_ref, bbar_ref, ct_ref, inr_ref, ini_ref) = refs
    else:
        (u_ref, bre_ref, bim_ref, cre_ref, cim_ref, lre_ref, lim_ref, ldt_ref, d_ref, wglu_ref,
         y_ref, fin_ref, up_ref, bur0_ref, bui0_ref, bur1_ref, bui1_ref, yacc_ref,
         abar_ref, bbar_ref, ct_ref) = refs
    bu_refs = ((bur0_ref, bui0_ref), (bur1_ref, bui1_ref))
    tc = S5_CHUNK_STEPS
    rows_c = tc * SUBLANES
    n_chunks = steps // tc
    yacc_ref[...] = jnp.zeros_like(yacc_ref)
    group_mask = (jnp.right_shift(lax.broadcasted_iota(jnp.int32, (S5_W, S5_N), 0), S5_CH_LOG2)
                  == jnp.right_shift(lax.broadcasted_iota(jnp.int32, (S5_W, S5_N), 1), S5_STATE_LOG2))

    def regroup(j, carry):
        dst = pl.ds(pl.multiple_of(j * SUBLANES, SUBLANES), SUBLANES)
        for half in range(S5_W // LANES):
            up_ref[dst, half * LANES:(half + 1) * LANES] = (
                u_ref[half, pl.ds(j, SUBLANES, stride=steps), :])
        return carry

    lax.fori_loop(0, steps, regroup, 0, unroll=8)

    @pl.when(pl.program_id(0) == 0)
    def _():
        for d in range(2):
            lam_re, lam_im = lre_ref[d:d + 1, :], lim_ref[d:d + 1, :]
            dt = jnp.exp(ldt_ref[d:d + 1, :])
            mag = jnp.exp(lam_re * dt)
            ang = lam_im * dt
            a_re, a_im = mag * jnp.cos(ang), mag * jnp.sin(ang)
            den = lam_re * lam_re + lam_im * lam_im
            n_re, n_im = a_re - 1.0, a_im
            f_re = (n_re * lam_re + n_im * lam_im) / den
            f_im = (n_im * lam_re - n_re * lam_im) / den

            def block_diagonal(rows16):
                dense = jnp.concatenate([rows16] * S5_GROUPS, axis=0)
                return jnp.where(group_mask, dense, 0.0).astype(BF16)

            abar_ref[2 * d:2 * d + 1, :] = a_re
            abar_ref[2 * d + 1:2 * d + 2, :] = a_im
            bbar_ref[d] = jnp.concatenate(
                [block_diagonal(f_re * bre_ref[d] - f_im * bim_ref[d]),
                 block_diagonal(f_re * bim_ref[d] + f_im * bre_ref[d])], axis=1)
            ct_ref[d] = jnp.concatenate(
                [block_diagonal(cre_ref[d]), block_diagonal(-cim_ref[d])], axis=1)

    for d in range(2):
        a_re, a_im = abar_ref[2 * d:2 * d + 1, :], abar_ref[2 * d + 1:2 * d + 2, :]
        bbar, c_t = bbar_ref[d], ct_ref[d]
        ar8 = jnp.broadcast_to(a_re, (SUBLANES, S5_N))
        ai8 = jnp.broadcast_to(a_im, (SUBLANES, S5_N))

        def run_pass(init, store, d=d, bbar=bbar, c_t=c_t, ar8=ar8, ai8=ai8):
            def chunk_rows(ci):
                c = jnp.minimum(ci, n_chunks - 1)
                c = c if d == 0 else n_chunks - 1 - c
                return pl.ds(pl.multiple_of(c * rows_c, rows_c), rows_c)

            def project_in(ci, slot):
                bu = _dot(up_ref[chunk_rows(ci), :].astype(BF16), bbar)
                bu_refs[slot][0][...] = bu[:, :S5_N]
                bu_refs[slot][1][...] = bu[:, S5_N:]

            def scan(slot, hc):
                bur_ref, bui_ref = bu_refs[slot]
                hr, hi = hc
                for jj in range(tc):
                    j = jj if d == 0 else tc - 1 - jj
                    r = slice(j * SUBLANES, (j + 1) * SUBLANES)
                    hr, hi = (ar8 * hr - ai8 * hi + bur_ref[r, :],
                              ar8 * hi + ai8 * hr + bui_ref[r, :])
                    if store:
                        bur_ref[r, :] = hr
                        bui_ref[r, :] = hi
                return hr, hi

            def project_out(ci, slot):
                bur_ref, bui_ref = bu_refs[slot]
                h = jnp.concatenate([bur_ref[...].astype(BF16), bui_ref[...].astype(BF16)], axis=1)
                yc = _dot_nt(h, c_t)
                for half in range(S5_W // LANES):
                    yacc_ref[half, chunk_rows(ci), :] += yc[:, half * LANES:(half + 1) * LANES]

            def chunk_pair(cp, carry):
                for slot in range(2):
                    ci = 2 * cp + slot
                    project_in(ci + 1, 1 - slot)
                    carry = scan(slot, carry)
                    if store:
                        project_out(ci, slot)
                return carry

            project_in(0, 0)
            return lax.fori_loop(0, n_chunks // 2, chunk_pair, init)

        zeros = (jnp.zeros((SUBLANES, S5_N), F32), jnp.zeros((SUBLANES, S5_N), F32))
        if segmented:
            f_r, f_i = run_pass(zeros, False)
            p_re, p_im = a_re, a_im
            for _ in range(int(math.log2(steps))):
                p_re, p_im = p_re * p_re - p_im * p_im, 2.0 * p_re * p_im
            c_r, c_i = h0_ref[0, d, 0:1, :], h0_ref[0, d, 1:2, :]
            order = range(SUBLANES) if d == 0 else range(SUBLANES - 1, -1, -1)
            for s in order:
                inr_ref[s:s + 1, :] = c_r
                ini_ref[s:s + 1, :] = c_i
                c_r, c_i = (f_r[s:s + 1, :] + p_re * c_r - p_im * c_i,
                            f_i[s:s + 1, :] + p_re * c_i + p_im * c_r)
            run_pass((inr_ref[...], ini_ref[...]), True)
        else:
            f_r, f_i = run_pass(zeros, True)
            fin_ref[0, d, 0] = f_r
            fin_ref[0, d, 1] = f_i

    ep = S5_EPILOGUE_ROWS
    per_chain = steps // ep

    def epilogue(e, carry):
        chain, jc = e // per_chain, e % per_chain
        rows = pl.ds(pl.multiple_of(e * ep, ep), ep)
        src = pl.ds(jc * ep * SUBLANES + chain, ep, stride=SUBLANES)
        halves = range(S5_W // LANES)
        acc = jnp.concatenate([yacc_ref[half, src, :] for half in halves], axis=1)
        u = jnp.concatenate([u_ref[half, rows, :] for half in halves], axis=1)
        y = acc + u * d_ref[...]
        y = jax.nn.gelu(y, approximate=True)
        y_ref[rows, :] = y * jax.nn.sigmoid(_dot(y.astype(BF16), wglu_ref[...]))
        return carry

    lax.fori_loop(0, SUBLANES * per_chain, epilogue, 0)


def _s5_call(layer, u, steps, s5w, h0=None):
    n_tok = u.shape[1]
    n_rows = steps * SUBLANES
    nb = n_tok // n_rows
    segmented = h0 is not None
    assert steps % (2 * S5_CHUNK_STEPS) == 0 and steps % S5_EPILOGUE_ROWS == 0
    assert steps & (steps - 1) == 0
    consts = [s5w["b_re"], s5w["b_im"], s5w["c_re"], s5w["c_im"], s5w["lam_re"], s5w["lam_im"],
              s5w["log_dt"], s5w["d"], s5w["w_glu"]]
    in_specs = ([pl.BlockSpec((S5_W // LANES, n_rows, LANES), lambda b: (0, b, 0))]
                + [_layer_spec(c, layer, single_buffer=True) for c in consts])
    args = [u] + consts
    out_shape = [jax.ShapeDtypeStruct((n_tok, S5_W), F32)]
    out_specs = [pl.BlockSpec((n_rows, S5_W), lambda b: (b, 0))]
    scratch = ([pltpu.VMEM((n_rows, S5_W), F32)]
               + [pltpu.VMEM((S5_CHUNK_STEPS * SUBLANES, S5_N), F32)] * 4
               + [pltpu.VMEM((S5_W // LANES, n_rows, LANES), F32)]
               + [pltpu.VMEM((4, S5_N), F32), pltpu.VMEM((2, S5_W, 2 * S5_N), BF16),
                  pltpu.VMEM((2, S5_W, 2 * S5_N), BF16)])
    if segmented:
        in_specs.append(pl.BlockSpec((1, None, 2, 2, S5_N), lambda b: (b, layer, 0, 0, 0)))
        args.append(h0)
        scratch += [pltpu.VMEM((SUBLANES, S5_N), F32), pltpu.VMEM((SUBLANES, S5_N), F32)]
    else:
        out_shape.append(jax.ShapeDtypeStruct((nb, 2, 2, SUBLANES, S5_N), F32))
        out_specs.append(pl.BlockSpec((1, 2, 2, SUBLANES, S5_N), lambda b: (b, 0, 0, 0, 0)))
    outs = pl.pallas_call(
        functools.partial(_s5_kernel, steps=steps, segmented=segmented),
        grid=(nb,), in_specs=in_specs, out_specs=out_specs, out_shape=out_shape,
        scratch_shapes=scratch, compiler_params=_cparams(1),
        name="s5_lat" if segmented else "s5_ctx",
    )(*args)
    return (outs[0], None) if segmented else (outs[0], outs[1])


def _post_kernel(x_ref, a_ref, s_ref, m_ref, g1_ref, sh2_ref, sc2_ref, g2_ref,
                 wo_ref, wgu_ref, wd_ref, l1g_ref, l1b_ref, l2g_ref, l2b_ref, o_ref):
    subs = [slice(s * TOKEN_SUB, (s + 1) * TOKEN_SUB) for s in range(x_ref.shape[0] // TOKEN_SUB)]
    mixed = [jnp.concatenate([a_ref[rows, :], s_ref[rows, :].astype(BF16), m_ref[rows, :]], axis=1)
             for rows in subs]
    mix = [_dot(mi, wo_ref[...]) for mi in mixed]
    x1 = [_layer_norm(ALPHA * x_ref[rows, :] + g1_ref[0] * mx, l1g_ref[...], l1b_ref[...])
          for rows, mx in zip(subs, mix)]
    hb = [(v * (1.0 + sc2_ref[0]) + sh2_ref[0]).astype(BF16) for v in x1]
    f = []
    for h in hb:
        gu = _dot(h, wgu_ref[...])
        gate, up = gu[:, :FFN_HIDDEN], gu[:, FFN_HIDDEN:]
        f.append(_dot((gate * jax.nn.sigmoid(gate) * up).astype(BF16), wd_ref[...]))
    for s, rows in enumerate(subs):
        o_ref[rows, :] = _layer_norm(ALPHA * x1[s] + g2_ref[0] * f[s], l2g_ref[...], l2b_ref[...])


def _post_call(latent, layer, seq_len, x, attn_d, y_s5, attn_m, mods3, wts):
    n_tok = x.shape[0]
    tm = TOKEN_BLOCK
    row = _mod_row(latent, layer, seq_len, tm)
    tok = lambda w: pl.BlockSpec((tm, w), lambda i: (i, 0))
    mod = lambda k: pl.BlockSpec((1, 1, D_MODEL), lambda i: (row(i) + k, 0, 0))
    weights = [wts["w_out"], wts["w_gate_up"], wts["w_down"],
               wts["ln1_g"], wts["ln1_b"], wts["ln2_g"], wts["ln2_b"]]
    w_specs = [_layer_spec(w, layer, single_buffer=True) for w in weights]
    return pl.pallas_call(
        _post_kernel,
        grid=(n_tok // tm,),
        in_specs=[tok(D_MODEL), tok(DIFF_W), tok(S5_W), tok(MLA_W), mod(2), mod(3), mod(4), mod(5)]
                 + w_specs,
        out_specs=tok(D_MODEL),
        out_shape=jax.ShapeDtypeStruct((n_tok, D_MODEL), F32),
        compiler_params=_cparams(1),
        name="post",
    )(x, attn_d, y_s5, attn_m, mods3, mods3, mods3, mods3, *weights)


def _rope_tables(length):
    rows = length // GRID_W
    row = np.repeat(np.arange(rows, dtype=np.float64), GRID_W)
    col = np.tile(np.arange(GRID_W, dtype=np.float64), rows)
    n_freq = DIFF_HEAD_DIM // 4
    inv = (ROPE_BASE ** (-np.arange(n_freq, dtype=np.float32) / np.float32(n_freq))).astype(np.float32)
    ang = np.concatenate([(row[:, None] * inv).astype(np.float32),
                          (col[:, None] * inv).astype(np.float32)], -1).astype(np.float64)
    cos, sin = np.cos(ang).astype(np.float32), np.sin(ang).astype(np.float32)
    zero = np.zeros_like(sin)
    cos32 = np.concatenate([cos, cos], -1)
    hi32 = np.concatenate([zero, sin], -1)
    lo32 = np.concatenate([-sin, zero], -1)
    diff = tuple(np.tile(t, (1, LANES // 32)) for t in (cos32, hi32, lo32))
    ones64, zeros64 = np.ones((length, 64), np.float32), np.zeros((length, 64), np.float32)
    ones32, zeros32 = np.ones((length, 32), np.float32), np.zeros((length, 32), np.float32)
    mla = (np.concatenate([ones64, cos32, ones32], -1),
           np.concatenate([zeros64, hi32, zeros32], -1),
           np.concatenate([zeros64, lo32, zeros32], -1))
    return tuple(jnp.asarray(t) for t in diff + mla)


def _stacked_weights(p):
    w_in = p["w_in"]
    krope_cols = jnp.pad(w_in[:, :, 1408:1440], ((0, 0), (0, 0), (64, 32)))
    w_in_ext = jnp.concatenate([w_in[:, :, :1408], krope_cols], axis=2).astype(BF16)
    w_uq = jnp.pad(p["mla_w_uq"].reshape(DEPTH, MLA_Q_RANK, MLA_HEADS, MLA_NOPE + MLA_ROPE),
                   ((0, 0), (0, 0), (0, 0), (0, MLA_HEAD_PAD - MLA_NOPE - MLA_ROPE)))
    w_uk = jnp.pad(p["mla_w_uk"].reshape(DEPTH, MLA_KV_RANK, MLA_HEADS, MLA_NOPE),
                   ((0, 0), (0, 0), (0, 0), (0, MLA_HEAD_PAD - MLA_NOPE)))
    prep = {
        "w_in": w_in_ext,
        "gq": p["mla_q_norm_g"].reshape(DEPTH, 1, MLA_Q_RANK),
        "gkv": p["mla_kv_norm_g"].reshape(DEPTH, 1, MLA_KV_RANK),
        "w_uq": w_uq.reshape(DEPTH, MLA_Q_RANK, MLA_HEADS * MLA_HEAD_PAD).astype(BF16),
        "w_ukv": jnp.concatenate(
            [w_uk.reshape(DEPTH, MLA_KV_RANK, MLA_HEADS * MLA_HEAD_PAD), p["mla_w_uv"]],
            axis=2).astype(BF16),
    }
    lam_params = [p[n].reshape(DEPTH, 1, DIFF_HEAD_DIM)
                  for n in ("diff_lq1", "diff_lk1", "diff_lq2", "diff_lk2")]
    g_tiled = jnp.tile(p["diff_norm_g"], (1, DIFF_HEADS)).reshape(DEPTH, 1, DIFF_W)
    s5w = {
        "b_re": jnp.transpose(p["s5_b_re"], (0, 1, 4, 2, 3)).reshape(DEPTH, 2, S5_CH, S5_N),
        "b_im": jnp.transpose(p["s5_b_im"], (0, 1, 4, 2, 3)).reshape(DEPTH, 2, S5_CH, S5_N),
        "c_re": jnp.transpose(p["s5_c_re"], (0, 1, 3, 2, 4)).reshape(DEPTH, 2, S5_CH, S5_N),
        "c_im": jnp.transpose(p["s5_c_im"], (0, 1, 3, 2, 4)).reshape(DEPTH, 2, S5_CH, S5_N),
        "lam_re": p["s5_lam_re"].reshape(DEPTH, 2, S5_N),
        "lam_im": p["s5_lam_im"].reshape(DEPTH, 2, S5_N),
        "log_dt": jnp.repeat(p["s5_log_dt"], S5_STATE, axis=-1),
        "d": p["s5_d"].reshape(DEPTH, 1, S5_W),
        "w_glu": p["s5_w_glu"].astype(BF16),
    }
    post = {
        "w_out": p["w_out"].astype(BF16),
        "w_gate_up": jnp.concatenate([p["ffn_w_gate"], p["ffn_w_up"]], axis=2).astype(BF16),
        "w_down": p["ffn_w_down"].astype(BF16),
        "ln1_g": p["ln1_g"].reshape(DEPTH, 1, D_MODEL), "ln1_b": p["ln1_b"].reshape(DEPTH, 1, D_MODEL),
        "ln2_g": p["ln2_g"].reshape(DEPTH, 1, D_MODEL), "ln2_b": p["ln2_b"].reshape(DEPTH, 1, D_MODEL),
    }
    return prep, lam_params, g_tiled, s5w, post


def kernel(x_prompt, x_sample, c, cache_diff_k, cache_diff_v, cache_mla_ckv, cache_mla_krope, state_s5, c_ctx, w_ada, b_ada, w_in, w_out, diff_lq1, diff_lk1, diff_lq2, diff_lk2, diff_norm_g, s5_lam_re, s5_lam_im, s5_log_dt, s5_b_re, s5_b_im, s5_c_re, s5_c_im, s5_d, s5_w_glu, mla_q_norm_g, mla_w_uq, mla_kv_norm_g, mla_w_uk, mla_w_uv, ln1_g, ln1_b, ln2_g, ln2_b, ffn_w_gate, ffn_w_up, ffn_w_down):
    p = dict(w_in=w_in, w_out=w_out, diff_lq1=diff_lq1, diff_lk1=diff_lk1, diff_lq2=diff_lq2,
             diff_lk2=diff_lk2, diff_norm_g=diff_norm_g, s5_lam_re=s5_lam_re, s5_lam_im=s5_lam_im,
             s5_log_dt=s5_log_dt, s5_b_re=s5_b_re, s5_b_im=s5_b_im, s5_c_re=s5_c_re, s5_c_im=s5_c_im,
             s5_d=s5_d, s5_w_glu=s5_w_glu, mla_q_norm_g=mla_q_norm_g, mla_w_uq=mla_w_uq,
             mla_kv_norm_g=mla_kv_norm_g, mla_w_uk=mla_w_uk, mla_w_uv=mla_w_uv, ln1_g=ln1_g,
             ln1_b=ln1_b, ln2_g=ln2_g, ln2_b=ln2_b, ffn_w_gate=ffn_w_gate, ffn_w_up=ffn_w_up,
             ffn_w_down=ffn_w_down)
    bsz, seq, _ = x_prompt.shape
    dec_b, dec_seq, _ = x_sample.shape
    past = cache_diff_k.shape[2]
    assert bsz % SUBLANES == 0 and dec_seq % SUBLANES == 0

    cond = jnp.concatenate([c_ctx[None, :], c, jnp.zeros((ADA_ROWS - 1 - dec_b, D_MODEL), F32)], 0)
    mods3 = _ada_call(cond, w_ada, b_ada).reshape(DEPTH * ADA_ROWS * 6, 1, D_MODEL)

    tables = _rope_tables(dec_seq)
    caches = (cache_diff_k.reshape(dec_b, DEPTH, past, DIFF_W),
              cache_diff_v.reshape(dec_b, DEPTH, past, DIFF_W),
              cache_mla_ckv,
              jnp.pad(cache_mla_krope, ((0, 0), (0, 0), (0, 0), (MLA_NOPE, LANES - MLA_NOPE - MLA_ROPE))))
    h0_all = jnp.moveaxis(state_s5, -1, 3).reshape(dec_b, DEPTH, 2, 2, S5_N)

    y_ctx = x_prompt.reshape(bsz * seq, D_MODEL)
    y_lat = x_sample.reshape(dec_b * dec_seq, D_MODEL)
    new_k, new_v, new_ckv, new_kr, new_st = [], [], [], [], []
    prep_w, lam_params, g_tiled, s5w, post_w = _stacked_weights(p)
    for l in range(DEPTH):
        qd, kd, vd, u, qc, kc, vm, k32, v32, ckv32, kr32 = _prep_call(
            False, l, y_ctx, seq, mods3, prep_w)
        attn_d = _diff_attn_call(l, seq, seq, qd, kd, vd, lam_params, g_tiled)
        attn_m = _mla_attn_call(seq, seq, qc, kc, vm)
        y_s5, fin = _s5_call(l, u, seq, s5w)
        y_ctx = _post_call(False, l, seq, y_ctx, attn_d, y_s5, attn_m, mods3, post_w)
        new_k.append(k32.reshape(bsz, seq, DIFF_HEADS, 2 * DIFF_HEAD_DIM))
        new_v.append(v32.reshape(bsz, seq, DIFF_HEADS, DIFF_V_DIM))
        new_ckv.append(ckv32.reshape(bsz, seq, MLA_KV_RANK))
        new_kr.append(kr32.reshape(bsz, seq, LANES)[:, :, MLA_NOPE:MLA_NOPE + MLA_ROPE])
        st = jnp.transpose(fin, (0, 3, 1, 4, 2)).reshape(bsz, 2, S5_GROUPS, S5_STATE, 2)
        new_st.append(st)

        qd, kd, vd, u, qc, kc, vm = _prep_call(True, l, y_lat, dec_seq, mods3, prep_w, tables, caches)
        attn_d = _diff_attn_call(l, dec_seq, dec_seq + past, qd, kd, vd, lam_params, g_tiled)
        attn_m = _mla_attn_call(dec_seq, dec_seq + past, qc, kc, vm)
        y_s5, _ = _s5_call(l, u, dec_seq // SUBLANES, s5w, h0_all)
        y_lat = _post_call(True, l, dec_seq, y_lat, attn_d, y_s5, attn_m, mods3, post_w)

    return (y_ctx.reshape(bsz, seq, D_MODEL), y_lat.reshape(dec_b, dec_seq, D_MODEL),
            jnp.stack(new_k, 1), jnp.stack(new_v, 1), jnp.stack(new_ckv, 1),
            jnp.stack(new_kr, 1), jnp.stack(new_st, 1))
```
